```python
import math
import functools
import jax
import jax.numpy as jnp
from jax import lax
import numpy as np


D_MODEL = 2048
BATCH = 1
SEQ = 16384
DEPTH = 2

GRID_W = 64
CTX_LEN = 256
NA_DH = 128
NA_W = D_MODEL // 2
NA_H = NA_W // NA_DH
NA_KR = 8
NA_KC = 16
ML_DH = 128
ML_W = D_MODEL // 4
ML_H = ML_W // ML_DH
ML_CHUNK = 64
SSM_P = 64
SSM_W = D_MODEL // 4
SSM_H = SSM_W // SSM_P
SSM_G = 2
SSM_N = 128
SSM_CONV = 4
CONV_PAD_L = SSM_CONV // 2
CONV_PAD_R = SSM_CONV - 1 - CONV_PAD_L
SSM_CHUNK = 128
CONV_CH = SSM_W + 2 * SSM_G * SSM_N
MIX_W = NA_W + ML_W + SSM_W
IN_COLS = 3 * NA_W + 4 * ML_W + 4 * ML_H + SSM_W + CONV_CH + 2 * SSM_H
MOE_GROUPS = 4
MOE_PER_GROUP = 8
N_EXPERTS = MOE_GROUPS * MOE_PER_GROUP
MOE_TOPK = 2
MOE_FF = D_MODEL // 2
MOE_BLOCK = 128
ROPE_THETA = 10000.0
EPS = 1e-6

kernel_name = 'hybrid_na_mlstm_ssd_hmoe_block'

F32 = jnp.float32


def rms_norm(x, w):
    xf = x.astype(F32)
    y = xf * lax.rsqrt(jnp.mean(xf * xf, axis=-1, keepdims=True) + EPS)
    return (y * w.astype(F32)).astype(x.dtype)


def to_chunks(a, size):
    b, t = a.shape[0], a.shape[1]
    return jnp.moveaxis(a.reshape((b, t // size, size) + a.shape[2:]), 1, 0)


def from_chunks(a):
    nc, b, size = a.shape[0], a.shape[1], a.shape[2]
    return jnp.moveaxis(a, 0, 1).reshape((b, nc * size) + a.shape[3:])


def adaln_params(cond, w_mod, b_mod):
    m = jax.nn.silu(cond.astype(F32)) @ w_mod.astype(F32) + b_mod.astype(F32)
    return [p[:, None, :].astype(cond.dtype) for p in jnp.split(m, 6, axis=-1)]


def axial_rope(x, pos_r, pos_c):
    half = x.shape[-1] // 2
    nf = half // 2
    inv = ROPE_THETA ** (-jnp.arange(nf, dtype=F32) / nf)

    def rotate(xp, pos):
        ang = pos.astype(F32)[:, None] * inv
        cos = jnp.cos(ang)[None, :, None, :].astype(x.dtype)
        sin = jnp.sin(ang)[None, :, None, :].astype(x.dtype)
        x1, x2 = xp[..., :nf], xp[..., nf:]
        return jnp.concatenate([x1 * cos - x2 * sin, x1 * sin + x2 * cos], axis=-1)

    return jnp.concatenate([rotate(x[..., :half], pos_r), rotate(x[..., half:], pos_c)], axis=-1)


def softmax_attention(q, k, v):
    s = jnp.einsum('bqhd,bkhd->bhqk', q, k).astype(F32) * q.shape[-1] ** -0.5
    p = jax.nn.softmax(s, axis=-1).astype(v.dtype)
    return jnp.einsum('bhqk,bkhd->bqhd', p, v).reshape(q.shape[0], q.shape[1], -1)


def neighborhood_attention(q, k, v, kc, vc, rpb):
    b, t, nh, dh = q.shape
    rows = t // GRID_W
    kr = min(NA_KR, rows)
    qg, kg, vg = (a.reshape(b, rows, GRID_W, nh, dh) for a in (q, k, v))
    r_idx = jnp.arange(rows)
    row_start = jnp.clip(r_idx - kr // 2, 0, rows - kr)
    c_idx = jnp.arange(GRID_W)
    col_win = jnp.clip(c_idx - NA_KC // 2, 0, GRID_W - NA_KC)[:, None] + jnp.arange(NA_KC)
    col_off = col_win - c_idx[:, None] + NA_KC - 1
    scale = dh ** -0.5
    n_win = kr * NA_KC

    def row_block(args):
        q_row, rs, ri = args
        kw = lax.dynamic_slice_in_dim(kg, rs, kr, axis=1)[:, :, col_win]
        vw = lax.dynamic_slice_in_dim(vg, rs, kr, axis=1)[:, :, col_win]
        row_off = rs + jnp.arange(kr) - ri + NA_KR - 1
        bias = rpb[:, row_off[None, :, None], col_off[:, None, :]].astype(F32)
        s_win = jnp.einsum('bqhd,brqchd->bhqrc', q_row, kw).astype(F32) * scale + bias
        s_ctx = jnp.einsum('bqhd,bkhd->bhqk', q_row, kc).astype(F32) * scale
        s = jnp.concatenate([s_win.reshape(b, nh, GRID_W, n_win), s_ctx], axis=-1)
        p = jax.nn.softmax(s, axis=-1).astype(v.dtype)
        p_win = p[..., :n_win].reshape(b, nh, GRID_W, kr, NA_KC)
        return (jnp.einsum('bhqrc,brqchd->bqhd', p_win, vw)
                + jnp.einsum('bhqk,bkhd->bqhd', p[..., n_win:], vc))

    out = lax.map(row_block, (jnp.moveaxis(qg, 1, 0), row_start, r_idx))
    return jnp.moveaxis(out, 0, 1).reshape(b, t, nh * dh)


def mlstm_chunk_scan(q, k, v, ig, lf, state):
    size = ML_CHUNK
    mask = jnp.tril(jnp.ones((size, size), bool))

    def step(carry, inp):
        cmat, nvec, m = carry
        qc, kc, vc, igc, lfc = inp
        bt = jnp.cumsum(lfc, axis=1).transpose(0, 2, 1)
        it = igc.transpose(0, 2, 1)
        log_d = jnp.where(mask, bt[..., :, None] - bt[..., None, :] + it[..., None, :], -jnp.inf)
        inter = bt + m[..., None]
        m_t = jnp.maximum(jnp.max(log_d, axis=-1), inter)
        wts = jnp.einsum('blhd,bshd->bhls', qc, kc) * jnp.exp(log_d - m_t[..., None])
        sc = jnp.exp(inter - m_t)
        num = (jnp.einsum('bhls,bshd->blhd', wts, vc)
               + jnp.einsum('bhed,blhd->blhe', cmat, qc) * sc.transpose(0, 2, 1)[..., None])
        den = jnp.sum(wts, axis=-1) + jnp.einsum('bhd,blhd->bhl', nvec, qc) * sc
        den = jnp.maximum(jnp.abs(den), jnp.exp(-m_t))
        h = num / den.transpose(0, 2, 1)[..., None]
        b_last = bt[..., -1]
        tail = b_last[..., None] - bt + it
        m_new = jnp.maximum(b_last + m, jnp.max(tail, axis=-1))
        wgt = jnp.exp(tail - m_new[..., None])
        decay = jnp.exp(b_last + m - m_new)
        cmat = decay[..., None, None] * cmat + jnp.einsum('bhs,bshe,bshd->bhed', wgt, vc, kc)
        nvec = decay[..., None] * nvec + jnp.einsum('bhs,bshd->bhd', wgt, kc)
        return (cmat, nvec, m_new), h

    state, hs = lax.scan(step, state, tuple(to_chunks(a, size) for a in (q, k, v, ig, lf)))
    return from_chunks(hs), state


def ssd_chunk_scan(x, dt, bm, cm, state, a):
    size = SSM_CHUNK
    mask = jnp.tril(jnp.ones((size, size), bool))

    def step(s, inp):
        xc, dtc, bc, cc = inp
        la = jnp.cumsum(dtc * a, axis=1).transpose(0, 2, 1)
        decay = jnp.exp(jnp.where(mask, la[..., :, None] - la[..., None, :], -jnp.inf))
        xdt = xc * dtc[..., None]
        g = jnp.einsum('blhn,bshn->bhls', cc, bc) * decay
        y = (jnp.einsum('bhls,bshp->blhp', g, xdt)
             + jnp.einsum('blhn,bhpn->blhp', cc, s) * jnp.exp(la).transpose(0, 2, 1)[..., None])
        tail = jnp.exp(la[..., -1:] - la)
        s = s * jnp.exp(la[..., -1])[..., None, None] + jnp.einsum('bhs,bshn,bshp->bhpn', tail, bc, xdt)
        return s, y

    state, ys = lax.scan(step, state, tuple(to_chunks(t, size) for t in (x, dt, bm, cm)))
    return from_chunks(ys), state


def bidirectional_scan(fwd_fn, bwd_fn, lat_f, lat_b, ctx_f, ctx_b, state0):
    flip = lambda args: tuple(jnp.flip(t, axis=1) for t in args)
    yc_f, s_f = fwd_fn(*ctx_f, state0)
    yc_b, s_b = bwd_fn(*flip(ctx_b), state0)
    y_f, _ = fwd_fn(*lat_f, s_f)
    y_b, _ = bwd_fn(*flip(lat_b), s_b)
    return y_f + jnp.flip(y_b, axis=1), yc_f + jnp.flip(yc_b, axis=1)


def mlstm_mixer(lat, ctx, pos_r, pos_c, gate_b, norm_w, need_ctx):
    def prep(parts, rotary):
        mq, mk, mv, mo, mg = parts
        b, t = mq.shape[0], mq.shape[1]
        q, k, v = (a.astype(F32).reshape(b, t, ML_H, ML_DH) for a in (mq, mk, mv))
        if rotary:
            q = axial_rope(q, pos_r, pos_c)
            k = axial_rope(k, pos_r, pos_c)
        k = k * ML_DH ** -0.5
        g = mg.astype(F32).reshape(b, t, 4, ML_H) + gate_b.astype(F32)
        fwd = (q, k, v, g[:, :, 0], jax.nn.log_sigmoid(g[:, :, 1]))
        bwd = (q, k, v, g[:, :, 2], jax.nn.log_sigmoid(g[:, :, 3]))
        return fwd, bwd, mo

    def finish(hs, mo):
        mu = jnp.mean(hs, axis=-1, keepdims=True)
        var = jnp.mean(jnp.square(hs - mu), axis=-1, keepdims=True)
        hn = ((hs - mu) * lax.rsqrt(var + EPS)).reshape(mo.shape[0], mo.shape[1], ML_W) * norm_w.astype(F32)
        return (jax.nn.sigmoid(mo.astype(F32)) * hn).astype(mo.dtype)

    lat_f, lat_b, mo = prep(lat, True)
    ctx_f, ctx_b, mo_c = prep(ctx, False)
    b = mo.shape[0]
    state0 = (jnp.zeros((b, ML_H, ML_DH, ML_DH), F32), jnp.zeros((b, ML_H, ML_DH), F32), jnp.zeros((b, ML_H), F32))
    hl, hc = bidirectional_scan(mlstm_chunk_scan, mlstm_chunk_scan, lat_f, lat_b, ctx_f, ctx_b, state0)
    return finish(hl, mo), (finish(hc, mo_c) if need_ctx else None)


def depthwise_conv(x, w, bias):
    y = lax.conv_general_dilated(x, w[:, None, :], window_strides=(1,), padding=[(CONV_PAD_L, CONV_PAD_R)],
                                 dimension_numbers=('NWC', 'WIO', 'NWC'), feature_group_count=x.shape[-1])
    return y + bias


def ssd_mixer(lat, ctx, conv_w, conv_b, dt_bias, a_log, d_skip, norm_w, need_ctx):
    def prep(parts):
        z, xbc, dtr = parts
        b, t = z.shape[0], z.shape[1]
        xbc = jax.nn.silu(depthwise_conv(xbc, conv_w, conv_b)).astype(F32)
        xs, bm, cm = jnp.split(xbc, [SSM_W, SSM_W + SSM_G * SSM_N], axis=-1)
        xs = xs.reshape(b, t, SSM_H, SSM_P)
        expand = lambda m: jnp.repeat(m.reshape(b, t, SSM_G, SSM_N), SSM_H // SSM_G, axis=2)
        bm, cm = expand(bm), expand(cm)
        dt = jax.nn.softplus(dtr.astype(F32).reshape(b, t, 2, SSM_H) + dt_bias.astype(F32))
        return (xs, dt[:, :, 0], bm, cm), (xs, dt[:, :, 1], bm, cm), z, xs

    def finish(y, xs, z):
        b, t = z.shape[0], z.shape[1]
        y = (y + d_skip.astype(F32)[:, None] * xs).reshape(b, t, SSM_W) * jax.nn.silu(z.astype(F32))
        yg = y.reshape(b, t, SSM_G, SSM_W // SSM_G)
        yg = yg * lax.rsqrt(jnp.mean(yg * yg, axis=-1, keepdims=True) + EPS)
        return (yg.reshape(b, t, SSM_W) * norm_w.astype(F32)).astype(z.dtype)

    lat_f, lat_b, z, xs = prep(lat)
    ctx_f, ctx_b, z_c, xs_c = prep(ctx)
    a = -jnp.exp(a_log.astype(F32))
    fwd = functools.partial(ssd_chunk_scan, a=a[0])
    bwd = functools.partial(ssd_chunk_scan, a=a[1])
    state0 = jnp.zeros((z.shape[0], SSM_H, SSM_P, SSM_N), F32)
    y, yc = bidirectional_scan(fwd, bwd, lat_f, lat_b, ctx_f, ctx_b, state0)
    return finish(y, xs, z), (finish(yc, xs_c, z_c) if need_ctx else None)


def hybrid_mixer(h, hc, w_in, w_out, na_rpb, ml_gate_b, ml_norm_w, conv_w, conv_b, dt_bias, a_log, d_skip,
                 ssm_norm_w, need_ctx):
    b, t, _ = h.shape
    t_idx = jnp.arange(t)
    pos_r, pos_c = t_idx // GRID_W, t_idx % GRID_W
    sizes = (NA_W, NA_W, NA_W, ML_W, ML_W, ML_W, ML_W, 4 * ML_H, SSM_W, CONV_CH, 2 * SSM_H)
    cuts = [int(s) for s in np.cumsum(sizes)[:-1]]
    lat = jnp.split(h @ w_in, cuts, axis=-1)
    ctx = jnp.split(hc @ w_in, cuts, axis=-1)
    heads = lambda a: a.reshape(a.shape[0], a.shape[1], NA_H, NA_DH)
    q, k, v = (heads(a) for a in lat[0:3])
    qc, kc, vc = (heads(a) for a in ctx[0:3])
    o_na = neighborhood_attention(q, k, v, kc, vc, na_rpb)
    o_ml, o_ml_c = mlstm_mixer(lat[3:8], ctx[3:8], pos_r, pos_c, ml_gate_b, ml_norm_w, need_ctx)
    o_ss, o_ss_c = ssd_mixer(lat[8:11], ctx[8:11], conv_w, conv_b, dt_bias, a_log, d_skip, ssm_norm_w, need_ctx)
    y = jnp.concatenate([o_na, o_ml, o_ss], axis=-1) @ w_out
    if not need_ctx:
        return y, None
    o_na_c = softmax_attention(qc, kc, vc)
    yc = jnp.concatenate([o_na_c, o_ml_c, o_ss_c], axis=-1) @ w_out
    return y, yc


def hier_moe(tokens, wg, bg, we, be, w1, w3, w2):
    n, d = tokens.shape
    tf = tokens.astype(F32)
    pg = jax.nn.softmax(tf @ wg.astype(F32) + bg.astype(F32), axis=-1)
    g_sel = jnp.argmax(pg, axis=-1)
    g_w = jnp.max(pg, axis=-1)
    le = (tf @ we.astype(F32) + be.astype(F32)).reshape(n, MOE_GROUPS, MOE_PER_GROUP)
    le_g = jnp.take_along_axis(le, g_sel[:, None, None], axis=1)[:, 0]
    top_v, top_i = lax.top_k(le_g, MOE_TOPK)
    w = jax.nn.softmax(top_v, axis=-1) * g_w[:, None]
    e = g_sel[:, None] * MOE_PER_GROUP + top_i
    n_assign = n * MOE_TOPK
    e_flat, w_flat = e.reshape(-1), w.reshape(-1)
    tok_flat = jnp.repeat(jnp.arange(n), MOE_TOPK)
    order = jnp.argsort(e_flat)
    e_s, tok_s, w_s = e_flat[order], tok_flat[order], w_flat[order]
    counts = jnp.bincount(e_flat, length=N_EXPERTS)
    starts = jnp.cumsum(counts) - counts
    padded = (counts + MOE_BLOCK - 1) // MOE_BLOCK * MOE_BLOCK
    pends = jnp.cumsum(padded)
    pstarts = pends - padded
    dest = pstarts[e_s] + jnp.arange(n_assign) - starts[e_s]
    n_blocks = -(-n_assign // MOE_BLOCK) + N_EXPERTS
    n_slots = n_blocks * MOE_BLOCK
    slot_tok = jnp.full((n_slots,), n, jnp.int32).at[dest].set(tok_s)
    slot_w = jnp.zeros((n_slots,), F32).at[dest].set(w_s)
    block_e = jnp.minimum(jnp.searchsorted(pends, jnp.arange(n_blocks) * MOE_BLOCK, side='right'), N_EXPERTS - 1)
    x_pad = jnp.concatenate([tokens, jnp.zeros((1, d), tokens.dtype)], axis=0)
    xb = x_pad[slot_tok].reshape(n_blocks, MOE_BLOCK, d)

    def expert_block(args):
        xblk, eid = args
        hid = jax.nn.silu(xblk @ w1[eid]) * (xblk @ w3[eid])
        return hid @ w2[eid]

    yb = lax.map(expert_block, (xb, block_e)).reshape(n_slots, d)
    out = jnp.zeros((n + 1, d), tokens.dtype).at[slot_tok].add(yb * slot_w[:, None].astype(yb.dtype))
    return out[:n]


def setup_inputs(seed: int = 0) -> dict:
    key = jax.random.key(seed)
    ks = iter(jax.random.split(key, 40))
    nrm = lambda shape, s: jax.random.normal(next(ks), shape, F32) * s
    x = nrm((BATCH, SEQ, D_MODEL), 1.0)
    c = nrm((BATCH, D_MODEL), 1.0)
    ctx = nrm((BATCH, CTX_LEN, D_MODEL), 1.0)
    c_ctx = nrm((D_MODEL,), 1.0)
    w_mod = nrm((DEPTH, D_MODEL, 6 * D_MODEL), 0.5 * D_MODEL ** -0.5)
    b_mod = nrm((DEPTH, 6 * D_MODEL), 0.01)
    norm1_w = 1.0 + nrm((DEPTH, D_MODEL), 0.02)
    norm2_w = 1.0 + nrm((DEPTH, D_MODEL), 0.02)
    w_in = nrm((DEPTH, D_MODEL, IN_COLS), D_MODEL ** -0.5)
    w_out = nrm((DEPTH, MIX_W, D_MODEL), MIX_W ** -0.5)
    na_rpb = nrm((DEPTH, NA_H, 2 * NA_KR - 1, 2 * NA_KC - 1), 0.1)
    i_b = nrm((DEPTH, 2, ML_H), 0.1)
    f_b = 3.0 + 3.0 * jax.random.uniform(next(ks), (DEPTH, 2, ML_H), F32)
    ml_gate_b = jnp.stack([i_b[:, 0], f_b[:, 0], i_b[:, 1], f_b[:, 1]], axis=1)
    ml_norm_w = 1.0 + nrm((DEPTH, ML_W), 0.02)
    ssm_conv_w = nrm((DEPTH, SSM_CONV, CONV_CH), SSM_CONV ** -0.5)
    ssm_conv_b = nrm((DEPTH, CONV_CH), 0.01)
    dt0 = jnp.exp(jax.random.uniform(next(ks), (DEPTH, 2, SSM_H), F32, minval=math.log(1e-3), maxval=math.log(1e-1)))
    ssm_dt_bias = dt0 + jnp.log(-jnp.expm1(-dt0))
    ssm_a_log = jnp.log(jax.random.uniform(next(ks), (DEPTH, 2, SSM_H), F32, minval=1.0, maxval=16.0))
    ssm_d = 1.0 + nrm((DEPTH, SSM_H), 0.1)
    ssm_norm_w = 1.0 + nrm((DEPTH, SSM_W), 0.02)
    router_g_w = nrm((DEPTH, D_MODEL, MOE_GROUPS), D_MODEL ** -0.5)
    router_g_b = nrm((DEPTH, MOE_GROUPS), 0.01)
    router_e_w = nrm((DEPTH, D_MODEL, N_EXPERTS), D_MODEL ** -0.5)
    router_e_b = nrm((DEPTH, N_EXPERTS), 0.01)
    moe_w1 = nrm((DEPTH, N_EXPERTS, D_MODEL, MOE_FF), D_MODEL ** -0.5)
    moe_w3 = nrm((DEPTH, N_EXPERTS, D_MODEL, MOE_FF), D_MODEL ** -0.5)
    moe_w2 = nrm((DEPTH, N_EXPERTS, MOE_FF, D_MODEL), MOE_FF ** -0.5)
    final_norm_w = 1.0 + nrm((D_MODEL,), 0.02)
    return {'x': x, 'c': c, 'ctx': ctx, 'c_ctx': c_ctx, 'w_mod': w_mod, 'b_mod': b_mod,
            'norm1_w': norm1_w, 'norm2_w': norm2_w, 'w_in': w_in, 'w_out': w_out, 'na_rpb': na_rpb,
            'ml_gate_b': ml_gate_b, 'ml_norm_w': ml_norm_w, 'ssm_conv_w': ssm_conv_w, 'ssm_conv_b': ssm_conv_b,
            'ssm_dt_bias': ssm_dt_bias, 'ssm_a_log': ssm_a_log, 'ssm_d': ssm_d, 'ssm_norm_w': ssm_norm_w,
            'router_g_w': router_g_w, 'router_g_b': router_g_b, 'router_e_w': router_e_w,
            'router_e_b': router_e_b, 'moe_w1': moe_w1, 'moe_w3': moe_w3, 'moe_w2': moe_w2,
            'final_norm_w': final_norm_w}


def reference(x, c, ctx, c_ctx, w_mod, b_mod, norm1_w, norm2_w, w_in, w_out, na_rpb, ml_gate_b, ml_norm_w,
              ssm_conv_w, ssm_conv_b, ssm_dt_bias, ssm_a_log, ssm_d, ssm_norm_w, router_g_w, router_g_b,
              router_e_w, router_e_b, moe_w1, moe_w3, moe_w2, final_norm_w):
    for l in range(DEPTH):
        last = l == DEPTH - 1
        sh1, sc1, g1, sh2, sc2, g2 = adaln_params(c, w_mod[l], b_mod[l])
        csh1, csc1, cg1, csh2, csc2, cg2 = adaln_params(c_ctx[None], w_mod[l], b_mod[l])
        h = rms_norm(x, norm1_w[l]) * (1 + sc1) + sh1
        hc = rms_norm(ctx, norm1_w[l]) * (1 + csc1) + csh1
        y, yc = hybrid_mixer(h, hc, w_in[l], w_out[l], na_rpb[l], ml_gate_b[l], ml_norm_w[l], ssm_conv_w[l],
                             ssm_conv_b[l], ssm_dt_bias[l], ssm_a_log[l], ssm_d[l], ssm_norm_w[l], not last)
        x = x + g1 * y
        h2 = rms_norm(x, norm2_w[l]) * (1 + sc2) + sh2
        moe_args = (router_g_w[l], router_g_b[l], router_e_w[l], router_e_b[l], moe_w1[l], moe_w3[l], moe_w2[l])
        if last:
            x = x + g2 * hier_moe(h2.reshape(-1, D_MODEL), *moe_args).reshape(x.shape)
        else:
            ctx = ctx + cg1 * yc
            h2c = rms_norm(ctx, norm2_w[l]) * (1 + csc2) + csh2
            n_lat = x.shape[0] * x.shape[1]
            out = hier_moe(jnp.concatenate([h2.reshape(-1, D_MODEL), h2c.reshape(-1, D_MODEL)], axis=0), *moe_args)
            x = x + g2 * out[:n_lat].reshape(x.shape)
            ctx = ctx + cg2 * out[n_lat:].reshape(ctx.shape)
    return rms_norm(x, final_norm_w)
```

```python
import functools

import jax
import jax.numpy as jnp
import numpy as np
from jax import lax
from jax.experimental import pallas as pl
from jax.experimental.pallas import tpu as pltpu

F32 = jnp.float32
BF16 = jnp.bfloat16
HIGHEST = lax.Precision.HIGHEST

D_MODEL = 2048
GRID_W = 64
CTX = 256
NA_DH = 128
NA_W = 1024
NA_H = 8
NA_KR = 8
NA_KC = 16
ML_DH = 128
ML_W = 512
ML_H = 4
SSM_P = 64
SSM_W = 512
SSM_H = 8
SSM_G = 2
SSM_N = 128
CONV_CH = SSM_W + 2 * SSM_G * SSM_N
MOE_GROUPS = 4
MOE_PER_GROUP = 8
N_EXPERTS = 32
MOE_FF = 1024
ROPE_THETA = 10000.0
EPS = 1e-6

CHUNK = 256
LANES = 128
NEG = -1e30
MOE_BM = 256
VMEM_LIMIT = 56 * 1024 * 1024

R_XBC, R_MQ, R_MK, R_MV, R_MO, R_Z = 0, 1024, 1536, 2048, 2560, 3072
REST_W = 3584
QKV_W = 3 * NA_W
G_I, G_F, G_DT = 0, 4, 8


def _cparams(sem):
    return pltpu.CompilerParams(dimension_semantics=sem, vmem_limit_bytes=VMEM_LIMIT)


def _silu(x):
    return x / (1.0 + jnp.exp(-x))


def _softplus(x):
    return jnp.maximum(x, 0.0) + jnp.log1p(jnp.exp(-jnp.abs(x)))


def _dot(a, b):
    return jnp.dot(a, b, preferred_element_type=F32)


def _dot_nt(a, b):
    return lax.dot_general(a, b, (((1,), (1,)), ((), ())), preferred_element_type=F32)


def _dot_tn(a, b):
    return lax.dot_general(a, b, (((0,), (0,)), ((), ())), preferred_element_type=F32)


def _mod_kernel(c_ref, w_ref, b_ref, o_ref):
    o_ref[0] = jnp.dot(_silu(c_ref[...]), w_ref[0], preferred_element_type=F32, precision=HIGHEST) + b_ref[0]


def adaln_modulation(cond, w_mod, b_mod):
    depth, d, n = w_mod.shape
    tn = 512
    return pl.pallas_call(
        _mod_kernel,
        grid=(depth, n // tn),
        in_specs=[pl.BlockSpec((8, d), lambda l, j: (0, 0)),
                  pl.BlockSpec((1, d, tn), lambda l, j: (l, 0, j)),
                  pl.BlockSpec((1, 1, tn), lambda l, j: (l, 0, j))],
        out_specs=pl.BlockSpec((1, 8, tn), lambda l, j: (l, 0, j)),
        out_shape=jax.ShapeDtypeStruct((depth, 8, n), F32),
        compiler_params=_cparams(("parallel", "parallel")),
        name="adaln_modulation",
    )(cond, w_mod, b_mod.reshape(depth, 1, n))


def _modulated_norm(x, nw, sc2, sh2, row0):
    r = x.shape[0]
    y = x * lax.rsqrt(jnp.mean(x * x, axis=-1, keepdims=True) + EPS) * nw
    is_ctx = (row0 + lax.broadcasted_iota(jnp.int32, (r, 1), 0)) < CTX
    sc = jnp.where(is_ctx, sc2[1:2, :], sc2[0:1, :])
    sh = jnp.where(is_ctx, sh2[1:2, :], sh2[0:1, :])
    return y * (1.0 + sc) + sh


IN_TM = 1280
IN_TN = 512
IN_SUB = 256


def _in_proj_kernel(x_ref, nw_ref, sc_ref, sh_ref, wm_ref, wg_ref, qkv_ref, rest_ref, gate_ref, a_scr):
    i = pl.program_id(0)
    j = pl.program_id(1)
    n_qkv = QKV_W // IN_TN

    @pl.when(j == 0)
    def _():
        def body(r, carry):
            rows = pl.ds(pl.multiple_of(r * IN_SUB, IN_SUB), IN_SUB)
            h = _modulated_norm(x_ref[rows, :], nw_ref[...], sc_ref[...], sh_ref[...], i * IN_TM + r * IN_SUB)
            a_scr[rows, :] = h.astype(BF16)
            gate_ref[rows, :] = jnp.dot(h, wg_ref[...], preferred_element_type=F32, precision=HIGHEST)
            return carry
        lax.fori_loop(0, IN_TM // IN_SUB, body, 0)

    acc = _dot(a_scr[...], wm_ref[...])

    @pl.when(j < n_qkv)
    def _():
        qkv_ref[...] = acc.astype(BF16)

    @pl.when(j >= n_qkv)
    def _():
        rest_ref[...] = acc


def in_projection(xa, nw, sc2, sh2, w_main, w_gate):
    t_all, d = xa.shape
    n_main = w_main.shape[1]
    n_qkv = QKV_W // IN_TN
    return pl.pallas_call(
        _in_proj_kernel,
        grid=(t_all // IN_TM, n_main // IN_TN),
        in_specs=[pl.BlockSpec((IN_TM, d), lambda i, j: (i, 0)),
                  pl.BlockSpec((1, d), lambda i, j: (0, 0)),
                  pl.BlockSpec((2, d), lambda i, j: (0, 0)),
                  pl.BlockSpec((2, d), lambda i, j: (0, 0)),
                  pl.BlockSpec((d, IN_TN), lambda i, j: (0, j)),
                  pl.BlockSpec((d, LANES), lambda i, j: (0, 0))],
        out_specs=[pl.BlockSpec((IN_TM, IN_TN), lambda i, j: (i, jnp.minimum(j, n_qkv - 1))),
                   pl.BlockSpec((IN_TM, IN_TN), lambda i, j: (i, jnp.maximum(j - n_qkv, 0))),
                   pl.BlockSpec((IN_TM, LANES), lambda i, j: (i, 0))],
        out_shape=[jax.ShapeDtypeStruct((t_all, QKV_W), BF16),
                   jax.ShapeDtypeStruct((t_all, REST_W), F32),
                   jax.ShapeDtypeStruct((t_all, LANES), F32)],
        scratch_shapes=[pltpu.VMEM((IN_TM, d), BF16)],
        compiler_params=_cparams(("arbitrary", "arbitrary")),
        name="in_projection",
    )(xa, nw, sc2, sh2, w_main, w_gate)


NA_RB = CHUNK // GRID_W


def _na_kernel(q_ref, k_ref, v_ref, bias_ref, o_ref, *, n_rows):
    rb = pl.program_id(1)
    scale = NA_DH ** -0.5
    kc = k_ref[0:CTX, :]
    vc = v_ref[0:CTX, :]

    @pl.when(rb == 0)
    def _():
        s = _dot_nt(q_ref[...], kc) * scale
        m = jnp.max(s, axis=-1, keepdims=True)
        p = jnp.exp(s - m)
        l = jnp.sum(p, axis=-1, keepdims=True)
        o_ref[...] = (_dot(p.astype(BF16), vc) / l).astype(o_ref.dtype)

    @pl.when(rb > 0)
    def _():
        for rr in range(NA_RB):
            r = (rb - 1) * NA_RB + rr
            rs = jnp.clip(r - NA_KR // 2, 0, n_rows - NA_KR)
            d0 = rs - r + NA_KR - 1
            start = pl.multiple_of(CTX + rs * GRID_W, GRID_W)
            q = q_ref[rr * GRID_W:(rr + 1) * GRID_W, :]
            kw = k_ref[pl.ds(start, NA_KR * GRID_W), :]
            vw = v_ref[pl.ds(start, NA_KR * GRID_W), :]
            s = _dot_nt(q, kw) * scale + bias_ref[0, d0]
            sc = _dot_nt(q, kc) * scale
            m = jnp.maximum(jnp.max(s, axis=-1, keepdims=True), jnp.max(sc, axis=-1, keepdims=True))
            p = jnp.exp(s - m)
            pc = jnp.exp(sc - m)
            l = jnp.sum(p, axis=-1, keepdims=True) + jnp.sum(pc, axis=-1, keepdims=True)
            o = _dot(p.astype(BF16), vw) + _dot(pc.astype(BF16), vc)
            o_ref[rr * GRID_W:(rr + 1) * GRID_W, :] = (o / l).astype(o_ref.dtype)


def na_bias_table(rpb):
    c = np.arange(GRID_W)
    cs = np.clip(c - NA_KC // 2, 0, GRID_W - NA_KC)
    kcol = np.arange(GRID_W)
    inside = (kcol[None, :] >= cs[:, None]) & (kcol[None, :] < cs[:, None] + NA_KC)
    col_off = np.clip(kcol[None, :] - c[:, None] + NA_KC - 1, 0, 2 * NA_KC - 2)
    d0 = np.arange(NA_KR)
    j = np.arange(NA_KR)
    row_off = d0[:, None] + j[None, :]
    t = rpb.astype(F32)[:, row_off[:, None, :, None], col_off[None, :, None, :]]
    t = jnp.where(jnp.asarray(inside)[None, None, :, None, :], t, NEG)
    return t.reshape(rpb.shape[0], NA_KR, GRID_W, NA_KR * GRID_W)


def neighbourhood_attention(qkv, bias_tab):
    t_all = qkv.shape[0]
    n_rows = (t_all - CTX) // GRID_W
    n_rb = t_all // CHUNK
    return pl.pallas_call(
        functools.partial(_na_kernel, n_rows=n_rows),
        grid=(NA_H, n_rb),
        in_specs=[pl.BlockSpec((CHUNK, NA_DH), lambda h, rb: (rb, h)),
                  pl.BlockSpec((t_all, NA_DH), lambda h, rb: (0, NA_H + h)),
                  pl.BlockSpec((t_all, NA_DH), lambda h, rb: (0, 2 * NA_H + h)),
                  pl.BlockSpec((1, NA_KR, GRID_W, NA_KR * GRID_W), lambda h, rb: (h, 0, 0, 0))],
        out_specs=pl.BlockSpec((CHUNK, NA_DH), lambda h, rb: (rb, h)),
        out_shape=jax.ShapeDtypeStruct((t_all, NA_W), BF16),
        compiler_params=_cparams(("parallel", "parallel")),
        name="neighbourhood_attention",
    )(qkv, qkv, qkv, bias_tab)


def _scan_chunk(d, s, n_chunks):
    return jnp.where(d == 0, s, jnp.where(s == 0, 0, n_chunks - s))


def _scan_masks(d):
    row = lax.broadcasted_iota(jnp.int32, (CHUNK, CHUNK), 0)
    col = lax.broadcasted_iota(jnp.int32, (CHUNK, CHUNK), 1)
    mask = jnp.where(d == 0, row - col, col - row) >= 0
    return mask, mask.astype(F32)


def _rope(x, cos, sin_signed):
    lane = lax.broadcasted_iota(jnp.int32, x.shape, 1)
    nf = ML_DH // 4
    partner = jnp.where((lane % (2 * nf)) < nf, pltpu.roll(x, ML_DH - nf, 1), pltpu.roll(x, nf, 1))
    return x * cos + partner * sin_signed


def _mlstm_kernel(q_ref, k_ref, v_ref, g_ref, gt_ref, gb_ref, gbt_ref, cos_ref, sin_ref, o_ref,
                  c_scr, n_scr, m_scr):
    d = pl.program_id(0)
    s = pl.program_id(1)

    @pl.when(s == 0)
    def _():
        c_scr[...] = jnp.zeros_like(c_scr)
        n_scr[...] = jnp.zeros_like(n_scr)
        m_scr[...] = jnp.zeros_like(m_scr)

    mask, mf = _scan_masks(d)
    g = g_ref[...] + gb_ref[0]
    gt = gt_ref[...] + gbt_ref[0]
    lf = -_softplus(-g)
    lft = -_softplus(-gt)
    cum = jnp.dot(mf, lf, preferred_element_type=F32, precision=HIGHEST)
    cumt = lax.dot_general(lft, mf, (((1,), (1,)), ((), ())), preferred_element_type=F32,
                           precision=HIGHEST)
    tot = jnp.sum(lf, axis=0, keepdims=True)
    cos = cos_ref[...]
    sin = sin_ref[...]

    for h in range(ML_H):
        hs = slice(h * ML_DH, (h + 1) * ML_DH)
        q = _rope(q_ref[:, hs], cos, sin)
        k = _rope(k_ref[:, hs], cos, sin) * (ML_DH ** -0.5)
        v = v_ref[:, hs]
        qb, kb, vb = q.astype(BF16), k.astype(BF16), v.astype(BF16)
        bt_col = cum[:, G_F + h:G_F + h + 1]
        bt_row = cumt[G_F + h:G_F + h + 1, :]
        ig_col = g[:, G_I + h:G_I + h + 1]
        ig_row = gt[G_I + h:G_I + h + 1, :]
        b_last = tot[:, G_F + h:G_F + h + 1]
        m_prev = m_scr[h]
        cmat = c_scr[h]
        nvec = n_scr[h]

        log_d = jnp.where(mask, bt_col - bt_row + ig_row, -jnp.inf)
        inter = bt_col + m_prev
        m_t = jnp.maximum(jnp.max(log_d, axis=-1, keepdims=True), inter)
        wts = _dot_nt(qb, kb) * jnp.exp(log_d - m_t)
        sc = jnp.exp(inter - m_t)
        num = _dot(wts.astype(BF16), vb) + _dot_nt(qb, cmat.astype(BF16)) * sc
        den = jnp.sum(wts, axis=-1, keepdims=True) + jnp.sum(q * nvec, axis=-1, keepdims=True) * sc
        den = jnp.maximum(jnp.abs(den), jnp.exp(-m_t))
        o_ref[0, :, hs] = num / den

        tail = b_last - bt_col + ig_col
        m_new = jnp.maximum(b_last + m_prev, jnp.max(tail, axis=0, keepdims=True))
        wgt = jnp.exp(tail - m_new)
        decay = jnp.exp(b_last + m_prev - m_new)
        c_scr[h] = decay * cmat + _dot_tn((v * wgt).astype(BF16), kb)
        n_scr[h] = decay * nvec + jnp.sum(wgt * k, axis=0, keepdims=True)
        m_scr[h] = m_new


def mlstm_scan(rest, gates, gates_t, gbias, gbias_t, cos_tab, sin_tab):
    t_all = rest.shape[0]
    n_chunks = t_all // CHUNK
    cm = lambda d, s: _scan_chunk(d, s, n_chunks)
    col = lambda off: off // ML_W
    return pl.pallas_call(
        _mlstm_kernel,
        grid=(2, n_chunks),
        in_specs=[pl.BlockSpec((CHUNK, ML_W), lambda d, s: (cm(d, s), col(R_MQ))),
                  pl.BlockSpec((CHUNK, ML_W), lambda d, s: (cm(d, s), col(R_MK))),
                  pl.BlockSpec((CHUNK, ML_W), lambda d, s: (cm(d, s), col(R_MV))),
                  pl.BlockSpec((CHUNK, LANES), lambda d, s: (cm(d, s), d)),
                  pl.BlockSpec((LANES, CHUNK), lambda d, s: (d, cm(d, s))),
                  pl.BlockSpec((1, 1, LANES), lambda d, s: (d, 0, 0)),
                  pl.BlockSpec((1, LANES, 1), lambda d, s: (d, 0, 0)),
                  pl.BlockSpec((CHUNK, ML_DH), lambda d, s: (cm(d, s), 0)),
                  pl.BlockSpec((CHUNK, ML_DH), lambda d, s: (cm(d, s), 0))],
        out_specs=pl.BlockSpec((1, CHUNK, ML_W), lambda d, s: (d, cm(d, s), 0)),
        out_shape=jax.ShapeDtypeStruct((2, t_all, ML_W), F32),
        scratch_shapes=[pltpu.VMEM((ML_H, ML_DH, ML_DH), F32),
                        pltpu.VMEM((ML_H, 1, ML_DH), F32),
                        pltpu.VMEM((ML_H, 1, 1), F32)],
        compiler_params=_cparams(("arbitrary", "arbitrary")),
        name="mlstm_scan",
    )(rest, rest, rest, gates, gates_t, gbias, gbias_t, cos_tab, sin_tab)


CONV_HALO = 8


def _conv_kernel(x_ref, p_ref, n_ref, w_ref, b_ref, o_ref, *, n_chunks):
    s = pl.program_id(0)
    x = x_ref[...]
    row = lax.broadcasted_iota(jnp.int32, x.shape, 0)
    prev = jnp.where(s >= 2, p_ref[...], 0.0)
    nxt = jnp.where((s >= 1) & (s <= n_chunks - 2), n_ref[...], 0.0)
    xm1 = jnp.where(row == 0, prev[CONV_HALO - 1:CONV_HALO, :], pltpu.roll(x, 1, 0))
    xm2 = jnp.where(row == 0, prev[CONV_HALO - 2:CONV_HALO - 1, :],
                    jnp.where(row == 1, prev[CONV_HALO - 1:CONV_HALO, :], pltpu.roll(x, 2, 0)))
    xp1 = jnp.where(row == CHUNK - 1, nxt[0:1, :], pltpu.roll(x, CHUNK - 1, 0))
    y = w_ref[0:1, :] * xm2 + w_ref[1:2, :] * xm1 + w_ref[2:3, :] * x + w_ref[3:4, :] * xp1 + b_ref[...]
    o_ref[...] = _silu(y)


def ssd_conv(rest, conv_w, conv_b):
    t_all = rest.shape[0]
    n_chunks = t_all // CHUNK
    hb = CHUNK // CONV_HALO
    return pl.pallas_call(
        functools.partial(_conv_kernel, n_chunks=n_chunks),
        grid=(n_chunks,),
        in_specs=[pl.BlockSpec((CHUNK, CONV_CH), lambda s: (s, 0)),
                  pl.BlockSpec((CONV_HALO, CONV_CH), lambda s: (jnp.maximum(s * hb - 1, 0), 0)),
                  pl.BlockSpec((CONV_HALO, CONV_CH), lambda s: (jnp.minimum((s + 1) * hb, n_chunks * hb - 1), 0)),
                  pl.BlockSpec((4, CONV_CH), lambda s: (0, 0)),
                  pl.BlockSpec((1, CONV_CH), lambda s: (0, 0))],
        out_specs=pl.BlockSpec((CHUNK, CONV_CH), lambda s: (s, 0)),
        out_shape=jax.ShapeDtypeStruct((t_all, CONV_CH), F32),
        compiler_params=_cparams(("parallel",)),
        name="ssd_conv",
    )(rest, rest, rest, conv_w, conv_b.reshape(1, CONV_CH))


def _ssd_kernel(x_ref, g_ref, gt_ref, gb_ref, gbt_ref, al_ref, alt_ref, o_ref, s_scr):
    d = pl.program_id(0)
    s = pl.program_id(1)

    @pl.when(s == 0)
    def _():
        s_scr[...] = jnp.zeros_like(s_scr)

    mask, mf = _scan_masks(d)
    dt = _softplus(g_ref[...] + gb_ref[0])
    dtt = _softplus(gt_ref[...] + gbt_ref[0])
    inc = dt * (-jnp.exp(al_ref[0]))
    inct = dtt * (-jnp.exp(alt_ref[0]))
    cum = jnp.dot(mf, inc, preferred_element_type=F32, precision=HIGHEST)
    cumt = lax.dot_general(inct, mf, (((1,), (1,)), ((), ())), preferred_element_type=F32, precision=HIGHEST)
    tot = jnp.sum(inc, axis=0, keepdims=True)

    hpg = SSM_H // SSM_G
    for gi in range(SSM_G):
        bm = x_ref[:, SSM_W + gi * SSM_N:SSM_W + (gi + 1) * SSM_N].astype(BF16)
        cm = x_ref[:, SSM_W + (SSM_G + gi) * SSM_N:SSM_W + (SSM_G + gi + 1) * SSM_N].astype(BF16)
        gmat = _dot_nt(cm, bm)
        for hh in range(hpg):
            h = gi * hpg + hh
            la_col = cum[:, G_DT + h:G_DT + h + 1]
            la_row = cumt[G_DT + h:G_DT + h + 1, :]
            la_last = tot[:, G_DT + h:G_DT + h + 1]
            dt_col = dt[:, G_DT + h:G_DT + h + 1]
            state = s_scr[h]
            decay = jnp.exp(jnp.where(mask, la_col - la_row, -jnp.inf))
            xdt = x_ref[:, h * SSM_P:(h + 1) * SSM_P] * dt_col
            y = (_dot((gmat * decay).astype(BF16), xdt.astype(BF16))
                 + _dot_nt(cm, state.astype(BF16)) * jnp.exp(la_col))
            o_ref[0, :, h * SSM_P:(h + 1) * SSM_P] = y
            tail = jnp.exp(la_last - la_col)
            s_scr[h] = state * jnp.exp(la_last) + _dot_tn((xdt * tail).astype(BF16), bm)


def ssd_scan(xbc_act, gates, gates_t, gbias, gbias_t, alog, alog_t):
    t_all = xbc_act.shape[0]
    n_chunks = t_all // CHUNK
    cm = lambda d, s: _scan_chunk(d, s, n_chunks)
    return pl.pallas_call(
        _ssd_kernel,
        grid=(2, n_chunks),
        in_specs=[pl.BlockSpec((CHUNK, CONV_CH), lambda d, s: (cm(d, s), 0)),
                  pl.BlockSpec((CHUNK, LANES), lambda d, s: (cm(d, s), d)),
                  pl.BlockSpec((LANES, CHUNK), lambda d, s: (d, cm(d, s))),
                  pl.BlockSpec((1, 1, LANES), lambda d, s: (d, 0, 0)),
                  pl.BlockSpec((1, LANES, 1), lambda d, s: (d, 0, 0)),
                  pl.BlockSpec((1, 1, LANES), lambda d, s: (d, 0, 0)),
                  pl.BlockSpec((1, LANES, 1), lambda d, s: (d, 0, 0))],
        out_specs=pl.BlockSpec((1, CHUNK, SSM_W), lambda d, s: (d, cm(d, s), 0)),
        out_shape=jax.ShapeDtypeStruct((2, t_all, SSM_W), F32),
        scratch_shapes=[pltpu.VMEM((SSM_H, SSM_P, SSM_N), F32)],
        compiler_params=_cparams(("arbitrary", "arbitrary")),
        name="ssd_scan",
    )(xbc_act, gates, gates_t, gbias, gbias_t, alog, alog_t)


OUT_TM = 640
OUT_TN = 512
OUT_SUB = 128


def _out_proj_kernel(ona_ref, hs_ref, mo_ref, ys_ref, xs_ref, z_ref, mlw_ref, dsk_ref, ssw_ref,
                     w_ref, x_ref, g_ref, o_ref, a_scr):
    i = pl.program_id(0)
    j = pl.program_id(1)

    @pl.when(j == 0)
    def _():
        def body(r, carry):
            rows = pl.ds(pl.multiple_of(r * OUT_SUB, OUT_SUB), OUT_SUB)
            a_scr[rows, 0:NA_W] = ona_ref[rows, :]
            hsum = hs_ref[0, rows, :] + hs_ref[1, rows, :]
            gate = 1.0 / (1.0 + jnp.exp(-mo_ref[rows, :]))
            for h in range(ML_H):
                cs = slice(h * ML_DH, (h + 1) * ML_DH)
                hh = hsum[:, cs]
                mu = jnp.mean(hh, axis=-1, keepdims=True)
                var = jnp.mean(jnp.square(hh - mu), axis=-1, keepdims=True)
                hn = (hh - mu) * lax.rsqrt(var + EPS) * mlw_ref[:, cs]
                a_scr[rows, NA_W + h * ML_DH:NA_W + (h + 1) * ML_DH] = (gate[:, cs] * hn).astype(BF16)
            y = ys_ref[0, rows, :] + ys_ref[1, rows, :] + dsk_ref[...] * xs_ref[rows, :]
            y = y * _silu(z_ref[rows, :])
            gw = SSM_W // SSM_G
            for gi in range(SSM_G):
                cs = slice(gi * gw, (gi + 1) * gw)
                yg = y[:, cs]
                yn = yg * lax.rsqrt(jnp.mean(yg * yg, axis=-1, keepdims=True) + EPS) * ssw_ref[:, cs]
                a_scr[rows, NA_W + ML_W + gi * gw:NA_W + ML_W + (gi + 1) * gw] = yn.astype(BF16)
            return carry
        lax.fori_loop(0, OUT_TM // OUT_SUB, body, 0)

    acc = _dot(a_scr[...], w_ref[...])
    is_ctx = (i * OUT_TM + lax.broadcasted_iota(jnp.int32, (OUT_TM, 1), 0)) < CTX
    gate1 = jnp.where(is_ctx, g_ref[1:2, :], g_ref[0:1, :])
    o_ref[...] = x_ref[...] + gate1 * acc


def out_projection(o_na, hs, rest, ys, xbc_act, ml_norm_w, d_skip_vec, ssm_norm_w, w_out, xa, g1):
    t_all, d = xa.shape
    cw = lambda off: off // ML_W
    return pl.pallas_call(
        _out_proj_kernel,
        grid=(t_all // OUT_TM, d // OUT_TN),
        in_specs=[pl.BlockSpec((OUT_TM, NA_W), lambda i, j: (i, 0)),
                  pl.BlockSpec((2, OUT_TM, ML_W), lambda i, j: (0, i, 0)),
                  pl.BlockSpec((OUT_TM, ML_W), lambda i, j: (i, cw(R_MO))),
                  pl.BlockSpec((2, OUT_TM, SSM_W), lambda i, j: (0, i, 0)),
                  pl.BlockSpec((OUT_TM, SSM_W), lambda i, j: (i, 0)),
                  pl.BlockSpec((OUT_TM, SSM_W), lambda i, j: (i, cw(R_Z))),
                  pl.BlockSpec((1, ML_W), lambda i, j: (0, 0)),
                  pl.BlockSpec((1, SSM_W), lambda i, j: (0, 0)),
                  pl.BlockSpec((1, SSM_W), lambda i, j: (0, 0)),
                  pl.BlockSpec((d, OUT_TN), lambda i, j: (0, j)),
                  pl.BlockSpec((OUT_TM, OUT_TN), lambda i, j: (i, j)),
                  pl.BlockSpec((2, OUT_TN), lambda i, j: (0, j))],
        out_specs=pl.BlockSpec((OUT_TM, OUT_TN), lambda i, j: (i, j)),
        out_shape=jax.ShapeDtypeStruct((t_all, d), F32),
        scratch_shapes=[pltpu.VMEM((OUT_TM, d), BF16)],
        compiler_params=_cparams(("arbitrary", "arbitrary")),
        name="out_projection",
    )(o_na, hs, rest, ys, xbc_act, rest, ml_norm_w, d_skip_vec, ssm_norm_w, w_out, xa, g1)


RT_TM = 640
RT_SUB = 128


def _router_kernel(x_ref, nw_ref, sc_ref, sh_ref, wr_ref, br_ref, h_ref, r_ref):
    i = pl.program_id(0)

    def body(r, carry):
        rows = pl.ds(pl.multiple_of(r * RT_SUB, RT_SUB), RT_SUB)
        h = _modulated_norm(x_ref[rows, :], nw_ref[...], sc_ref[...], sh_ref[...], i * RT_TM + r * RT_SUB)
        h_ref[rows, :] = h.astype(BF16)
        logit = jnp.dot(h, wr_ref[...], preferred_element_type=F32, precision=HIGHEST) + br_ref[...]
        lane = lax.broadcasted_iota(jnp.int32, logit.shape, 1)
        big = jnp.int32(LANES)
        is_g = lane < MOE_GROUPS
        lg = jnp.where(is_g, logit, -jnp.inf)
        gmax = jnp.max(lg, axis=-1, keepdims=True)
        g_sel = jnp.min(jnp.where(is_g & (lg == gmax), lane, big), axis=-1, keepdims=True)
        g_w = 1.0 / jnp.sum(jnp.exp(lg - gmax), axis=-1, keepdims=True)
        lo = MOE_GROUPS + g_sel * MOE_PER_GROUP
        in_g = (lane >= lo) & (lane < lo + MOE_PER_GROUP)
        le = jnp.where(in_g, logit, -jnp.inf)
        v1 = jnp.max(le, axis=-1, keepdims=True)
        i1 = jnp.min(jnp.where(in_g & (le == v1), lane, big), axis=-1, keepdims=True)
        le2 = jnp.where(lane == i1, -jnp.inf, le)
        v2 = jnp.max(le2, axis=-1, keepdims=True)
        i2 = jnp.min(jnp.where(in_g & (lane != i1) & (le2 == v2), lane, big), axis=-1, keepdims=True)
        e2 = jnp.exp(v2 - v1)
        w1 = g_w / (1.0 + e2)
        w2 = g_w * e2 / (1.0 + e2)
        out = jnp.where(lane == 0, (i1 - MOE_GROUPS).astype(F32),
                        jnp.where(lane == 1, (i2 - MOE_GROUPS).astype(F32),
                                  jnp.where(lane == 2, w1, jnp.where(lane == 3, w2, 0.0))))
        r_ref[rows, :] = out
        return carry
    lax.fori_loop(0, RT_TM // RT_SUB, body, 0)


def moe_router(xa, nw, sc2, sh2, w_route, b_route):
    t_all, d = xa.shape
    return pl.pallas_call(
        _router_kernel,
        grid=(t_all // RT_TM,),
        in_specs=[pl.BlockSpec((RT_TM, d), lambda i: (i, 0)),
                  pl.BlockSpec((1, d), lambda i: (0, 0)),
                  pl.BlockSpec((2, d), lambda i: (0, 0)),
                  pl.BlockSpec((2, d), lambda i: (0, 0)),
                  pl.BlockSpec((d, LANES), lambda i: (0, 0)),
                  pl.BlockSpec((1, LANES), lambda i: (0, 0))],
        out_specs=[pl.BlockSpec((RT_TM, d), lambda i: (i, 0)),
                   pl.BlockSpec((RT_TM, LANES), lambda i: (i, 0))],
        out_shape=[jax.ShapeDtypeStruct((t_all, d), BF16),
                   jax.ShapeDtypeStruct((t_all, LANES), F32)],
        compiler_params=_cparams(("parallel",)),
        name="moe_router",
    )(xa, nw, sc2, sh2, w_route, b_route)


def _expert_kernel(be_ref, nu_ref, x_ref, sw_ref, w1_ref, w3_ref, w2_ref, o_ref):
    b = pl.program_id(0)

    @pl.when(b < nu_ref[0])
    def _():
        x = x_ref[...]
        hid = _silu(_dot(x, w1_ref[0])) * _dot(x, w3_ref[0])
        o_ref[...] = _dot(hid.astype(BF16), w2_ref[0]) * sw_ref[...]

    @pl.when(b >= nu_ref[0])
    def _():
        o_ref[...] = jnp.zeros_like(o_ref)


def expert_blocks(block_e, n_used, xs, slot_w, w1, w3, w2):
    n_slots, d = xs.shape
    ff = w1.shape[-1]
    n_blocks = n_slots // MOE_BM
    return pl.pallas_call(
        _expert_kernel,
        grid_spec=pltpu.PrefetchScalarGridSpec(
            num_scalar_prefetch=2,
            grid=(n_blocks,),
            in_specs=[pl.BlockSpec((MOE_BM, d), lambda b, be, nu: (b, 0)),
                      pl.BlockSpec((MOE_BM, 1), lambda b, be, nu: (b, 0)),
                      pl.BlockSpec((1, d, ff), lambda b, be, nu: (be[b], 0, 0)),
                      pl.BlockSpec((1, d, ff), lambda b, be, nu: (be[b], 0, 0)),
                      pl.BlockSpec((1, ff, d), lambda b, be, nu: (be[b], 0, 0))],
            out_specs=pl.BlockSpec((MOE_BM, d), lambda b, be, nu: (b, 0))),
        out_shape=jax.ShapeDtypeStruct((n_slots, d), F32),
        compiler_params=_cparams(("arbitrary",)),
        name="moe_experts",
    )(block_e, n_used, xs, slot_w, w1, w3, w2)


def _combine_kernel(x_ref, y_ref, g_ref, fw_ref, o_ref, *, row_block0, final_norm):
    i = pl.program_id(0) + row_block0
    is_ctx = (i * CHUNK + lax.broadcasted_iota(jnp.int32, (CHUNK, 1), 0)) < CTX
    gate2 = jnp.where(is_ctx, g_ref[1:2, :], g_ref[0:1, :])
    x = x_ref[...] + gate2 * (y_ref[0] + y_ref[1])
    if final_norm:
        x = x * lax.rsqrt(jnp.mean(x * x, axis=-1, keepdims=True) + EPS) * fw_ref[...]
    o_ref[...] = x


def moe_combine(xa, yg, g2, final_w, final_norm):
    t_all, d = xa.shape
    rb0 = CTX // CHUNK if final_norm else 0
    n_out = t_all // CHUNK - rb0
    return pl.pallas_call(
        functools.partial(_combine_kernel, row_block0=rb0, final_norm=final_norm),
        grid=(n_out,),
        in_specs=[pl.BlockSpec((CHUNK, d), lambda i: (i + rb0, 0)),
                  pl.BlockSpec((2, CHUNK, d), lambda i: (0, i + rb0, 0)),
                  pl.BlockSpec((2, d), lambda i: (0, 0)),
                  pl.BlockSpec((1, d), lambda i: (0, 0))],
        out_specs=pl.BlockSpec((CHUNK, d), lambda i: (i, 0)),
        out_shape=jax.ShapeDtypeStruct((n_out * CHUNK, d), F32),
        compiler_params=_cparams(("parallel",)),
        name="moe_combine",
    )(xa, yg, g2, final_w)


def moe_dispatch(route):
    n = route.shape[0]
    e_flat = route[:, 0:2].astype(jnp.int32).reshape(-1)
    w_flat = route[:, 2:4].reshape(-1)
    n_assign = 2 * n
    tok_flat = jnp.arange(n_assign, dtype=jnp.int32) // 2
    order = jnp.argsort(e_flat, stable=True)
    e_s, tok_s, w_s = e_flat[order], tok_flat[order], w_flat[order]
    counts = jnp.bincount(e_flat, length=N_EXPERTS)
    starts = jnp.cumsum(counts) - counts
    padded = (counts + MOE_BM - 1) // MOE_BM * MOE_BM
    pends = jnp.cumsum(padded)
    pstarts = pends - padded
    dest = (pstarts[e_s] + jnp.arange(n_assign) - starts[e_s]).astype(jnp.int32)
    n_blocks = -(-n_assign // MOE_BM) + N_EXPERTS
    n_slots = n_blocks * MOE_BM
    slot_tok = jnp.full((n_slots,), n, jnp.int32).at[dest].set(tok_s)
    slot_w = jnp.zeros((n_slots,), F32).at[dest].set(w_s)
    pos = jnp.zeros((n_assign,), jnp.int32).at[order].set(dest)
    block_e = jnp.minimum(jnp.searchsorted(pends, jnp.arange(n_blocks) * MOE_BM, side='right'),
                          N_EXPERTS - 1).astype(jnp.int32)
    n_used = (pends[-1] // MOE_BM).astype(jnp.int32).reshape(1)
    return slot_tok, slot_w, pos, block_e, n_used


def hier_moe(xa, nw, sc2, sh2, g2, w_route, b_route, w1, w3, w2, final_w, final_norm):
    t_all, d = xa.shape
    h2, route = moe_router(xa, nw, sc2, sh2, w_route, b_route)
    slot_tok, slot_w, pos, block_e, n_used = moe_dispatch(route)
    xs = jnp.take(h2, slot_tok, axis=0, mode='fill', fill_value=0)
    y = expert_blocks(block_e, n_used, xs, slot_w[:, None], w1, w3, w2)
    yg = jnp.take(y, pos.reshape(t_all, 2).T, axis=0)
    return moe_combine(xa, yg, g2, final_w, final_norm)


def rope_tables(t_lat):
    nf = ML_DH // 4
    inv = ROPE_THETA ** (-jnp.arange(nf, dtype=F32) / nf)
    t_idx = jnp.arange(t_lat)
    ang_r = (t_idx // GRID_W).astype(F32)[:, None] * inv
    ang_c = (t_idx % GRID_W).astype(F32)[:, None] * inv
    cos = jnp.concatenate([jnp.cos(ang_r)] * 2 + [jnp.cos(ang_c)] * 2, axis=-1)
    sin = jnp.concatenate([-jnp.sin(ang_r), jnp.sin(ang_r), -jnp.sin(ang_c), jnp.sin(ang_c)], axis=-1)
    cos = jnp.concatenate([jnp.ones((CTX, ML_DH), F32), cos], axis=0)
    sin = jnp.concatenate([jnp.zeros((CTX, ML_DH), F32), sin], axis=0)
    return cos, sin


def _pad_lanes(v):
    return jnp.pad(v, [(0, 0)] * (v.ndim - 1) + [(0, LANES - v.shape[-1])])


def kernel(x, c, ctx, c_ctx, w_mod, b_mod, norm1_w, norm2_w, w_in, w_out, na_rpb, ml_gate_b, ml_norm_w, ssm_conv_w, ssm_conv_b, ssm_dt_bias, ssm_a_log, ssm_d, ssm_norm_w, router_g_w, router_g_b, router_e_w, router_e_b, moe_w1, moe_w3, moe_w2, final_norm_w):
    depth = w_mod.shape[0]
    t_lat = x.shape[1]
    d = D_MODEL
    xa = jnp.concatenate([ctx[0], x[0]], axis=0)

    cond = jnp.zeros((8, d), F32).at[0].set(c[0]).at[1].set(c_ctx)
    mod = adaln_modulation(cond, w_mod, b_mod)[:, 0:2, :].reshape(depth, 2, 6, d)
    cos_tab, sin_tab = rope_tables(t_lat)

    o_q, o_mq, o_mg, o_z, o_xbc, o_dt = 0, 3 * NA_W, 3 * NA_W + 4 * ML_W, 3 * NA_W + 4 * ML_W + 4 * ML_H, \
        3 * NA_W + 4 * ML_W + 4 * ML_H + SSM_W, 3 * NA_W + 4 * ML_W + 4 * ML_H + SSM_W + CONV_CH

    out = None
    for l in range(depth):
        last = l == depth - 1
        sh1, sc1, g1, sh2, sc2, g2 = (mod[l, :, p, :] for p in range(6))
        wl = w_in[l]
        w_main = jnp.concatenate([wl[:, o_q:o_mq], wl[:, o_xbc:o_dt], wl[:, o_mq:o_mg], wl[:, o_z:o_xbc]],
                                 axis=1).astype(BF16)
        w_gate = _pad_lanes(jnp.concatenate([wl[:, o_mg:o_z], wl[:, o_dt:]], axis=1))

        qkv, rest, gate = in_projection(xa, norm1_w[l][None], sc1, sh1, w_main, w_gate)

        gdir = [jnp.concatenate([gate[:, 8 * dd:8 * dd + 8], gate[:, 16 + 8 * dd:24 + 8 * dd]], axis=1)
                for dd in range(2)]
        gates = jnp.concatenate([_pad_lanes(gd) for gd in gdir], axis=1)
        gates_t = gates.T
        gb = ml_gate_b[l]
        gbias = _pad_lanes(jnp.stack([jnp.concatenate([gb[2 * dd], gb[2 * dd + 1], ssm_dt_bias[l, dd]])
                                      for dd in range(2)]))
        alog = _pad_lanes(jnp.stack([jnp.concatenate([jnp.zeros((8,), F32), ssm_a_log[l, dd]])
                                     for dd in range(2)]))

        o_na = neighbourhood_attention(qkv, na_bias_table(na_rpb[l]))
        hs = mlstm_scan(rest, gates, gates_t, gbias[:, None, :], gbias[:, :, None], cos_tab, sin_tab)
        xbc_act = ssd_conv(rest, ssm_conv_w[l], ssm_conv_b[l])
        ys = ssd_scan(xbc_act, gates, gates_t, gbias[:, None, :], gbias[:, :, None],
                      alog[:, None, :], alog[:, :, None])

        xa = out_projection(o_na, hs, rest, ys, xbc_act, ml_norm_w[l][None],
                            jnp.repeat(ssm_d[l], SSM_P)[None], ssm_norm_w[l][None],
                            w_out[l].astype(BF16), xa, g1)

        w_route = _pad_lanes(jnp.concatenate([router_g_w[l], router_e_w[l]], axis=1))
        b_route = _pad_lanes(jnp.concatenate([router_g_b[l], router_e_b[l]])[None])
        res = hier_moe(xa, norm2_w[l][None], sc2, sh2, g2, w_route, b_route,
                       moe_w1[l].astype(BF16), moe_w3[l].astype(BF16), moe_w2[l].astype(BF16),
                       final_norm_w[None], last)
        if last:
            out = res
        else:
            xa = res
    return out[None]
```

```python
import functools

import jax
import jax.numpy as jnp
import numpy as np
from jax import lax
from jax.experimental import pallas as pl
from jax.experimental.pallas import tpu as pltpu

F32 = jnp.float32
BF16 = jnp.bfloat16
HIGHEST = lax.Precision.HIGHEST

D_MODEL = 2048
GRID_W = 64
CTX = 256
NA_DH = 128
NA_W = 1024
NA_H = 8
NA_KR = 8
NA_KC = 16
ML_DH = 128
ML_W = 512
ML_H = 4
SSM_P = 64
SSM_W = 512
SSM_H = 8
SSM_G = 2
SSM_N = 128
CONV_CH = SSM_W + 2 * SSM_G * SSM_N
MOE_GROUPS = 4
MOE_PER_GROUP = 8
N_EXPERTS = 32
MOE_FF = 1024
ROPE_THETA = 10000.0
EPS = 1e-6

CHUNK = 256
LANES = 128
NEG = -1e30
MOE_BM = 256
VMEM_LIMIT = 56 * 1024 * 1024

R_XBC, R_MQ, R_MK, R_MV, R_MO, R_Z = 0, 1024, 1536, 2048, 2560, 3072
REST_W = 3584
QKV_W = 3 * NA_W
G_I, G_F, G_DT = 0, 4, 8


def _cparams(sem):
    return pltpu.CompilerParams(dimension_semantics=sem, vmem_limit_bytes=VMEM_LIMIT)


def _silu(x):
    return x / (1.0 + jnp.exp(-x))


def _softplus(x):
    return jnp.maximum(x, 0.0) + jnp.log1p(jnp.exp(-jnp.abs(x)))


def _dot(a, b):
    return jnp.dot(a, b, preferred_element_type=F32)


def _dot_nt(a, b):
    return lax.dot_general(a, b, (((1,), (1,)), ((), ())), preferred_element_type=F32)


def _dot_tn(a, b):
    return lax.dot_general(a, b, (((0,), (0,)), ((), ())), preferred_element_type=F32)


def _mod_kernel(c_ref, w_ref, b_ref, o_ref):
    o_ref[0] = jnp.dot(_silu(c_ref[...]), w_ref[0], preferred_element_type=F32, precision=HIGHEST) + b_ref[0]


def adaln_modulation(cond, w_mod, b_mod):
    depth, d, n = w_mod.shape
    tn = 512
    return pl.pallas_call(
        _mod_kernel,
        grid=(depth, n // tn),
        in_specs=[pl.BlockSpec((8, d), lambda l, j: (0, 0)),
                  pl.BlockSpec((1, d, tn), lambda l, j: (l, 0, j)),
                  pl.BlockSpec((1, 1, tn), lambda l, j: (l, 0, j))],
        out_specs=pl.BlockSpec((1, 8, tn), lambda l, j: (l, 0, j)),
        out_shape=jax.ShapeDtypeStruct((depth, 8, n), F32),
        compiler_params=_cparams(("parallel", "parallel")),
        name="adaln_modulation",
    )(cond, w_mod, b_mod.reshape(depth, 1, n))


def _modulated_norm(x, nw, sc2, sh2, row0):
    r = x.shape[0]
    y = x * lax.rsqrt(jnp.mean(x * x, axis=-1, keepdims=True) + EPS) * nw
    is_ctx = (row0 + lax.broadcasted_iota(jnp.int32, (r, 1), 0)) < CTX
    sc = jnp.where(is_ctx, sc2[1:2, :], sc2[0:1, :])
    sh = jnp.where(is_ctx, sh2[1:2, :], sh2[0:1, :])
    return y * (1.0 + sc) + sh


IN_TM = 1280
IN_TN = 512
IN_SUB = 256


def _in_proj_kernel(x_ref, nw_ref, sc_ref, sh_ref, wm_ref, wg_ref, qkv_ref, rest_ref, gate_ref, a_scr):
    i = pl.program_id(0)
    j = pl.program_id(1)
    n_qkv = QKV_W // IN_TN

    @pl.when(j == 0)
    def _():
        def body(r, carry):
            rows = pl.ds(pl.multiple_of(r * IN_SUB, IN_SUB), IN_SUB)
            h = _modulated_norm(x_ref[rows, :], nw_ref[...], sc_ref[...], sh_ref[...], i * IN_TM + r * IN_SUB)
            a_scr[rows, :] = h.astype(BF16)
            gate_ref[rows, :] = jnp.dot(h, wg_ref[...], preferred_element_type=F32, precision=HIGHEST)
            return carry
        lax.fori_loop(0, IN_TM // IN_SUB, body, 0)

    acc = _dot(a_scr[...], wm_ref[...])

    @pl.when(j < n_qkv)
    def _():
        qkv_ref[...] = acc.astype(BF16)

    @pl.when(j >= n_qkv)
    def _():
        rest_ref[...] = acc


def in_projection(xa, nw, sc2, sh2, w_main, w_gate):
    t_all, d = xa.shape
    n_main = w_main.shape[1]
    n_qkv = QKV_W // IN_TN
    return pl.pallas_call(
        _in_proj_kernel,
        grid=(t_all // IN_TM, n_main // IN_TN),
        in_specs=[pl.BlockSpec((IN_TM, d), lambda i, j: (i, 0)),
                  pl.BlockSpec((1, d), lambda i, j: (0, 0)),
                  pl.BlockSpec((2, d), lambda i, j: (0, 0)),
                  pl.BlockSpec((2, d), lambda i, j: (0, 0)),
                  pl.BlockSpec((d, IN_TN), lambda i, j: (0, j)),
                  pl.BlockSpec((d, LANES), lambda i, j: (0, 0))],
        out_specs=[pl.BlockSpec((IN_TM, IN_TN), lambda i, j: (i, jnp.minimum(j, n_qkv - 1))),
                   pl.BlockSpec((IN_TM, IN_TN), lambda i, j: (i, jnp.maximum(j - n_qkv, 0))),
                   pl.BlockSpec((IN_TM, LANES), lambda i, j: (i, 0))],
        out_shape=[jax.ShapeDtypeStruct((t_all, QKV_W), BF16),
                   jax.ShapeDtypeStruct((t_all, REST_W), F32),
                   jax.ShapeDtypeStruct((t_all, LANES), F32)],
        scratch_shapes=[pltpu.VMEM((IN_TM, d), BF16)],
        compiler_params=_cparams(("arbitrary", "arbitrary")),
        name="in_projection",
    )(xa, nw, sc2, sh2, w_main, w_gate)


NA_RB = CHUNK // GRID_W


def _na_kernel(q_ref, k_ref, v_ref, bias_ref, o_ref, *, n_rows):
    rb = pl.program_id(1)
    scale = NA_DH ** -0.5
    kc = k_ref[0:CTX, :]
    vc = v_ref[0:CTX, :]

    @pl.when(rb == 0)
    def _():
        s = _dot_nt(q_ref[...], kc) * scale
        m = jnp.max(s, axis=-1, keepdims=True)
        p = jnp.exp(s - m)
        l = jnp.sum(p, axis=-1, keepdims=True)
        o_ref[...] = (_dot(p.astype(BF16), vc) / l).astype(o_ref.dtype)

    @pl.when(rb > 0)
    def _():
        for rr in range(NA_RB):
            r = (rb - 1) * NA_RB + rr
            rs = jnp.clip(r - NA_KR // 2, 0, n_rows - NA_KR)
            d0 = rs - r + NA_KR - 1
            start = pl.multiple_of(CTX + rs * GRID_W, GRID_W)
            q = q_ref[rr * GRID_W:(rr + 1) * GRID_W, :]
            kw = k_ref[pl.ds(start, NA_KR * GRID_W), :]
            vw = v_ref[pl.ds(start, NA_KR * GRID_W), :]
            s = _dot_nt(q, kw) * scale + bias_ref[0, d0]
            sc = _dot_nt(q, kc) * scale
            m = jnp.maximum(jnp.max(s, axis=-1, keepdims=True), jnp.max(sc, axis=-1, keepdims=True))
            p = jnp.exp(s - m)
            pc = jnp.exp(sc - m)
            l = jnp.sum(p, axis=-1, keepdims=True) + jnp.sum(pc, axis=-1, keepdims=True)
            o = _dot(p.astype(BF16), vw) + _dot(pc.astype(BF16), vc)
            o_ref[rr * GRID_W:(rr + 1) * GRID_W, :] = (o / l).astype(o_ref.dtype)


def na_bias_table(rpb):
    c = np.arange(GRID_W)
    cs = np.clip(c - NA_KC // 2, 0, GRID_W - NA_KC)
    kcol = np.arange(GRID_W)
    inside = (kcol[None, :] >= cs[:, None]) & (kcol[None, :] < cs[:, None] + NA_KC)
    col_off = np.clip(kcol[None, :] - c[:, None] + NA_KC - 1, 0, 2 * NA_KC - 2)
    d0 = np.arange(NA_KR)
    j = np.arange(NA_KR)
    row_off = d0[:, None] + j[None, :]
    t = rpb.astype(F32)[:, row_off[:, None, :, None], col_off[None, :, None, :]]
    t = jnp.where(jnp.asarray(inside)[None, None, :, None, :], t, NEG)
    return t.reshape(rpb.shape[0], NA_KR, GRID_W, NA_KR * GRID_W)


def neighbourhood_attention(qkv, bias_tab):
    t_all = qkv.shape[0]
    n_rows = (t_all - CTX) // GRID_W
    n_rb = t_all // CHUNK
    return pl.pallas_call(
        functools.partial(_na_kernel, n_rows=n_rows),
        grid=(NA_H, n_rb),
        in_specs=[pl.BlockSpec((CHUNK, NA_DH), lambda h, rb: (rb, h)),
                  pl.BlockSpec((t_all, NA_DH), lambda h, rb: (0, NA_H + h)),
                  pl.BlockSpec((t_all, NA_DH), lambda h, rb: (0, 2 * NA_H + h)),
                  pl.BlockSpec((1, NA_KR, GRID_W, NA_KR * GRID_W), lambda h, rb: (h, 0, 0, 0))],
        out_specs=pl.BlockSpec((CHUNK, NA_DH), lambda h, rb: (rb, h)),
        out_shape=jax.ShapeDtypeStruct((t_all, NA_W), BF16),
        compiler_params=_cparams(("parallel", "parallel")),
        name="neighbourhood_attention",
    )(qkv, qkv, qkv, bias_tab)


def _scan_chunk(d, s, n_chunks):
    return jnp.where(d == 0, s, jnp.where(s == 0, 0, n_chunks - s))


def _scan_masks(d):
    row = lax.broadcasted_iota(jnp.int32, (CHUNK, CHUNK), 0)
    col = lax.broadcasted_iota(jnp.int32, (CHUNK, CHUNK), 1)
    mask = jnp.where(d == 0, row - col, col - row) >= 0
    return mask, mask.astype(F32)


def _rope(x, cos, sin_signed):
    lane = lax.broadcasted_iota(jnp.int32, x.shape, 1)
    nf = ML_DH // 4
    partner = jnp.where((lane % (2 * nf)) < nf, pltpu.roll(x, ML_DH - nf, 1), pltpu.roll(x, nf, 1))
    return x * cos + partner * sin_signed


def _mlstm_kernel(q_ref, k_ref, v_ref, g_ref, gt_ref, gb_ref, gbt_ref, cos_ref, sin_ref, o_ref,
                  c_scr, n_scr, m_scr):
    d = pl.program_id(0)
    s = pl.program_id(1)

    @pl.when(s == 0)
    def _():
        c_scr[...] = jnp.zeros_like(c_scr)
        n_scr[...] = jnp.zeros_like(n_scr)
        m_scr[...] = jnp.zeros_like(m_scr)

    mask, mf = _scan_masks(d)
    g = g_ref[...] + gb_ref[0]
    gt = gt_ref[...] + gbt_ref[0]
    lf = -_softplus(-g)
    lft = -_softplus(-gt)
    cum = jnp.dot(mf, lf, preferred_element_type=F32, precision=HIGHEST)
    cumt = lax.dot_general(lft, mf, (((1,), (1,)), ((), ())), preferred_element_type=F32,
                           precision=HIGHEST)
    tot = jnp.sum(lf, axis=0, keepdims=True)
    cos = cos_ref[...]
    sin = sin_ref[...]

    for h in range(ML_H):
        hs = slice(h * ML_DH, (h + 1) * ML_DH)
        q = _rope(q_ref[:, hs], cos, sin)
        k = _rope(k_ref[:, hs], cos, sin) * (ML_DH ** -0.5)
        v = v_ref[:, hs]
        qb, kb, vb = q.astype(BF16), k.astype(BF16), v.astype(BF16)
        bt_col = cum[:, G_F + h:G_F + h + 1]
        bt_row = cumt[G_F + h:G_F + h + 1, :]
        ig_col = g[:, G_I + h:G_I + h + 1]
        ig_row = gt[G_I + h:G_I + h + 1, :]
        b_last = tot[:, G_F + h:G_F + h + 1]
        m_prev = m_scr[h]
        cmat = c_scr[h]
        nvec = n_scr[h]

        log_d = jnp.where(mask, bt_col - bt_row + ig_row, -jnp.inf)
        inter = bt_col + m_prev
        m_t = jnp.maximum(jnp.max(log_d, axis=-1, keepdims=True), inter)
        wts = _dot_nt(qb, kb) * jnp.exp(log_d - m_t)
        sc = jnp.exp(inter - m_t)
        num = _dot(wts.astype(BF16), vb) + _dot_nt(qb, cmat.astype(BF16)) * sc
        den = jnp.sum(wts, axis=-1, keepdims=True) + jnp.sum(q * nvec, axis=-1, keepdims=True) * sc
        den = jnp.maximum(jnp.abs(den), jnp.exp(-m_t))
        o_ref[0, :, hs] = num / den

        tail = b_last - bt_col + ig_col
        m_new = jnp.maximum(b_last + m_prev, jnp.max(tail, axis=0, keepdims=True))
        wgt = jnp.exp(tail - m_new)
        decay = jnp.exp(b_last + m_prev - m_new)
        c_scr[h] = decay * cmat + _dot_tn((v * wgt).astype(BF16), kb)
        n_scr[h] = decay * nvec + jnp.sum(wgt * k, axis=0, keepdims=True)
        m_scr[h] = m_new


def mlstm_scan(rest, gates, gates_t, gbias, gbias_t, cos_tab, sin_tab):
    t_all = rest.shape[0]
    n_chunks = t_all // CHUNK
    cm = lambda d, s: _scan_chunk(d, s, n_chunks)
    col = lambda off: off // ML_W
    return pl.pallas_call(
        _mlstm_kernel,
        grid=(2, n_chunks),
        in_specs=[pl.BlockSpec((CHUNK, ML_W), lambda d, s: (cm(d, s), col(R_MQ))),
                  pl.BlockSpec((CHUNK, ML_W), lambda d, s: (cm(d, s), col(R_MK))),
                  pl.BlockSpec((CHUNK, ML_W), lambda d, s: (cm(d, s), col(R_MV))),
                  pl.BlockSpec((CHUNK, LANES), lambda d, s: (cm(d, s), d)),
                  pl.BlockSpec((LANES, CHUNK), lambda d, s: (d, cm(d, s))),
                  pl.BlockSpec((1, 1, LANES), lambda d, s: (d, 0, 0)),
                  pl.BlockSpec((1, LANES, 1), lambda d, s: (d, 0, 0)),
                  pl.BlockSpec((CHUNK, ML_DH), lambda d, s: (cm(d, s), 0)),
                  pl.BlockSpec((CHUNK, ML_DH), lambda d, s: (cm(d, s), 0))],
        out_specs=pl.BlockSpec((1, CHUNK, ML_W), lambda d, s: (d, cm(d, s), 0)),
        out_shape=jax.ShapeDtypeStruct((2, t_all, ML_W), F32),
        scratch_shapes=[pltpu.VMEM((ML_H, ML_DH, ML_DH), F32),
                        pltpu.VMEM((ML_H, 1, ML_DH), F32),
                        pltpu.VMEM((ML_H, 1, 1), F32)],
        compiler_params=_cparams(("arbitrary", "arbitrary")),
        name="mlstm_scan",
    )(rest, rest, rest, gates, gates_t, gbias, gbias_t, cos_tab, sin_tab)


CONV_HALO = 8


def _conv_kernel(x_ref, p_ref, n_ref, w_ref, b_ref, o_ref, *, n_chunks):
    s = pl.program_id(0)
    x = x_ref[...]
    row = lax.broadcasted_iota(jnp.int32, x.shape, 0)
    prev = jnp.where(s >= 2, p_ref[...], 0.0)
    nxt = jnp.where((s >= 1) & (s <= n_chunks - 2), n_ref[...], 0.0)
    xm1 = jnp.where(row == 0, prev[CONV_HALO - 1:CONV_HALO, :], pltpu.roll(x, 1, 0))
    xm2 = jnp.where(row == 0, prev[CONV_HALO - 2:CONV_HALO - 1, :],
                    jnp.where(row == 1, prev[CONV_HALO - 1:CONV_HALO, :], pltpu.roll(x, 2, 0)))
    xp1 = jnp.where(row == CHUNK - 1, nxt[0:1, :], pltpu.roll(x, CHUNK - 1, 0))
    y = w_ref[0:1, :] * xm2 + w_ref[1:2, :] * xm1 + w_ref[2:3, :] * x + w_ref[3:4, :] * xp1 + b_ref[...]
    o_ref[...] = _silu(y)


def ssd_conv(rest, conv_w, conv_b):
    t_all = rest.shape[0]
    n_chunks = t_all // CHUNK
    hb = CHUNK // CONV_HALO
    return pl.pallas_call(
        functools.partial(_conv_kernel, n_chunks=n_chunks),
        grid=(n_chunks,),
        in_specs=[pl.BlockSpec((CHUNK, CONV_CH), lambda s: (s, 0)),
                  pl.BlockSpec((CONV_HALO, CONV_CH), lambda s: (jnp.maximum(s * hb - 1, 0), 0)),
                  pl.BlockSpec((CONV_HALO, CONV_CH), lambda s: (jnp.minimum((s + 1) * hb, n_chunks * hb - 1), 0)),
                  pl.BlockSpec((4, CONV_CH), lambda s: (0, 0)),
                  pl.BlockSpec((1, CONV_CH), lambda s: (0, 0))],
        out_specs=pl.BlockSpec((CHUNK, CONV_CH), lambda s: (s, 0)),
        out_shape=jax.ShapeDtypeStruct((t_all, CONV_CH), F32),
        compiler_params=_cparams(("parallel",)),
        name="ssd_conv",
    )(rest, rest, rest, conv_w, conv_b.reshape(1, CONV_CH))


def _ssd_kernel(x_ref, g_ref, gt_ref, gb_ref, gbt_ref, al_ref, alt_ref, o_ref, s_scr):
    d = pl.program_id(0)
    s = pl.program_id(1)

    @pl.when(s == 0)
    def _():
        s_scr[...] = jnp.zeros_like(s_scr)

    mask, mf = _scan_masks(d)
    dt = _softplus(g_ref[...] + gb_ref[0])
    dtt = _softplus(gt_ref[...] + gbt_ref[0])
    inc = dt * (-jnp.exp(al_ref[0]))
    inct = dtt * (-jnp.exp(alt_ref[0]))
    cum = jnp.dot(mf, inc, preferred_element_type=F32, precision=HIGHEST)
    cumt = lax.dot_general(inct, mf, (((1,), (1,)), ((), ())), preferred_element_type=F32, precision=HIGHEST)
    tot = jnp.sum(inc, axis=0, keepdims=True)

    hpg = SSM_H // SSM_G
    for gi in range(SSM_G):
        bm = x_ref[:, SSM_W + gi * SSM_N:SSM_W + (gi + 1) * SSM_N].astype(BF16)
        cm = x_ref[:, SSM_W + (SSM_G + gi) * SSM_N:SSM_W + (SSM_G + gi + 1) * SSM_N].astype(BF16)
        gmat = _dot_nt(cm, bm)
        for hh in range(hpg):
            h = gi * hpg + hh
            la_col = cum[:, G_DT + h:G_DT + h + 1]
            la_row = cumt[G_DT + h:G_DT + h + 1, :]
            la_last = tot[:, G_DT + h:G_DT + h + 1]
            dt_col = dt[:, G_DT + h:G_DT + h + 1]
            state = s_scr[h]
            decay = jnp.exp(jnp.where(mask, la_col - la_row, -jnp.inf))
            xdt = x_ref[:, h * SSM_P:(h + 1) * SSM_P] * dt_col
            y = (_dot((gmat * decay).astype(BF16), xdt.astype(BF16))
                 + _dot_nt(cm, state.astype(BF16)) * jnp.exp(la_col))
            o_ref[0, :, h * SSM_P:(h + 1) * SSM_P] = y
            tail = jnp.exp(la_last - la_col)
            s_scr[h] = state * jnp.exp(la_last) + _dot_tn((xdt * tail).astype(BF16), bm)


def ssd_scan(xbc_act, gates, gates_t, gbias, gbias_t, alog, alog_t):
    t_all = xbc_act.shape[0]
    n_chunks = t_all // CHUNK
    cm = lambda d, s: _scan_chunk(d, s, n_chunks)
    return pl.pallas_call(
        _ssd_kernel,
        grid=(2, n_chunks),
        in_specs=[pl.BlockSpec((CHUNK, CONV_CH), lambda d, s: (cm(d, s), 0)),
                  pl.BlockSpec((CHUNK, LANES), lambda d, s: (cm(d, s), d)),
                  pl.BlockSpec((LANES, CHUNK), lambda d, s: (d, cm(d, s))),
                  pl.BlockSpec((1, 1, LANES), lambda d, s: (d, 0, 0)),
                  pl.BlockSpec((1, LANES, 1), lambda d, s: (d, 0, 0)),
                  pl.BlockSpec((1, 1, LANES), lambda d, s: (d, 0, 0)),
                  pl.BlockSpec((1, LANES, 1), lambda d, s: (d, 0, 0))],
        out_specs=pl.BlockSpec((1, CHUNK, SSM_W), lambda d, s: (d, cm(d, s), 0)),
        out_shape=jax.ShapeDtypeStruct((2, t_all, SSM_W), F32),
        scratch_shapes=[pltpu.VMEM((SSM_H, SSM_P, SSM_N), F32)],
        compiler_params=_cparams(("arbitrary", "arbitrary")),
        name="ssd_scan",
    )(xbc_act, gates, gates_t, gbias, gbias_t, alog, alog_t)


OUT_TM = 640
OUT_TN = 512
OUT_SUB = 128


def _out_proj_kernel(ona_ref, hs_ref, mo_ref, ys_ref, xs_ref, z_ref, mlw_ref, dsk_ref, ssw_ref,
                     w_ref, x_ref, g_ref, o_ref, a_scr):
    i = pl.program_id(0)
    j = pl.program_id(1)

    @pl.when(j == 0)
    def _():
        def body(r, carry):
            rows = pl.ds(pl.multiple_of(r * OUT_SUB, OUT_SUB), OUT_SUB)
            a_scr[rows, 0:NA_W] = ona_ref[rows, :]
            hsum = hs_ref[0, rows, :] + hs_ref[1, rows, :]
            gate = 1.0 / (1.0 + jnp.exp(-mo_ref[rows, :]))
            for h in range(ML_H):
                cs = slice(h * ML_DH, (h + 1) * ML_DH)
                hh = hsum[:, cs]
                mu = jnp.mean(hh, axis=-1, keepdims=True)
                var = jnp.mean(jnp.square(hh - mu), axis=-1, keepdims=True)
                hn = (hh - mu) * lax.rsqrt(var + EPS) * mlw_ref[:, cs]
                a_scr[rows, NA_W + h * ML_DH:NA_W + (h + 1) * ML_DH] = (gate[:, cs] * hn).astype(BF16)
            y = ys_ref[0, rows, :] + ys_ref[1, rows, :] + dsk_ref[...] * xs_ref[rows, :]
            y = y * _silu(z_ref[rows, :])
            gw = SSM_W // SSM_G
            for gi in range(SSM_G):
                cs = slice(gi * gw, (gi + 1) * gw)
                yg = y[:, cs]
                yn = yg * lax.rsqrt(jnp.mean(yg * yg, axis=-1, keepdims=True) + EPS) * ssw_ref[:, cs]
                a_scr[rows, NA_W + ML_W + gi * gw:NA_W + ML_W + (gi + 1) * gw] = yn.astype(BF16)
            return carry
        lax.fori_loop(0, OUT_TM // OUT_SUB, body, 0)

    acc = _dot(a_scr[...], w_ref[...])
    is_ctx = (i * OUT_TM + lax.broadcasted_iota(jnp.int32, (OUT_TM, 1), 0)) < CTX
    gate1 = jnp.where(is_ctx, g_ref[1:2, :], g_ref[0:1, :])
    o_ref[...] = x_ref[...] + gate1 * acc


def out_projection(o_na, hs, rest, ys, xbc_act, ml_norm_w, d_skip_vec, ssm_norm_w, w_out, xa, g1):
    t_all, d = xa.shape
    cw = lambda off: off // ML_W
    return pl.pallas_call(
        _out_proj_kernel,
        grid=(t_all // OUT_TM, d // OUT_TN),
        in_specs=[pl.BlockSpec((OUT_TM, NA_W), lambda i, j: (i, 0)),
                  pl.BlockSpec((2, OUT_TM, ML_W), lambda i, j: (0, i, 0)),
                  pl.BlockSpec((OUT_TM, ML_W), lambda i, j: (i, cw(R_MO))),
                  pl.BlockSpec((2, OUT_TM, SSM_W), lambda i, j: (0, i, 0)),
                  pl.BlockSpec((OUT_TM, SSM_W), lambda i, j: (i, 0)),
                  pl.BlockSpec((OUT_TM, SSM_W), lambda i, j: (i, cw(R_Z))),
                  pl.BlockSpec((1, ML_W), lambda i, j: (0, 0)),
                  pl.BlockSpec((1, SSM_W), lambda i, j: (0, 0)),
                  pl.BlockSpec((1, SSM_W), lambda i, j: (0, 0)),
                  pl.BlockSpec((d, OUT_TN), lambda i, j: (0, j)),
                  pl.BlockSpec((OUT_TM, OUT_TN), lambda i, j: (i, j)),
                  pl.BlockSpec((2, OUT_TN), lambda i, j: (0, j))],
        out_specs=pl.BlockSpec((OUT_TM, OUT_TN), lambda i, j: (i, j)),
        out_shape=jax.ShapeDtypeStruct((t_all, d), F32),
        scratch_shapes=[pltpu.VMEM((OUT_TM, d), BF16)],
        compiler_params=_cparams(("arbitrary", "arbitrary")),
        name="out_projection",
    )(o_na, hs, rest, ys, xbc_act, rest, ml_norm_w, d_skip_vec, ssm_norm_w, w_out, xa, g1)


RT_TM = 640
RT_SUB = 128


def _router_kernel(x_ref, nw_ref, sc_ref, sh_ref, wr_ref, br_ref, h_ref, r_ref, cnt_ref, run_scr):
    i = pl.program_id(0)

    @pl.when(i == 0)
    def _():
        run_scr[...] = jnp.zeros_like(run_scr)

    t_row = lax.broadcasted_iota(jnp.int32, (RT_SUB, RT_SUB), 0)
    t_col = lax.broadcasted_iota(jnp.int32, (RT_SUB, RT_SUB), 1)
    earlier = (t_col < t_row).astype(BF16)

    def body(r, run):
        rows = pl.ds(pl.multiple_of(r * RT_SUB, RT_SUB), RT_SUB)
        h = _modulated_norm(x_ref[rows, :], nw_ref[...], sc_ref[...], sh_ref[...], i * RT_TM + r * RT_SUB)
        h_ref[rows, :] = h.astype(BF16)
        logit = jnp.dot(h, wr_ref[...], preferred_element_type=F32, precision=HIGHEST) + br_ref[...]
        lane = lax.broadcasted_iota(jnp.int32, logit.shape, 1)
        big = jnp.int32(LANES)
        is_g = lane < MOE_GROUPS
        lg = jnp.where(is_g, logit, -jnp.inf)
        gmax = jnp.max(lg, axis=-1, keepdims=True)
        g_sel = jnp.min(jnp.where(is_g & (lg == gmax), lane, big), axis=-1, keepdims=True)
        g_w = 1.0 / jnp.sum(jnp.exp(lg - gmax), axis=-1, keepdims=True)
        lo = MOE_GROUPS + g_sel * MOE_PER_GROUP
        in_g = (lane >= lo) & (lane < lo + MOE_PER_GROUP)
        le = jnp.where(in_g, logit, -jnp.inf)
        v1 = jnp.max(le, axis=-1, keepdims=True)
        i1 = jnp.min(jnp.where(in_g & (le == v1), lane, big), axis=-1, keepdims=True)
        le2 = jnp.where(lane == i1, -jnp.inf, le)
        v2 = jnp.max(le2, axis=-1, keepdims=True)
        i2 = jnp.min(jnp.where(in_g & (lane != i1) & (le2 == v2), lane, big), axis=-1, keepdims=True)
        e2 = jnp.exp(v2 - v1)
        w1 = g_w / (1.0 + e2)
        w2 = g_w * e2 / (1.0 + e2)
        oh1 = (lane == i1 - MOE_GROUPS).astype(F32)
        oh2 = (lane == i2 - MOE_GROUPS).astype(F32)
        oh = oh1 + oh2
        before = _dot(earlier, oh.astype(BF16)) + run
        rank1 = jnp.sum(before * oh1, axis=-1, keepdims=True)
        rank2 = jnp.sum(before * oh2, axis=-1, keepdims=True)
        out = jnp.where(lane == 0, (i1 - MOE_GROUPS).astype(F32),
                        jnp.where(lane == 1, (i2 - MOE_GROUPS).astype(F32),
                                  jnp.where(lane == 2, w1,
                                            jnp.where(lane == 3, w2,
                                                      jnp.where(lane == 4, rank1,
                                                                jnp.where(lane == 5, rank2, 0.0))))))
        r_ref[rows, :] = out
        return run + jnp.sum(oh, axis=0, keepdims=True)
    run = lax.fori_loop(0, RT_TM // RT_SUB, body, run_scr[...])
    run_scr[...] = run
    cnt_ref[...] = jnp.broadcast_to(run, cnt_ref.shape)


def moe_router(xa, nw, sc2, sh2, w_route, b_route):
    t_all, d = xa.shape
    return pl.pallas_call(
        _router_kernel,
        grid=(t_all // RT_TM,),
        in_specs=[pl.BlockSpec((RT_TM, d), lambda i: (i, 0)),
                  pl.BlockSpec((1, d), lambda i: (0, 0)),
                  pl.BlockSpec((2, d), lambda i: (0, 0)),
                  pl.BlockSpec((2, d), lambda i: (0, 0)),
                  pl.BlockSpec((d, LANES), lambda i: (0, 0)),
                  pl.BlockSpec((1, LANES), lambda i: (0, 0))],
        out_specs=[pl.BlockSpec((RT_TM, d), lambda i: (i, 0)),
                   pl.BlockSpec((RT_TM, LANES), lambda i: (i, 0)),
                   pl.BlockSpec((8, LANES), lambda i: (0, 0))],
        out_shape=[jax.ShapeDtypeStruct((t_all, d), BF16),
                   jax.ShapeDtypeStruct((t_all, LANES), F32),
                   jax.ShapeDtypeStruct((8, LANES), F32)],
        scratch_shapes=[pltpu.VMEM((1, LANES), F32)],
        compiler_params=_cparams(("arbitrary",)),
        name="moe_router",
    )(xa, nw, sc2, sh2, w_route, b_route)


def _expert_kernel(be_ref, nu_ref, x_ref, w1_ref, w3_ref, w2_ref, o_ref):
    b = pl.program_id(0)

    @pl.when(b < nu_ref[0])
    def _():
        x = x_ref[...]
        hid = _silu(_dot(x, w1_ref[0])) * _dot(x, w3_ref[0])
        o_ref[...] = _dot(hid.astype(BF16), w2_ref[0])

    @pl.when(b >= nu_ref[0])
    def _():
        o_ref[...] = jnp.zeros_like(o_ref)


def expert_blocks(block_e, n_used, xs, w1, w3, w2):
    n_slots, d = xs.shape
    ff = w1.shape[-1]
    n_blocks = n_slots // MOE_BM
    return pl.pallas_call(
        _expert_kernel,
        grid_spec=pltpu.PrefetchScalarGridSpec(
            num_scalar_prefetch=2,
            grid=(n_blocks,),
            in_specs=[pl.BlockSpec((MOE_BM, d), lambda b, be, nu: (b, 0)),
                      pl.BlockSpec((1, d, ff), lambda b, be, nu: (be[b], 0, 0)),
                      pl.BlockSpec((1, d, ff), lambda b, be, nu: (be[b], 0, 0)),
                      pl.BlockSpec((1, ff, d), lambda b, be, nu: (be[b], 0, 0))],
            out_specs=pl.BlockSpec((MOE_BM, d), lambda b, be, nu: (b, 0))),
        out_shape=jax.ShapeDtypeStruct((n_slots, d), F32),
        compiler_params=_cparams(("arbitrary",)),
        name="moe_experts",
    )(block_e, n_used, xs, w1, w3, w2)


def _combine_kernel(dest_ref, x_ref, r_ref, g_ref, fw_ref, y_hbm, o_ref, ybuf, sem, *,
                    row_block0, n_tiles, final_norm):
    i = pl.program_id(0)

    def start_gather(tile, slot):
        base = (tile + row_block0) * (2 * CHUNK)

        def body(t, carry):
            for k in range(2):
                row = dest_ref[base + 2 * t + k]
                pltpu.make_async_copy(y_hbm.at[pl.ds(row, 1), :],
                                      ybuf.at[slot, pl.ds(k * CHUNK + t, 1), :], sem.at[slot]).start()
            return carry
        lax.fori_loop(0, CHUNK, body, 0)

    @pl.when(i == 0)
    def _():
        start_gather(0, 0)

    @pl.when(i + 1 < n_tiles)
    def _():
        start_gather(i + 1, (i + 1) % 2)

    slot = i % 2
    pltpu.make_async_copy(y_hbm.at[pl.ds(0, 2 * CHUNK), :], ybuf.at[slot], sem.at[slot]).wait()

    rt = r_ref[...]
    y = rt[:, 2:3] * ybuf[slot, 0:CHUNK, :] + rt[:, 3:4] * ybuf[slot, CHUNK:2 * CHUNK, :]
    is_ctx = ((i + row_block0) * CHUNK + lax.broadcasted_iota(jnp.int32, (CHUNK, 1), 0)) < CTX
    gate2 = jnp.where(is_ctx, g_ref[1:2, :], g_ref[0:1, :])
    x = x_ref[...] + gate2 * y
    if final_norm:
        x = x * lax.rsqrt(jnp.mean(x * x, axis=-1, keepdims=True) + EPS) * fw_ref[...]
    o_ref[...] = x


def moe_combine(dest, xa, route, g2, final_w, y, final_norm):
    t_all, d = xa.shape
    rb0 = CTX // CHUNK if final_norm else 0
    n_out = t_all // CHUNK - rb0
    return pl.pallas_call(
        functools.partial(_combine_kernel, row_block0=rb0, n_tiles=n_out, final_norm=final_norm),
        grid_spec=pltpu.PrefetchScalarGridSpec(
            num_scalar_prefetch=1,
            grid=(n_out,),
            in_specs=[pl.BlockSpec((CHUNK, d), lambda i, ds: (i + rb0, 0)),
                      pl.BlockSpec((CHUNK, LANES), lambda i, ds: (i + rb0, 0)),
                      pl.BlockSpec((2, d), lambda i, ds: (0, 0)),
                      pl.BlockSpec((1, d), lambda i, ds: (0, 0)),
                      pl.BlockSpec(memory_space=pl.ANY)],
            out_specs=pl.BlockSpec((CHUNK, d), lambda i, ds: (i, 0)),
            scratch_shapes=[pltpu.VMEM((2, 2 * CHUNK, d), F32),
                            pltpu.SemaphoreType.DMA((2,))]),
        out_shape=jax.ShapeDtypeStruct((n_out * CHUNK, d), F32),
        compiler_params=_cparams(("arbitrary",)),
        name="moe_combine",
    )(dest, xa, route, g2, final_w, y)


def moe_dispatch(route, counts):
    n = route.shape[0]
    n_assign = 2 * n
    e_flat = route[:, 0:2].astype(jnp.int32).reshape(-1)
    rank_flat = route[:, 4:6].astype(jnp.int32).reshape(-1)
    padded = (counts + MOE_BM - 1) // MOE_BM * MOE_BM
    pends = jnp.cumsum(padded)
    pstarts = pends - padded
    dest = jnp.take(pstarts, e_flat) + rank_flat
    n_blocks = -(-n_assign // MOE_BM) + N_EXPERTS
    n_slots = n_blocks * MOE_BM
    tok_flat = jnp.arange(n_assign, dtype=jnp.int32) // 2
    slot_tok = jnp.zeros((n_slots,), jnp.int32).at[dest].set(tok_flat)
    block_start = jnp.arange(n_blocks, dtype=jnp.int32) * MOE_BM
    block_e = jnp.minimum(jnp.sum((pends[None, :] <= block_start[:, None]).astype(jnp.int32), axis=1),
                          N_EXPERTS - 1)
    n_used = (pends[-1] // MOE_BM).astype(jnp.int32).reshape(1)
    return dest, slot_tok, block_e, n_used


def hier_moe(xa, nw, sc2, sh2, g2, w_route, b_route, w1, w3, w2, final_w, final_norm):
    h2, route, cnt = moe_router(xa, nw, sc2, sh2, w_route, b_route)
    dest, slot_tok, block_e, n_used = moe_dispatch(route, cnt[0, :N_EXPERTS].astype(jnp.int32))
    xs = jnp.take(h2, slot_tok, axis=0)
    y = expert_blocks(block_e, n_used, xs, w1, w3, w2)
    return moe_combine(dest, xa, route, g2, final_w, y, final_norm)


def rope_tables(t_lat):
    nf = ML_DH // 4
    inv = ROPE_THETA ** (-jnp.arange(nf, dtype=F32) / nf)
    t_idx = jnp.arange(t_lat)
    ang_r = (t_idx // GRID_W).astype(F32)[:, None] * inv
    ang_c = (t_idx % GRID_W).astype(F32)[:, None] * inv
    cos = jnp.concatenate([jnp.cos(ang_r)] * 2 + [jnp.cos(ang_c)] * 2, axis=-1)
    sin = jnp.concatenate([-jnp.sin(ang_r), jnp.sin(ang_r), -jnp.sin(ang_c), jnp.sin(ang_c)], axis=-1)
    cos = jnp.concatenate([jnp.ones((CTX, ML_DH), F32), cos], axis=0)
    sin = jnp.concatenate([jnp.zeros((CTX, ML_DH), F32), sin], axis=0)
    return cos, sin


def _pad_lanes(v):
    return jnp.pad(v, [(0, 0)] * (v.ndim - 1) + [(0, LANES - v.shape[-1])])


def kernel(x, c, ctx, c_ctx, w_mod, b_mod, norm1_w, norm2_w, w_in, w_out, na_rpb, ml_gate_b, ml_norm_w, ssm_conv_w, ssm_conv_b, ssm_dt_bias, ssm_a_log, ssm_d, ssm_norm_w, router_g_w, router_g_b, router_e_w, router_e_b, moe_w1, moe_w3, moe_w2, final_norm_w):
    depth = w_mod.shape[0]
    t_lat = x.shape[1]
    d = D_MODEL
    xa = jnp.concatenate([ctx[0], x[0]], axis=0)

    cond = jnp.zeros((8, d), F32).at[0].set(c[0]).at[1].set(c_ctx)
    mod = adaln_modulation(cond, w_mod, b_mod)[:, 0:2, :].reshape(depth, 2, 6, d)
    cos_tab, sin_tab = rope_tables(t_lat)

    o_q, o_mq, o_mg, o_z, o_xbc, o_dt = 0, 3 * NA_W, 3 * NA_W + 4 * ML_W, 3 * NA_W + 4 * ML_W + 4 * ML_H, \
        3 * NA_W + 4 * ML_W + 4 * ML_H + SSM_W, 3 * NA_W + 4 * ML_W + 4 * ML_H + SSM_W + CONV_CH

    out = None
    for l in range(depth):
        last = l == depth - 1
        sh1, sc1, g1, sh2, sc2, g2 = (mod[l, :, p, :] for p in range(6))
        wl = w_in[l]
        w_main = jnp.concatenate([wl[:, o_q:o_mq], wl[:, o_xbc:o_dt], wl[:, o_mq:o_mg], wl[:, o_z:o_xbc]],
                                 axis=1).astype(BF16)
        w_gate = _pad_lanes(jnp.concatenate([wl[:, o_mg:o_z], wl[:, o_dt:]], axis=1))

        qkv, rest, gate = in_projection(xa, norm1_w[l][None], sc1, sh1, w_main, w_gate)

        gdir = [jnp.concatenate([gate[:, 8 * dd:8 * dd + 8], gate[:, 16 + 8 * dd:24 + 8 * dd]], axis=1)
                for dd in range(2)]
        gates = jnp.concatenate([_pad_lanes(gd) for gd in gdir], axis=1)
        gates_t = gates.T
        gb = ml_gate_b[l]
        gbias = _pad_lanes(jnp.stack([jnp.concatenate([gb[2 * dd], gb[2 * dd + 1], ssm_dt_bias[l, dd]])
                                      for dd in range(2)]))
        alog = _pad_lanes(jnp.stack([jnp.concatenate([jnp.zeros((8,), F32), ssm_a_log[l, dd]])
                                     for dd in range(2)]))

        o_na = neighbourhood_attention(qkv, na_bias_table(na_rpb[l]))
        hs = mlstm_scan(rest, gates, gates_t, gbias[:, None, :], gbias[:, :, None], cos_tab, sin_tab)
        xbc_act = ssd_conv(rest, ssm_conv_w[l], ssm_conv_b[l])
        ys = ssd_scan(xbc_act, gates, gates_t, gbias[:, None, :], gbias[:, :, None],
                      alog[:, None, :], alog[:, :, None])

        xa = out_projection(o_na, hs, rest, ys, xbc_act, ml_norm_w[l][None],
                            jnp.repeat(ssm_d[l], SSM_P)[None], ssm_norm_w[l][None],
                            w_out[l].astype(BF16), xa, g1)

        w_route = _pad_lanes(jnp.concatenate([router_g_w[l], router_e_w[l]], axis=1))
        b_route = _pad_lanes(jnp.concatenate([router_g_b[l], router_e_b[l]])[None])
        res = hier_moe(xa, norm2_w[l][None], sc2, sh2, g2, w_route, b_route,
                       moe_w1[l].astype(BF16), moe_w3[l].astype(BF16), moe_w2[l].astype(BF16),
                       final_norm_w[None], last)
        if last:
            out = res
        else:
            xa = res
    return out[None]
```

```python
import functools

import jax
import jax.numpy as jnp
import numpy as np
from jax import lax
from jax.experimental import pallas as pl
from jax.experimental.pallas import tpu as pltpu

F32 = jnp.float32
BF16 = jnp.bfloat16
HIGHEST = lax.Precision.HIGHEST

D_MODEL = 2048
GRID_W = 64
CTX = 256
NA_DH = 128
NA_W = 1024
NA_H = 8
NA_KR = 8
NA_KC = 16
ML_DH = 128
ML_W = 512
ML_H = 4
SSM_P = 64
SSM_W = 512
SSM_H = 8
SSM_G = 2
SSM_N = 128
CONV_CH = SSM_W + 2 * SSM_G * SSM_N
MOE_GROUPS = 4
MOE_PER_GROUP = 8
N_EXPERTS = 32
MOE_FF = 1024
ROPE_THETA = 10000.0
EPS = 1e-6

CHUNK = 256
LANES = 128
NEG = -1e30
MOE_BM = 256
VMEM_LIMIT = 56 * 1024 * 1024

R_XBC, R_MQ, R_MK, R_MV, R_MO, R_Z = 0, 1024, 1536, 2048, 2560, 3072
REST_W = 3584
QKV_W = 3 * NA_W
G_I, G_F, G_DT = 0, 4, 8
G_DIR = 16


def _cparams(sem):
    return pltpu.CompilerParams(dimension_semantics=sem, vmem_limit_bytes=VMEM_LIMIT)


def _silu(x):
    return x / (1.0 + jnp.exp(-x))


def _softplus(x):
    return jnp.maximum(x, 0.0) + jnp.log1p(jnp.exp(-jnp.abs(x)))


def _dot(a, b):
    return jnp.dot(a, b, preferred_element_type=F32)


def _dot_nt(a, b):
    return lax.dot_general(a, b, (((1,), (1,)), ((), ())), preferred_element_type=F32)


def _dot_tn(a, b):
    return lax.dot_general(a, b, (((0,), (0,)), ((), ())), preferred_element_type=F32)


def _mod_kernel(c_ref, w_ref, b_ref, o_ref):
    o_ref[0] = jnp.dot(_silu(c_ref[...]), w_ref[0], preferred_element_type=F32, precision=HIGHEST) + b_ref[0]


def adaln_modulation(cond, w_mod, b_mod):
    depth, d, n = w_mod.shape
    tn = 512
    return pl.pallas_call(
        _mod_kernel,
        grid=(depth, n // tn),
        in_specs=[pl.BlockSpec((8, d), lambda l, j: (0, 0)),
                  pl.BlockSpec((1, d, tn), lambda l, j: (l, 0, j)),
                  pl.BlockSpec((1, 1, tn), lambda l, j: (l, 0, j))],
        out_specs=pl.BlockSpec((1, 8, tn), lambda l, j: (l, 0, j)),
        out_shape=jax.ShapeDtypeStruct((depth, 8, n), F32),
        compiler_params=_cparams(("parallel", "parallel")),
        name="adaln_modulation",
    )(cond, w_mod, b_mod.reshape(depth, 1, n))


def _modulated_norm(x, nw, sc2, sh2, row0):
    r = x.shape[0]
    y = x * lax.rsqrt(jnp.mean(x * x, axis=-1, keepdims=True) + EPS) * nw
    is_ctx = (row0 + lax.broadcasted_iota(jnp.int32, (r, 1), 0)) < CTX
    sc = jnp.where(is_ctx, sc2[1:2, :], sc2[0:1, :])
    sh = jnp.where(is_ctx, sh2[1:2, :], sh2[0:1, :])
    return y * (1.0 + sc) + sh


IN_TM = 1280
IN_TN = 512
IN_SUB = 256


def _in_proj_kernel(x_ref, nw_ref, sc_ref, sh_ref, wm_ref, wg_ref, qkv_ref, rest_ref, gate_ref, gate_t_ref, a_scr):
    i = pl.program_id(0)
    j = pl.program_id(1)
    n_qkv = QKV_W // IN_TN

    @pl.when(j == 0)
    def _():
        for r in range(IN_TM // IN_SUB):
            rows = slice(r * IN_SUB, (r + 1) * IN_SUB)
            h = _modulated_norm(x_ref[rows, :], nw_ref[...], sc_ref[...], sh_ref[...], i * IN_TM + r * IN_SUB)
            a_scr[rows, :] = h.astype(BF16)
            gate = jnp.dot(h, wg_ref[0], preferred_element_type=F32, precision=HIGHEST)
            gate_ref[rows, :] = gate
            gate_t_ref[:, rows] = gate.T

    acc = _dot(a_scr[...], wm_ref[0])

    @pl.when(j < n_qkv)
    def _():
        qkv_ref[...] = acc.astype(BF16)

    @pl.when(j >= n_qkv)
    def _():
        rest_ref[...] = acc


def in_projection(xa, nw, sc2, sh2, w_main, w_gate, layer):
    t_all, d = xa.shape
    n_main = w_main.shape[2]
    n_qkv = QKV_W // IN_TN
    return pl.pallas_call(
        _in_proj_kernel,
        grid=(t_all // IN_TM, n_main // IN_TN),
        in_specs=[pl.BlockSpec((IN_TM, d), lambda i, j: (i, 0)),
                  pl.BlockSpec((1, d), lambda i, j: (0, 0)),
                  pl.BlockSpec((2, d), lambda i, j: (0, 0)),
                  pl.BlockSpec((2, d), lambda i, j: (0, 0)),
                  pl.BlockSpec((1, d, IN_TN), lambda i, j: (layer, 0, j)),
                  pl.BlockSpec((1, d, LANES), lambda i, j: (layer, 0, 0))],
        out_specs=[pl.BlockSpec((IN_TM, IN_TN), lambda i, j: (i, jnp.minimum(j, n_qkv - 1))),
                   pl.BlockSpec((IN_TM, IN_TN), lambda i, j: (i, jnp.maximum(j - n_qkv, 0))),
                   pl.BlockSpec((IN_TM, LANES), lambda i, j: (i, 0)),
                   pl.BlockSpec((LANES, IN_TM), lambda i, j: (0, i))],
        out_shape=[jax.ShapeDtypeStruct((t_all, QKV_W), BF16),
                   jax.ShapeDtypeStruct((t_all, REST_W), F32),
                   jax.ShapeDtypeStruct((t_all, LANES), F32),
                   jax.ShapeDtypeStruct((LANES, t_all), F32)],
        scratch_shapes=[pltpu.VMEM((IN_TM, d), BF16)],
        compiler_params=_cparams(("arbitrary", "arbitrary")),
        name="in_projection",
    )(xa, nw, sc2, sh2, w_main, w_gate)


NA_RB = CHUNK // GRID_W
NA_UR = 12


def _na_kernel(q_ref, k_ref, v_ref, bias_ref, o_ref, *, n_rows):
    rb = pl.program_id(1)
    scale = NA_DH ** -0.5
    kc = k_ref[0:CTX, :]
    vc = v_ref[0:CTX, :]
    q = q_ref[...]

    @pl.when(rb == 0)
    def _():
        s = _dot_nt(q, kc) * scale
        m = jnp.max(s, axis=-1, keepdims=True)
        p = jnp.exp(s - m)
        l = jnp.sum(p, axis=-1, keepdims=True)
        o_ref[...] = (_dot(p.astype(BF16), vc) / l).astype(o_ref.dtype)

    @pl.when(rb > 0)
    def _():
        r0 = (rb - 1) * NA_RB
        u0 = jnp.clip(r0 - NA_KR // 2, 0, n_rows - NA_UR)
        case = jnp.where(r0 == 0, 1, jnp.where(r0 == n_rows - NA_RB, 2, 0))
        start = pl.multiple_of(CTX + u0 * GRID_W, GRID_W)
        kw = k_ref[pl.ds(start, NA_UR * GRID_W), :]
        vw = v_ref[pl.ds(start, NA_UR * GRID_W), :]
        s = _dot_nt(q, kw) * scale + bias_ref[0, case]
        sc = _dot_nt(q, kc) * scale
        m = jnp.maximum(jnp.max(s, axis=-1, keepdims=True), jnp.max(sc, axis=-1, keepdims=True))
        p = jnp.exp(s - m)
        pc = jnp.exp(sc - m)
        l = jnp.sum(p, axis=-1, keepdims=True) + jnp.sum(pc, axis=-1, keepdims=True)
        o = _dot(p.astype(BF16), vw) + _dot(pc.astype(BF16), vc)
        o_ref[...] = (o / l).astype(o_ref.dtype)


def na_bias_table(rpb):
    h = rpb.shape[0]
    c = np.arange(GRID_W)
    cs = np.clip(c - NA_KC // 2, 0, GRID_W - NA_KC)
    kcol = np.arange(GRID_W)
    inside = (kcol[None, :] >= cs[:, None]) & (kcol[None, :] < cs[:, None] + NA_KC)
    off = kcol[None, :] - c[:, None] + NA_KC - 1
    onehot = (off[None] == np.arange(2 * NA_KC - 1)[:, None, None]) & inside[None]
    band = jnp.einsum('hro,ock->hrck', rpb.astype(F32), jnp.asarray(onehot, F32), precision=HIGHEST)
    band = band + jnp.asarray(np.where(inside, 0.0, NEG), F32)
    neg = jnp.full((h, GRID_W, GRID_W), NEG, F32)
    cases = ([(rr, NA_KR // 2 - 1) for rr in range(NA_RB)],
             [(0, NA_KR - 1 - rr) for rr in range(NA_RB)],
             [(NA_UR - NA_KR, NA_KR // 2 - 1 - rr) for rr in range(NA_RB)])
    tabs = []
    for case in cases:
        rows = []
        for w_off, d0 in case:
            rows.append(jnp.concatenate(
                [band[:, d0 + u - w_off] if 0 <= u - w_off < NA_KR else neg for u in range(NA_UR)], axis=-1))
        tabs.append(jnp.concatenate(rows, axis=1))
    return jnp.stack(tabs, axis=1)


def neighbourhood_attention(qkv, bias_tab):
    t_all = qkv.shape[0]
    n_rows = (t_all - CTX) // GRID_W
    n_rb = t_all // CHUNK
    return pl.pallas_call(
        functools.partial(_na_kernel, n_rows=n_rows),
        grid=(NA_H, n_rb),
        in_specs=[pl.BlockSpec((CHUNK, NA_DH), lambda h, rb: (rb, h)),
                  pl.BlockSpec((t_all, NA_DH), lambda h, rb: (0, NA_H + h)),
                  pl.BlockSpec((t_all, NA_DH), lambda h, rb: (0, 2 * NA_H + h)),
                  pl.BlockSpec((1, 3, CHUNK, NA_UR * GRID_W), lambda h, rb: (h, 0, 0, 0))],
        out_specs=pl.BlockSpec((CHUNK, NA_DH), lambda h, rb: (rb, h)),
        out_shape=jax.ShapeDtypeStruct((t_all, NA_W), BF16),
        compiler_params=_cparams(("parallel", "parallel")),
        name="neighbourhood_attention",
    )(qkv, qkv, qkv, bias_tab)


def _scan_chunk(d, s, n_chunks):
    return jnp.where(d == 0, s, jnp.where(s == 0, 0, n_chunks - s))


def _scan_masks(d):
    row = lax.broadcasted_iota(jnp.int32, (CHUNK, CHUNK), 0)
    col = lax.broadcasted_iota(jnp.int32, (CHUNK, CHUNK), 1)
    mask = jnp.where(d == 0, row - col, col - row) >= 0
    return mask, mask.astype(F32)


def _direction_gates(d, g_ref, gt_ref):
    g = g_ref[...]
    gt = gt_ref[...]
    g = jnp.where(d == 0, g, pltpu.roll(g, LANES - G_DIR, 1))
    gt = jnp.where(d == 0, gt, pltpu.roll(gt, LANES - G_DIR, 0))
    return g, gt


def _rope(x, cos, sin_signed):
    lane = lax.broadcasted_iota(jnp.int32, x.shape, 1)
    nf = ML_DH // 4
    partner = jnp.where((lane % (2 * nf)) < nf, pltpu.roll(x, ML_DH - nf, 1), pltpu.roll(x, nf, 1))
    return x * cos + partner * sin_signed


def _mlstm_kernel(q_ref, k_ref, v_ref, g_ref, gt_ref, gb_ref, gbt_ref, cos_ref, sin_ref, o_ref,
                  c_scr, n_scr, m_scr):
    d = pl.program_id(0)
    s = pl.program_id(1)

    @pl.when(s == 0)
    def _():
        c_scr[...] = jnp.zeros_like(c_scr)
        n_scr[...] = jnp.zeros_like(n_scr)
        m_scr[...] = jnp.zeros_like(m_scr)

    mask, mf = _scan_masks(d)
    g, gt = _direction_gates(d, g_ref, gt_ref)
    g = g + gb_ref[0]
    gt = gt + gbt_ref[0]
    lf = -_softplus(-g)
    lft = -_softplus(-gt)
    cum = jnp.dot(mf, lf, preferred_element_type=F32, precision=HIGHEST)
    cumt = lax.dot_general(lft, mf, (((1,), (1,)), ((), ())), preferred_element_type=F32,
                           precision=HIGHEST)
    tot = jnp.sum(lf, axis=0, keepdims=True)
    cos = cos_ref[...]
    sin = sin_ref[...]

    for h in range(ML_H):
        hs = slice(h * ML_DH, (h + 1) * ML_DH)
        q = _rope(q_ref[:, hs], cos, sin)
        k = _rope(k_ref[:, hs], cos, sin) * (ML_DH ** -0.5)
        v = v_ref[:, hs]
        qb, kb, vb = q.astype(BF16), k.astype(BF16), v.astype(BF16)
        bt_col = cum[:, G_F + h:G_F + h + 1]
        bt_row = cumt[G_F + h:G_F + h + 1, :]
        ig_col = g[:, G_I + h:G_I + h + 1]
        ig_row = gt[G_I + h:G_I + h + 1, :]
        b_last = tot[:, G_F + h:G_F + h + 1]
        m_prev = m_scr[h]
        cmat = c_scr[h]
        nvec = n_scr[h]

        log_d = jnp.where(mask, bt_col - bt_row + ig_row, -jnp.inf)
        inter = bt_col + m_prev
        m_t = jnp.maximum(jnp.max(log_d, axis=-1, keepdims=True), inter)
        wts = _dot_nt(qb, kb) * jnp.exp(log_d - m_t)
        sc = jnp.exp(inter - m_t)
        num = _dot(wts.astype(BF16), vb) + _dot_nt(qb, cmat.astype(BF16)) * sc
        den = jnp.sum(wts, axis=-1, keepdims=True) + jnp.sum(q * nvec, axis=-1, keepdims=True) * sc
        den = jnp.maximum(jnp.abs(den), jnp.exp(-m_t))
        o_ref[0, :, hs] = num / den

        tail = b_last - bt_col + ig_col
        m_new = jnp.maximum(b_last + m_prev, jnp.max(tail, axis=0, keepdims=True))
        wgt = jnp.exp(tail - m_new)
        decay = jnp.exp(b_last + m_prev - m_new)
        c_scr[h] = decay * cmat + _dot_tn((v * wgt).astype(BF16), kb)
        n_scr[h] = decay * nvec + jnp.sum(wgt * k, axis=0, keepdims=True)
        m_scr[h] = m_new


def mlstm_scan(rest, gates, gates_t, gbias, gbias_t, cos_tab, sin_tab):
    t_all = rest.shape[0]
    n_chunks = t_all // CHUNK
    cm = lambda d, s: _scan_chunk(d, s, n_chunks)
    col = lambda off: off // ML_W
    return pl.pallas_call(
        _mlstm_kernel,
        grid=(2, n_chunks),
        in_specs=[pl.BlockSpec((CHUNK, ML_W), lambda d, s: (cm(d, s), col(R_MQ))),
                  pl.BlockSpec((CHUNK, ML_W), lambda d, s: (cm(d, s), col(R_MK))),
                  pl.BlockSpec((CHUNK, ML_W), lambda d, s: (cm(d, s), col(R_MV))),
                  pl.BlockSpec((CHUNK, LANES), lambda d, s: (cm(d, s), 0)),
                  pl.BlockSpec((LANES, CHUNK), lambda d, s: (0, cm(d, s))),
                  pl.BlockSpec((1, 1, LANES), lambda d, s: (d, 0, 0)),
                  pl.BlockSpec((1, LANES, 1), lambda d, s: (d, 0, 0)),
                  pl.BlockSpec((CHUNK, ML_DH), lambda d, s: (cm(d, s), 0)),
                  pl.BlockSpec((CHUNK, ML_DH), lambda d, s: (cm(d, s), 0))],
        out_specs=pl.BlockSpec((1, CHUNK, ML_W), lambda d, s: (d, cm(d, s), 0)),
        out_shape=jax.ShapeDtypeStruct((2, t_all, ML_W), F32),
        scratch_shapes=[pltpu.VMEM((ML_H, ML_DH, ML_DH), F32),
                        pltpu.VMEM((ML_H, 1, ML_DH), F32),
                        pltpu.VMEM((ML_H, 1, 1), F32)],
        compiler_params=_cparams(("arbitrary", "arbitrary")),
        name="mlstm_scan",
    )(rest, rest, rest, gates, gates_t, gbias, gbias_t, cos_tab, sin_tab)


CONV_HALO = 8


def _conv_kernel(x_ref, p_ref, n_ref, w_ref, b_ref, o_ref, *, n_chunks):
    s = pl.program_id(0)
    x = x_ref[...]
    row = lax.broadcasted_iota(jnp.int32, x.shape, 0)
    prev = jnp.where(s >= 2, p_ref[...], 0.0)
    nxt = jnp.where((s >= 1) & (s <= n_chunks - 2), n_ref[...], 0.0)
    xm1 = jnp.where(row == 0, prev[CONV_HALO - 1:CONV_HALO, :], pltpu.roll(x, 1, 0))
    xm2 = jnp.where(row == 0, prev[CONV_HALO - 2:CONV_HALO - 1, :],
                    jnp.where(row == 1, prev[CONV_HALO - 1:CONV_HALO, :], pltpu.roll(x, 2, 0)))
    xp1 = jnp.where(row == CHUNK - 1, nxt[0:1, :], pltpu.roll(x, CHUNK - 1, 0))
    w = w_ref[0]
    y = w[0:1, :] * xm2 + w[1:2, :] * xm1 + w[2:3, :] * x + w[3:4, :] * xp1 + b_ref[0]
    o_ref[...] = _silu(y)


def ssd_conv(rest, conv_w, conv_b, layer):
    t_all = rest.shape[0]
    n_chunks = t_all // CHUNK
    hb = CHUNK // CONV_HALO
    return pl.pallas_call(
        functools.partial(_conv_kernel, n_chunks=n_chunks),
        grid=(n_chunks,),
        in_specs=[pl.BlockSpec((CHUNK, CONV_CH), lambda s: (s, 0)),
                  pl.BlockSpec((CONV_HALO, CONV_CH), lambda s: (jnp.maximum(s * hb - 1, 0), 0)),
                  pl.BlockSpec((CONV_HALO, CONV_CH), lambda s: (jnp.minimum((s + 1) * hb, n_chunks * hb - 1), 0)),
                  pl.BlockSpec((1, 4, CONV_CH), lambda s: (layer, 0, 0)),
                  pl.BlockSpec((1, 1, CONV_CH), lambda s: (layer, 0, 0))],
        out_specs=pl.BlockSpec((CHUNK, CONV_CH), lambda s: (s, 0)),
        out_shape=jax.ShapeDtypeStruct((t_all, CONV_CH), F32),
        compiler_params=_cparams(("parallel",)),
        name="ssd_conv",
    )(rest, rest, rest, conv_w, conv_b)


def _ssd_kernel(x_ref, g_ref, gt_ref, gb_ref, gbt_ref, al_ref, alt_ref, o_ref, s_scr):
    d = pl.program_id(0)
    s = pl.program_id(1)

    @pl.when(s == 0)
    def _():
        s_scr[...] = jnp.zeros_like(s_scr)

    mask, mf = _scan_masks(d)
    g, gt = _direction_gates(d, g_ref, gt_ref)
    dt = _softplus(g + gb_ref[0])
    dtt = _softplus(gt + gbt_ref[0])
    inc = dt * (-jnp.exp(al_ref[0]))
    inct = dtt * (-jnp.exp(alt_ref[0]))
    cum = jnp.dot(mf, inc, preferred_element_type=F32, precision=HIGHEST)
    cumt = lax.dot_general(inct, mf, (((1,), (1,)), ((), ())), preferred_element_type=F32, precision=HIGHEST)
    tot = jnp.sum(inc, axis=0, keepdims=True)

    hpg = SSM_H // SSM_G
    for gi in range(SSM_G):
        bm = x_ref[:, SSM_W + gi * SSM_N:SSM_W + (gi + 1) * SSM_N].astype(BF16)
        cm = x_ref[:, SSM_W + (SSM_G + gi) * SSM_N:SSM_W + (SSM_G + gi + 1) * SSM_N].astype(BF16)
        gmat = _dot_nt(cm, bm)
        for hh in range(hpg):
            h = gi * hpg + hh
            la_col = cum[:, G_DT + h:G_DT + h + 1]
            la_row = cumt[G_DT + h:G_DT + h + 1, :]
            la_last = tot[:, G_DT + h:G_DT + h + 1]
            dt_col = dt[:, G_DT + h:G_DT + h + 1]
            state = s_scr[h]
            decay = jnp.exp(jnp.where(mask, la_col - la_row, -jnp.inf))
            xdt = x_ref[:, h * SSM_P:(h + 1) * SSM_P] * dt_col
            y = (_dot((gmat * decay).astype(BF16), xdt.astype(BF16))
                 + _dot_nt(cm, state.astype(BF16)) * jnp.exp(la_col))
            o_ref[0, :, h * SSM_P:(h + 1) * SSM_P] = y
            tail = jnp.exp(la_last - la_col)
            s_scr[h] = state * jnp.exp(la_last) + _dot_tn((xdt * tail).astype(BF16), bm)


def ssd_scan(xbc_act, gates, gates_t, gbias, gbias_t, alog, alog_t):
    t_all = xbc_act.shape[0]
    n_chunks = t_all // CHUNK
    cm = lambda d, s: _scan_chunk(d, s, n_chunks)
    return pl.pallas_call(
        _ssd_kernel,
        grid=(2, n_chunks),
        in_specs=[pl.BlockSpec((CHUNK, CONV_CH), lambda d, s: (cm(d, s), 0)),
                  pl.BlockSpec((CHUNK, LANES), lambda d, s: (cm(d, s), 0)),
                  pl.BlockSpec((LANES, CHUNK), lambda d, s: (0, cm(d, s))),
                  pl.BlockSpec((1, 1, LANES), lambda d, s: (d, 0, 0)),
                  pl.BlockSpec((1, LANES, 1), lambda d, s: (d, 0, 0)),
                  pl.BlockSpec((1, 1, LANES), lambda d, s: (d, 0, 0)),
                  pl.BlockSpec((1, LANES, 1), lambda d, s: (d, 0, 0))],
        out_specs=pl.BlockSpec((1, CHUNK, SSM_W), lambda d, s: (d, cm(d, s), 0)),
        out_shape=jax.ShapeDtypeStruct((2, t_all, SSM_W), F32),
        scratch_shapes=[pltpu.VMEM((SSM_H, SSM_P, SSM_N), F32)],
        compiler_params=_cparams(("arbitrary", "arbitrary")),
        name="ssd_scan",
    )(xbc_act, gates, gates_t, gbias, gbias_t, alog, alog_t)


OUT_TM = 640
OUT_TN = 512
OUT_SUB = 128


def _out_proj_kernel(ona_ref, hs_ref, mo_ref, ys_ref, xs_ref, z_ref, mlw_ref, dsk_ref, ssw_ref,
                     w_ref, x_ref, g_ref, o_ref, a_scr):
    i = pl.program_id(0)
    j = pl.program_id(1)

    @pl.when(j == 0)
    def _():
        def body(r, carry):
            rows = pl.ds(pl.multiple_of(r * OUT_SUB, OUT_SUB), OUT_SUB)
            a_scr[rows, 0:NA_W] = ona_ref[rows, :]
            hsum = hs_ref[0, rows, :] + hs_ref[1, rows, :]
            gate = 1.0 / (1.0 + jnp.exp(-mo_ref[rows, :]))
            for h in range(ML_H):
                cs = slice(h * ML_DH, (h + 1) * ML_DH)
                hh = hsum[:, cs]
                mu = jnp.mean(hh, axis=-1, keepdims=True)
                var = jnp.mean(jnp.square(hh - mu), axis=-1, keepdims=True)
                hn = (hh - mu) * lax.rsqrt(var + EPS) * mlw_ref[0, :, cs]
                a_scr[rows, NA_W + h * ML_DH:NA_W + (h + 1) * ML_DH] = (gate[:, cs] * hn).astype(BF16)
            y = ys_ref[0, rows, :] + ys_ref[1, rows, :] + dsk_ref[0] * xs_ref[rows, :]
            y = y * _silu(z_ref[rows, :])
            gw = SSM_W // SSM_G
            for gi in range(SSM_G):
                cs = slice(gi * gw, (gi + 1) * gw)
                yg = y[:, cs]
                yn = yg * lax.rsqrt(jnp.mean(yg * yg, axis=-1, keepdims=True) + EPS) * ssw_ref[0, :, cs]
                a_scr[rows, NA_W + ML_W + gi * gw:NA_W + ML_W + (gi + 1) * gw] = yn.astype(BF16)
            return carry
        lax.fori_loop(0, OUT_TM // OUT_SUB, body, 0)

    acc = _dot(a_scr[...], w_ref[0])
    is_ctx = (i * OUT_TM + lax.broadcasted_iota(jnp.int32, (OUT_TM, 1), 0)) < CTX
    gate1 = jnp.where(is_ctx, g_ref[1:2, :], g_ref[0:1, :])
    o_ref[...] = x_ref[...] + gate1 * acc


def out_projection(o_na, hs, rest, ys, xbc_act, ml_norm_w, d_skip_vec, ssm_norm_w, w_out, xa, g1, layer):
    t_all, d = xa.shape
    cw = lambda off: off // ML_W
    vec = pl.BlockSpec((1, 1, ML_W), lambda i, j: (layer, 0, 0))
    return pl.pallas_call(
        _out_proj_kernel,
        grid=(t_all // OUT_TM, d // OUT_TN),
        in_specs=[pl.BlockSpec((OUT_TM, NA_W), lambda i, j: (i, 0)),
                  pl.BlockSpec((2, OUT_TM, ML_W), lambda i, j: (0, i, 0)),
                  pl.BlockSpec((OUT_TM, ML_W), lambda i, j: (i, cw(R_MO))),
                  pl.BlockSpec((2, OUT_TM, SSM_W), lambda i, j: (0, i, 0)),
                  pl.BlockSpec((OUT_TM, SSM_W), lambda i, j: (i, 0)),
                  pl.BlockSpec((OUT_TM, SSM_W), lambda i, j: (i, cw(R_Z))),
                  vec, vec, vec,
                  pl.BlockSpec((1, d, OUT_TN), lambda i, j: (layer, 0, j)),
                  pl.BlockSpec((OUT_TM, OUT_TN), lambda i, j: (i, j)),
                  pl.BlockSpec((2, OUT_TN), lambda i, j: (0, j))],
        out_specs=pl.BlockSpec((OUT_TM, OUT_TN), lambda i, j: (i, j)),
        out_shape=jax.ShapeDtypeStruct((t_all, d), F32),
        scratch_shapes=[pltpu.VMEM((OUT_TM, d), BF16)],
        compiler_params=_cparams(("arbitrary", "arbitrary")),
        name="out_projection",
    )(o_na, hs, rest, ys, xbc_act, rest, ml_norm_w, d_skip_vec, ssm_norm_w, w_out, xa, g1)


RT_TM = 640
RT_SUB = 128


def _router_kernel(x_ref, nw_ref, sc_ref, sh_ref, wr_ref, br_ref, h_ref, r_ref, cnt_ref, run_scr):
    i = pl.program_id(0)

    @pl.when(i == 0)
    def _():
        run_scr[...] = jnp.zeros_like(run_scr)

    t_row = lax.broadcasted_iota(jnp.int32, (RT_SUB, RT_SUB), 0)
    t_col = lax.broadcasted_iota(jnp.int32, (RT_SUB, RT_SUB), 1)
    earlier = (t_col < t_row).astype(BF16)

    def body(r, run):
        rows = pl.ds(pl.multiple_of(r * RT_SUB, RT_SUB), RT_SUB)
        h = _modulated_norm(x_ref[rows, :], nw_ref[...], sc_ref[...], sh_ref[...], i * RT_TM + r * RT_SUB)
        h_ref[rows, :] = h
        logit = jnp.dot(h, wr_ref[0], preferred_element_type=F32, precision=HIGHEST) + br_ref[0]
        lane = lax.broadcasted_iota(jnp.int32, logit.shape, 1)
        big = jnp.int32(LANES)
        is_g = lane < MOE_GROUPS
        lg = jnp.where(is_g, logit, -jnp.inf)
        gmax = jnp.max(lg, axis=-1, keepdims=True)
        g_sel = jnp.min(jnp.where(is_g & (lg == gmax), lane, big), axis=-1, keepdims=True)
        g_w = 1.0 / jnp.sum(jnp.exp(lg - gmax), axis=-1, keepdims=True)
        lo = MOE_GROUPS + g_sel * MOE_PER_GROUP
        in_g = (lane >= lo) & (lane < lo + MOE_PER_GROUP)
        le = jnp.where(in_g, logit, -jnp.inf)
        v1 = jnp.max(le, axis=-1, keepdims=True)
        i1 = jnp.min(jnp.where(in_g & (le == v1), lane, big), axis=-1, keepdims=True)
        le2 = jnp.where(lane == i1, -jnp.inf, le)
        v2 = jnp.max(le2, axis=-1, keepdims=True)
        i2 = jnp.min(jnp.where(in_g & (lane != i1) & (le2 == v2), lane, big), axis=-1, keepdims=True)
        e2 = jnp.exp(v2 - v1)
        w1 = g_w / (1.0 + e2)
        w2 = g_w * e2 / (1.0 + e2)
        oh1 = (lane == i1 - MOE_GROUPS).astype(F32)
        oh2 = (lane == i2 - MOE_GROUPS).astype(F32)
        oh = oh1 + oh2
        before = _dot(earlier, oh.astype(BF16)) + run
        rank1 = jnp.sum(before * oh1, axis=-1, keepdims=True)
        rank2 = jnp.sum(before * oh2, axis=-1, keepdims=True)
        out = jnp.where(lane == 0, (i1 - MOE_GROUPS).astype(F32),
                        jnp.where(lane == 1, (i2 - MOE_GROUPS).astype(F32),
                                  jnp.where(lane == 2, w1,
                                            jnp.where(lane == 3, w2,
                                                      jnp.where(lane == 4, rank1,
                                                                jnp.where(lane == 5, rank2, 0.0))))))
        r_ref[rows, :] = out
        return run + jnp.sum(oh, axis=0, keepdims=True)
    run = lax.fori_loop(0, RT_TM // RT_SUB, body, run_scr[...])
    run_scr[...] = run
    cnt_ref[...] = jnp.broadcast_to(run, cnt_ref.shape)


def moe_router(xa, nw, sc2, sh2, w_route, b_route, layer):
    t_all, d = xa.shape
    return pl.pallas_call(
        _router_kernel,
        grid=(t_all // RT_TM,),
        in_specs=[pl.BlockSpec((RT_TM, d), lambda i: (i, 0)),
                  pl.BlockSpec((1, d), lambda i: (0, 0)),
                  pl.BlockSpec((2, d), lambda i: (0, 0)),
                  pl.BlockSpec((2, d), lambda i: (0, 0)),
                  pl.BlockSpec((1, d, LANES), lambda i: (layer, 0, 0)),
                  pl.BlockSpec((1, 1, LANES), lambda i: (layer, 0, 0))],
        out_specs=[pl.BlockSpec((RT_TM, d), lambda i: (i, 0)),
                   pl.BlockSpec((RT_TM, LANES), lambda i: (i, 0)),
                   pl.BlockSpec((8, LANES), lambda i: (0, 0))],
        out_shape=[jax.ShapeDtypeStruct((t_all, d), F32),
                   jax.ShapeDtypeStruct((t_all, LANES), F32),
                   jax.ShapeDtypeStruct((8, LANES), F32)],
        scratch_shapes=[pltpu.VMEM((1, LANES), F32)],
        compiler_params=_cparams(("arbitrary",)),
        name="moe_router",
    )(xa, nw, sc2, sh2, w_route, b_route)


def _expert_kernel(be_ref, nu_ref, st_ref, h_hbm, w1_ref, w3_ref, w2_ref, o_ref, xbuf, sem):
    b = pl.program_id(0)
    n_used = nu_ref[0]

    def start_gather(blk, slot):
        base = blk * MOE_BM

        def body(t, carry):
            tok = st_ref[base + t]
            pltpu.make_async_copy(h_hbm.at[pl.ds(tok, 1), :], xbuf.at[slot, pl.ds(t, 1), :], sem.at[slot]).start()
            return carry
        lax.fori_loop(0, MOE_BM, body, 0, unroll=8)

    @pl.when(b == 0)
    def _():
        start_gather(0, 0)

    @pl.when(b + 1 < n_used)
    def _():
        start_gather(b + 1, (b + 1) % 2)

    @pl.when(b < n_used)
    def _():
        slot = b % 2
        pltpu.make_async_copy(h_hbm.at[pl.ds(0, MOE_BM), :], xbuf.at[slot], sem.at[slot]).wait()
        x = xbuf[slot].astype(BF16)
        hid = _silu(_dot(x, w1_ref[0, 0])) * _dot(x, w3_ref[0, 0])
        o_ref[...] = _dot(hid.astype(BF16), w2_ref[0, 0])

    @pl.when(b >= n_used)
    def _():
        o_ref[...] = jnp.zeros_like(o_ref)


def expert_blocks(block_e, n_used, slot_tok, h2, w1, w3, w2, layer):
    d = h2.shape[1]
    ff = w1.shape[-1]
    n_slots = slot_tok.shape[0]
    n_blocks = n_slots // MOE_BM
    return pl.pallas_call(
        _expert_kernel,
        grid_spec=pltpu.PrefetchScalarGridSpec(
            num_scalar_prefetch=3,
            grid=(n_blocks,),
            in_specs=[pl.BlockSpec(memory_space=pl.ANY),
                      pl.BlockSpec((1, 1, d, ff), lambda b, be, nu, st: (layer, be[b], 0, 0)),
                      pl.BlockSpec((1, 1, d, ff), lambda b, be, nu, st: (layer, be[b], 0, 0)),
                      pl.BlockSpec((1, 1, ff, d), lambda b, be, nu, st: (layer, be[b], 0, 0))],
            out_specs=pl.BlockSpec((MOE_BM, d), lambda b, be, nu, st: (b, 0)),
            scratch_shapes=[pltpu.VMEM((2, MOE_BM, d), F32),
                            pltpu.SemaphoreType.DMA((2,))]),
        out_shape=jax.ShapeDtypeStruct((n_slots, d), F32),
        compiler_params=_cparams(("arbitrary",)),
        name="moe_experts",
    )(block_e, n_used, slot_tok, h2, w1, w3, w2)


def _combine_kernel(dest_ref, x_ref, r_ref, g_ref, fw_ref, y_hbm, o_ref, ybuf, sem, *,
                    row_block0, n_tiles, final_norm):
    i = pl.program_id(0)

    def start_gather(tile, slot):
        base = (tile + row_block0) * (2 * CHUNK)

        def body(t, carry):
            for k in range(2):
                row = dest_ref[base + 2 * t + k]
                pltpu.make_async_copy(y_hbm.at[pl.ds(row, 1), :],
                                      ybuf.at[slot, pl.ds(k * CHUNK + t, 1), :], sem.at[slot]).start()
            return carry
        lax.fori_loop(0, CHUNK, body, 0, unroll=4)

    @pl.when(i == 0)
    def _():
        start_gather(0, 0)

    @pl.when(i + 1 < n_tiles)
    def _():
        start_gather(i + 1, (i + 1) % 2)

    slot = i % 2
    pltpu.make_async_copy(y_hbm.at[pl.ds(0, 2 * CHUNK), :], ybuf.at[slot], sem.at[slot]).wait()

    rt = r_ref[...]
    y = rt[:, 2:3] * ybuf[slot, 0:CHUNK, :] + rt[:, 3:4] * ybuf[slot, CHUNK:2 * CHUNK, :]
    is_ctx = ((i + row_block0) * CHUNK + lax.broadcasted_iota(jnp.int32, (CHUNK, 1), 0)) < CTX
    gate2 = jnp.where(is_ctx, g_ref[1:2, :], g_ref[0:1, :])
    x = x_ref[...] + gate2 * y
    if final_norm:
        x = x * lax.rsqrt(jnp.mean(x * x, axis=-1, keepdims=True) + EPS) * fw_ref[...]
    o_ref[...] = x


def moe_combine(dest, xa, route, g2, final_w, y, final_norm):
    t_all, d = xa.shape
    rb0 = CTX // CHUNK if final_norm else 0
    n_out = t_all // CHUNK - rb0
    return pl.pallas_call(
        functools.partial(_combine_kernel, row_block0=rb0, n_tiles=n_out, final_norm=final_norm),
        grid_spec=pltpu.PrefetchScalarGridSpec(
            num_scalar_prefetch=1,
            grid=(n_out,),
            in_specs=[pl.BlockSpec((CHUNK, d), lambda i, ds: (i + rb0, 0)),
                      pl.BlockSpec((CHUNK, LANES), lambda i, ds: (i + rb0, 0)),
                      pl.BlockSpec((2, d), lambda i, ds: (0, 0)),
                      pl.BlockSpec((1, d), lambda i, ds: (0, 0)),
                      pl.BlockSpec(memory_space=pl.ANY)],
            out_specs=pl.BlockSpec((CHUNK, d), lambda i, ds: (i, 0)),
            scratch_shapes=[pltpu.VMEM((2, 2 * CHUNK, d), F32),
                            pltpu.SemaphoreType.DMA((2,))]),
        out_shape=jax.ShapeDtypeStruct((n_out * CHUNK, d), F32),
        compiler_params=_cparams(("arbitrary",)),
        name="moe_combine",
    )(dest, xa, route, g2, final_w, y)


def moe_dispatch(route, counts):
    n = route.shape[0]
    n_assign = 2 * n
    e_flat = route[:, 0:2].astype(jnp.int32).reshape(-1)
    rank_flat = route[:, 4:6].astype(jnp.int32).reshape(-1)
    padded = (counts + MOE_BM - 1) // MOE_BM * MOE_BM
    pends = jnp.cumsum(padded)
    pstarts = pends - padded
    experts = jnp.arange(N_EXPERTS, dtype=jnp.int32)
    dest = jnp.sum(jnp.where(e_flat[:, None] == experts[None, :], pstarts[None, :], 0), axis=1) + rank_flat
    n_blocks = -(-n_assign // MOE_BM) + N_EXPERTS
    n_slots = n_blocks * MOE_BM
    tok_flat = jnp.arange(n_assign, dtype=jnp.int32) // 2
    slot_tok = jnp.zeros((n_slots,), jnp.int32).at[dest].set(tok_flat)
    block_start = jnp.arange(n_blocks, dtype=jnp.int32) * MOE_BM
    block_e = jnp.minimum(jnp.sum((pends[None, :] <= block_start[:, None]).astype(jnp.int32), axis=1),
                          N_EXPERTS - 1)
    n_used = (pends[-1] // MOE_BM).astype(jnp.int32).reshape(1)
    return dest, slot_tok, block_e, n_used


def hier_moe(xa, nw, sc2, sh2, g2, w_route, b_route, w1, w3, w2, final_w, layer, final_norm):
    h2, route, cnt = moe_router(xa, nw, sc2, sh2, w_route, b_route, layer)
    dest, slot_tok, block_e, n_used = moe_dispatch(route, cnt[0, :N_EXPERTS].astype(jnp.int32))
    y = expert_blocks(block_e, n_used, slot_tok, h2, w1, w3, w2, layer)
    return moe_combine(dest, xa, route, g2, final_w, y, final_norm)


def rope_tables(t_lat):
    nf = ML_DH // 4
    inv = ROPE_THETA ** (-jnp.arange(nf, dtype=F32) / nf)
    t_idx = jnp.arange(t_lat)
    ang_r = (t_idx // GRID_W).astype(F32)[:, None] * inv
    ang_c = (t_idx % GRID_W).astype(F32)[:, None] * inv
    cos = jnp.concatenate([jnp.cos(ang_r)] * 2 + [jnp.cos(ang_c)] * 2, axis=-1)
    sin = jnp.concatenate([-jnp.sin(ang_r), jnp.sin(ang_r), -jnp.sin(ang_c), jnp.sin(ang_c)], axis=-1)
    cos = jnp.concatenate([jnp.ones((CTX, ML_DH), F32), cos], axis=0)
    sin = jnp.concatenate([jnp.zeros((CTX, ML_DH), F32), sin], axis=0)
    return cos, sin


def _pad_lanes(v):
    return jnp.pad(v, [(0, 0)] * (v.ndim - 1) + [(0, LANES - v.shape[-1])])


def kernel(x, c, ctx, c_ctx, w_mod, b_mod, norm1_w, norm2_w, w_in, w_out, na_rpb, ml_gate_b, ml_norm_w, ssm_conv_w, ssm_conv_b, ssm_dt_bias, ssm_a_log, ssm_d, ssm_norm_w, router_g_w, router_g_b, router_e_w, router_e_b, moe_w1, moe_w3, moe_w2, final_norm_w):
    depth = w_mod.shape[0]
    t_lat = x.shape[1]
    d = D_MODEL
    xa = jnp.concatenate([ctx[0], x[0]], axis=0)

    cond = jnp.zeros((8, d), F32).at[0].set(c[0]).at[1].set(c_ctx)
    mod = adaln_modulation(cond, w_mod, b_mod)[:, 0:2, :].reshape(depth, 2, 6, d)
    cos_tab, sin_tab = rope_tables(t_lat)

    o_mq = 3 * NA_W
    o_mg = o_mq + 4 * ML_W
    o_z = o_mg + 4 * ML_H
    o_xbc = o_z + SSM_W
    o_dt = o_xbc + CONV_CH
    w_main = jnp.concatenate([w_in[:, :, 0:o_mq], w_in[:, :, o_xbc:o_dt], w_in[:, :, o_mq:o_mg],
                              w_in[:, :, o_z:o_xbc]], axis=2).astype(BF16)
    w_mg, w_dt = w_in[:, :, o_mg:o_z], w_in[:, :, o_dt:]
    w_gate = _pad_lanes(jnp.concatenate([w_mg[:, :, 0:8], w_dt[:, :, 0:8], w_mg[:, :, 8:16], w_dt[:, :, 8:16]],
                                        axis=2))
    w_out_b = w_out.astype(BF16)
    w1_b, w3_b, w2_b = moe_w1.astype(BF16), moe_w3.astype(BF16), moe_w2.astype(BF16)
    w_route = _pad_lanes(jnp.concatenate([router_g_w, router_e_w], axis=2))
    b_route = _pad_lanes(jnp.concatenate([router_g_b, router_e_b], axis=1))[:, None, :]
    gb4 = ml_gate_b.reshape(depth, 2, 2 * ML_H)
    gbias = _pad_lanes(jnp.concatenate([gb4, ssm_dt_bias], axis=2))
    alog = _pad_lanes(jnp.concatenate([jnp.zeros_like(ssm_a_log), ssm_a_log], axis=2))
    d_skip_vec = jnp.repeat(ssm_d, SSM_P, axis=1)[:, None, :]
    conv_b = ssm_conv_b[:, None, :]

    out = None
    for l in range(depth):
        last = l == depth - 1
        sh1, sc1, g1, sh2, sc2, g2 = (mod[l, :, p, :] for p in range(6))

        qkv, rest, gates, gates_t = in_projection(xa, norm1_w[l][None], sc1, sh1, w_main, w_gate, l)
        o_na = neighbourhood_attention(qkv, na_bias_table(na_rpb[l]))
        hs = mlstm_scan(rest, gates, gates_t, gbias[l][:, None, :], gbias[l][:, :, None], cos_tab, sin_tab)
        xbc_act = ssd_conv(rest, ssm_conv_w, conv_b, l)
        ys = ssd_scan(xbc_act, gates, gates_t, gbias[l][:, None, :], gbias[l][:, :, None],
                      alog[l][:, None, :], alog[l][:, :, None])
        xa = out_projection(o_na, hs, rest, ys, xbc_act, ml_norm_w[:, None, :], d_skip_vec,
                            ssm_norm_w[:, None, :], w_out_b, xa, g1, l)
        res = hier_moe(xa, norm2_w[l][None], sc2, sh2, g2, w_route, b_route, w1_b, w3_b, w2_b,
                       final_norm_w[None], l, last)
        if last:
            out = res
        else:
            xa = res
    return out[None]
```

```python
import functools

import jax
import jax.numpy as jnp
import numpy as np
from jax import lax
from jax.experimental import pallas as pl
from jax.experimental.pallas import tpu as pltpu

F32 = jnp.float32
BF16 = jnp.bfloat16
HIGHEST = lax.Precision.HIGHEST

D_MODEL = 2048
GRID_W = 64
CTX = 256
NA_DH = 128
NA_W = 1024
NA_H = 8
NA_KR = 8
NA_KC = 16
ML_DH = 128
ML_W = 512
ML_H = 4
SSM_P = 64
SSM_W = 512
SSM_H = 8
SSM_G = 2
SSM_N = 128
CONV_CH = SSM_W + 2 * SSM_G * SSM_N
MOE_GROUPS = 4
MOE_PER_GROUP = 8
N_EXPERTS = 32
MOE_FF = 1024
ROPE_THETA = 10000.0
EPS = 1e-6

CHUNK = 256
LANES = 128
NEG = -1e30
MOE_BM = 256
VMEM_LIMIT = 56 * 1024 * 1024

R_XBC, R_MQ, R_MK, R_MV, R_MO, R_Z = 0, 1024, 1536, 2048, 2560, 3072
REST_W = 3584
QKV_W = 3 * NA_W
G_I, G_F, G_DT = 0, 4, 8
G_DIR = 16


def _cparams(sem):
    return pltpu.CompilerParams(dimension_semantics=sem, vmem_limit_bytes=VMEM_LIMIT)


def _silu(x):
    return x / (1.0 + jnp.exp(-x))


def _softplus(x):
    return jnp.maximum(x, 0.0) + jnp.log1p(jnp.exp(-jnp.abs(x)))


def _dot(a, b):
    return jnp.dot(a, b, preferred_element_type=F32)


def _dot_nt(a, b):
    return lax.dot_general(a, b, (((1,), (1,)), ((), ())), preferred_element_type=F32)


def _dot_tn(a, b):
    return lax.dot_general(a, b, (((0,), (0,)), ((), ())), preferred_element_type=F32)


def _split2(x):
    hi = x.astype(BF16)
    return hi, (x - hi.astype(F32)).astype(BF16)


def _dot_split(a, w_hi, w_lo):
    a_hi, a_lo = _split2(a)
    return _dot(a_hi, w_hi) + _dot(a_hi, w_lo) + _dot(a_lo, w_hi)


def _mod_kernel(c_ref, w_ref, b_ref, o_ref):
    o_ref[0] = jnp.dot(_silu(c_ref[...]), w_ref[0], preferred_element_type=F32, precision=HIGHEST) + b_ref[0]


def adaln_modulation(cond, w_mod, b_mod):
    depth, d, n = w_mod.shape
    tn = 512
    return pl.pallas_call(
        _mod_kernel,
        grid=(depth, n // tn),
        in_specs=[pl.BlockSpec((8, d), lambda l, j: (0, 0)),
                  pl.BlockSpec((1, d, tn), lambda l, j: (l, 0, j)),
                  pl.BlockSpec((1, 1, tn), lambda l, j: (l, 0, j))],
        out_specs=pl.BlockSpec((1, 8, tn), lambda l, j: (l, 0, j)),
        out_shape=jax.ShapeDtypeStruct((depth, 8, n), F32),
        compiler_params=_cparams(("parallel", "parallel")),
        name="adaln_modulation",
    )(cond, w_mod, b_mod.reshape(depth, 1, n))


def _modulated_norm(x, nw, sc2, sh2, row0):
    r = x.shape[0]
    y = x * lax.rsqrt(jnp.mean(x * x, axis=-1, keepdims=True) + EPS) * nw
    is_ctx = (row0 + lax.broadcasted_iota(jnp.int32, (r, 1), 0)) < CTX
    sc = jnp.where(is_ctx, sc2[1:2, :], sc2[0:1, :])
    sh = jnp.where(is_ctx, sh2[1:2, :], sh2[0:1, :])
    return y * (1.0 + sc) + sh


IN_TM = 1280
IN_TN = 512
IN_SUB = 256


def _in_proj_kernel(x_ref, nw_ref, sc_ref, sh_ref, wm_ref, wg_ref, qkv_ref, rest_ref, gate_ref, gate_t_ref, a_scr):
    i = pl.program_id(0)
    j = pl.program_id(1)
    n_qkv = QKV_W // IN_TN

    @pl.when(j == 0)
    def _():
        wg_hi, wg_lo = _split2(wg_ref[0])
        for r in range(IN_TM // IN_SUB):
            rows = slice(r * IN_SUB, (r + 1) * IN_SUB)
            h = _modulated_norm(x_ref[rows, :], nw_ref[...], sc_ref[...], sh_ref[...], i * IN_TM + r * IN_SUB)
            a_scr[rows, :] = h.astype(BF16)
            gate = _dot_split(h, wg_hi, wg_lo)
            gate_ref[rows, :] = gate
            gate_t_ref[:, rows] = gate.T

    acc = _dot(a_scr[...], wm_ref[0])

    @pl.when(j < n_qkv)
    def _():
        qkv_ref[...] = acc.astype(BF16)

    @pl.when(j >= n_qkv)
    def _():
        rest_ref[...] = acc


def in_projection(xa, nw, sc2, sh2, w_main, w_gate, layer):
    t_all, d = xa.shape
    n_main = w_main.shape[2]
    n_qkv = QKV_W // IN_TN
    return pl.pallas_call(
        _in_proj_kernel,
        grid=(t_all // IN_TM, n_main // IN_TN),
        in_specs=[pl.BlockSpec((IN_TM, d), lambda i, j: (i, 0)),
                  pl.BlockSpec((1, d), lambda i, j: (0, 0)),
                  pl.BlockSpec((2, d), lambda i, j: (0, 0)),
                  pl.BlockSpec((2, d), lambda i, j: (0, 0)),
                  pl.BlockSpec((1, d, IN_TN), lambda i, j: (layer, 0, j)),
                  pl.BlockSpec((1, d, LANES), lambda i, j: (layer, 0, 0))],
        out_specs=[pl.BlockSpec((IN_TM, IN_TN), lambda i, j: (i, jnp.minimum(j, n_qkv - 1))),
                   pl.BlockSpec((IN_TM, IN_TN), lambda i, j: (i, jnp.maximum(j - n_qkv, 0))),
                   pl.BlockSpec((IN_TM, LANES), lambda i, j: (i, 0)),
                   pl.BlockSpec((LANES, IN_TM), lambda i, j: (0, i))],
        out_shape=[jax.ShapeDtypeStruct((t_all, QKV_W), BF16),
                   jax.ShapeDtypeStruct((t_all, REST_W), F32),
                   jax.ShapeDtypeStruct((t_all, LANES), F32),
                   jax.ShapeDtypeStruct((LANES, t_all), F32)],
        scratch_shapes=[pltpu.VMEM((IN_TM, d), BF16)],
        compiler_params=_cparams(("arbitrary", "arbitrary")),
        name="in_projection",
    )(xa, nw, sc2, sh2, w_main, w_gate)


NA_RB = CHUNK // GRID_W
NA_UR = 12


NA_HB = 2


def _na_kernel(q_ref, k_ref, v_ref, bias_ref, o_ref, *, n_rows):
    rb = pl.program_id(1)
    scale = NA_DH ** -0.5

    @pl.when(rb == 0)
    def _():
        for hh in range(NA_HB):
            hs = slice(hh * NA_DH, (hh + 1) * NA_DH)
            s = _dot_nt(q_ref[:, hs], k_ref[0:CTX, hs]) * scale
            m = jnp.max(s, axis=-1, keepdims=True)
            p = jnp.exp(s - m)
            l = jnp.sum(p, axis=-1, keepdims=True)
            o_ref[:, hs] = (_dot(p.astype(BF16), v_ref[0:CTX, hs]) / l).astype(o_ref.dtype)

    @pl.when(rb > 0)
    def _():
        r0 = (rb - 1) * NA_RB
        u0 = jnp.clip(r0 - NA_KR // 2, 0, n_rows - NA_UR)
        case = jnp.where(r0 == 0, 1, jnp.where(r0 == n_rows - NA_RB, 2, 0))
        start = pl.multiple_of(CTX + u0 * GRID_W, GRID_W)
        for hh in range(NA_HB):
            hs = slice(hh * NA_DH, (hh + 1) * NA_DH)
            q = q_ref[:, hs]
            kc = k_ref[0:CTX, hs]
            vc = v_ref[0:CTX, hs]
            kw = k_ref[pl.ds(start, NA_UR * GRID_W), hs]
            vw = v_ref[pl.ds(start, NA_UR * GRID_W), hs]
            s = _dot_nt(q, kw) * scale + bias_ref[hh, case]
            sc = _dot_nt(q, kc) * scale
            m = jnp.maximum(jnp.max(s, axis=-1, keepdims=True), jnp.max(sc, axis=-1, keepdims=True))
            p = jnp.exp(s - m)
            pc = jnp.exp(sc - m)
            l = jnp.sum(p, axis=-1, keepdims=True) + jnp.sum(pc, axis=-1, keepdims=True)
            o = _dot(p.astype(BF16), vw) + _dot(pc.astype(BF16), vc)
            o_ref[:, hs] = (o / l).astype(o_ref.dtype)


def na_bias_table(rpb):
    h = rpb.shape[0]
    c = np.arange(GRID_W)
    cs = np.clip(c - NA_KC // 2, 0, GRID_W - NA_KC)
    kcol = np.arange(GRID_W)
    inside = (kcol[None, :] >= cs[:, None]) & (kcol[None, :] < cs[:, None] + NA_KC)
    off = kcol[None, :] - c[:, None] + NA_KC - 1
    onehot = (off[None] == np.arange(2 * NA_KC - 1)[:, None, None]) & inside[None]
    band = jnp.einsum('hro,ock->hrck', rpb.astype(F32), jnp.asarray(onehot, F32), precision=HIGHEST)
    band = band + jnp.asarray(np.where(inside, 0.0, NEG), F32)
    neg = jnp.full((h, GRID_W, GRID_W), NEG, F32)
    cases = ([(rr, NA_KR // 2 - 1) for rr in range(NA_RB)],
             [(0, NA_KR - 1 - rr) for rr in range(NA_RB)],
             [(NA_UR - NA_KR, NA_KR // 2 - 1 - rr) for rr in range(NA_RB)])
    tabs = []
    for case in cases:
        rows = []
        for w_off, d0 in case:
            rows.append(jnp.concatenate(
                [band[:, d0 + u - w_off] if 0 <= u - w_off < NA_KR else neg for u in range(NA_UR)], axis=-1))
        tabs.append(jnp.concatenate(rows, axis=1))
    return jnp.stack(tabs, axis=1)


def neighbourhood_attention(qkv, bias_tab):
    t_all = qkv.shape[0]
    n_rows = (t_all - CTX) // GRID_W
    n_rb = t_all // CHUNK
    return pl.pallas_call(
        functools.partial(_na_kernel, n_rows=n_rows),
        grid=(NA_H // NA_HB, n_rb),
        in_specs=[pl.BlockSpec((CHUNK, NA_HB * NA_DH), lambda h, rb: (rb, h)),
                  pl.BlockSpec((t_all, NA_HB * NA_DH), lambda h, rb: (0, NA_H // NA_HB + h)),
                  pl.BlockSpec((t_all, NA_HB * NA_DH), lambda h, rb: (0, 2 * (NA_H // NA_HB) + h)),
                  pl.BlockSpec((NA_HB, 3, CHUNK, NA_UR * GRID_W), lambda h, rb: (h, 0, 0, 0))],
        out_specs=pl.BlockSpec((CHUNK, NA_HB * NA_DH), lambda h, rb: (rb, h)),
        out_shape=jax.ShapeDtypeStruct((t_all, NA_W), BF16),
        compiler_params=_cparams(("parallel", "parallel")),
        name="neighbourhood_attention",
    )(qkv, qkv, qkv, bias_tab)


def _scan_chunk(d, s, n_chunks):
    return jnp.where(d == 0, s, jnp.where(s == 0, 0, n_chunks - s))


def _scan_masks(d):
    row = lax.broadcasted_iota(jnp.int32, (CHUNK, CHUNK), 0)
    col = lax.broadcasted_iota(jnp.int32, (CHUNK, CHUNK), 1)
    mask = jnp.where(d == 0, row - col, col - row) >= 0
    return mask, mask.astype(BF16)


def _split3(x):
    hi = x.astype(BF16)
    r = x - hi.astype(F32)
    mid = r.astype(BF16)
    return hi, mid, (r - mid.astype(F32)).astype(BF16)


def _masked_cumsum(mb, x, xt):
    cum = sum(_dot(mb, p) for p in _split3(x))
    cumt = sum(_dot_nt(p, mb) for p in _split3(xt))
    return cum, cumt


def _direction_gates(d, g_ref, gt_ref):
    g = g_ref[...]
    gt = gt_ref[...]
    g = jnp.where(d == 0, g, pltpu.roll(g, LANES - G_DIR, 1))
    gt = jnp.where(d == 0, gt, pltpu.roll(gt, LANES - G_DIR, 0))
    return g, gt


def _rope(x, cos, sin_signed):
    lane = lax.broadcasted_iota(jnp.int32, x.shape, 1)
    nf = ML_DH // 4
    partner = jnp.where((lane % (2 * nf)) < nf, pltpu.roll(x, ML_DH - nf, 1), pltpu.roll(x, nf, 1))
    return x * cos + partner * sin_signed


def _mlstm_kernel(q_ref, k_ref, v_ref, g_ref, gt_ref, gb_ref, gbt_ref, cos_ref, sin_ref, o_ref,
                  c_scr, n_scr, m_scr):
    d = pl.program_id(0)
    s = pl.program_id(1)

    @pl.when(s == 0)
    def _():
        c_scr[...] = jnp.zeros_like(c_scr)
        n_scr[...] = jnp.zeros_like(n_scr)
        m_scr[...] = jnp.zeros_like(m_scr)

    mask, mf = _scan_masks(d)
    g, gt = _direction_gates(d, g_ref, gt_ref)
    g = g + gb_ref[0]
    gt = gt + gbt_ref[0]
    lf = -_softplus(-g)
    lft = -_softplus(-gt)
    cum, cumt = _masked_cumsum(mf, lf, lft)
    tot = jnp.sum(lf, axis=0, keepdims=True)
    cos = cos_ref[...]
    sin = sin_ref[...]

    for h in range(ML_H):
        hs = slice(h * ML_DH, (h + 1) * ML_DH)
        q = _rope(q_ref[:, hs], cos, sin)
        k = _rope(k_ref[:, hs], cos, sin) * (ML_DH ** -0.5)
        v = v_ref[:, hs]
        qb, kb, vb = q.astype(BF16), k.astype(BF16), v.astype(BF16)
        bt_col = cum[:, G_F + h:G_F + h + 1]
        bt_row = cumt[G_F + h:G_F + h + 1, :]
        ig_col = g[:, G_I + h:G_I + h + 1]
        ig_row = gt[G_I + h:G_I + h + 1, :]
        b_last = tot[:, G_F + h:G_F + h + 1]
        m_prev = m_scr[h]
        cmat = c_scr[h]
        nvec = n_scr[h]

        log_d = jnp.where(mask, bt_col - bt_row + ig_row, -jnp.inf)
        inter = bt_col + m_prev
        m_t = jnp.maximum(jnp.max(log_d, axis=-1, keepdims=True), inter)
        wts = _dot_nt(qb, kb) * jnp.exp(log_d - m_t)
        sc = jnp.exp(inter - m_t)
        num = _dot(wts.astype(BF16), vb) + _dot_nt(qb, cmat.astype(BF16)) * sc
        den = jnp.sum(wts, axis=-1, keepdims=True) + jnp.sum(q * nvec, axis=-1, keepdims=True) * sc
        den = jnp.maximum(jnp.abs(den), jnp.exp(-m_t))
        o_ref[0, :, hs] = num / den

        tail = b_last - bt_col + ig_col
        m_new = jnp.maximum(b_last + m_prev, jnp.max(tail, axis=0, keepdims=True))
        wgt = jnp.exp(tail - m_new)
        decay = jnp.exp(b_last + m_prev - m_new)
        c_scr[h] = decay * cmat + _dot_tn((v * wgt).astype(BF16), kb)
        n_scr[h] = decay * nvec + jnp.sum(wgt * k, axis=0, keepdims=True)
        m_scr[h] = m_new


def mlstm_scan(rest, gates, gates_t, gbias, gbias_t, cos_tab, sin_tab):
    t_all = rest.shape[0]
    n_chunks = t_all // CHUNK
    cm = lambda d, s: _scan_chunk(d, s, n_chunks)
    col = lambda off: off // ML_W
    return pl.pallas_call(
        _mlstm_kernel,
        grid=(2, n_chunks),
        in_specs=[pl.BlockSpec((CHUNK, ML_W), lambda d, s: (cm(d, s), col(R_MQ))),
                  pl.BlockSpec((CHUNK, ML_W), lambda d, s: (cm(d, s), col(R_MK))),
                  pl.BlockSpec((CHUNK, ML_W), lambda d, s: (cm(d, s), col(R_MV))),
                  pl.BlockSpec((CHUNK, LANES), lambda d, s: (cm(d, s), 0)),
                  pl.BlockSpec((LANES, CHUNK), lambda d, s: (0, cm(d, s))),
                  pl.BlockSpec((1, 1, LANES), lambda d, s: (d, 0, 0)),
                  pl.BlockSpec((1, LANES, 1), lambda d, s: (d, 0, 0)),
                  pl.BlockSpec((CHUNK, ML_DH), lambda d, s: (cm(d, s), 0)),
                  pl.BlockSpec((CHUNK, ML_DH), lambda d, s: (cm(d, s), 0))],
        out_specs=pl.BlockSpec((1, CHUNK, ML_W), lambda d, s: (d, cm(d, s), 0)),
        out_shape=jax.ShapeDtypeStruct((2, t_all, ML_W), F32),
        scratch_shapes=[pltpu.VMEM((ML_H, ML_DH, ML_DH), F32),
                        pltpu.VMEM((ML_H, 1, ML_DH), F32),
                        pltpu.VMEM((ML_H, 1, 1), F32)],
        compiler_params=_cparams(("arbitrary", "arbitrary")),
        name="mlstm_scan",
    )(rest, rest, rest, gates, gates_t, gbias, gbias_t, cos_tab, sin_tab)


CONV_HALO = 8


def _conv_kernel(x_ref, p_ref, n_ref, w_ref, b_ref, o_ref, *, n_chunks):
    s = pl.program_id(0)
    x = x_ref[...]
    row = lax.broadcasted_iota(jnp.int32, x.shape, 0)
    prev = jnp.where(s >= 2, p_ref[...], 0.0)
    nxt = jnp.where((s >= 1) & (s <= n_chunks - 2), n_ref[...], 0.0)
    xm1 = jnp.where(row == 0, prev[CONV_HALO - 1:CONV_HALO, :], pltpu.roll(x, 1, 0))
    xm2 = jnp.where(row == 0, prev[CONV_HALO - 2:CONV_HALO - 1, :],
                    jnp.where(row == 1, prev[CONV_HALO - 1:CONV_HALO, :], pltpu.roll(x, 2, 0)))
    xp1 = jnp.where(row == CHUNK - 1, nxt[0:1, :], pltpu.roll(x, CHUNK - 1, 0))
    w = w_ref[0]
    y = w[0:1, :] * xm2 + w[1:2, :] * xm1 + w[2:3, :] * x + w[3:4, :] * xp1 + b_ref[0]
    o_ref[...] = _silu(y)


def ssd_conv(rest, conv_w, conv_b, layer):
    t_all = rest.shape[0]
    n_chunks = t_all // CHUNK
    hb = CHUNK // CONV_HALO
    return pl.pallas_call(
        functools.partial(_conv_kernel, n_chunks=n_chunks),
        grid=(n_chunks,),
        in_specs=[pl.BlockSpec((CHUNK, CONV_CH), lambda s: (s, 0)),
                  pl.BlockSpec((CONV_HALO, CONV_CH), lambda s: (jnp.maximum(s * hb - 1, 0), 0)),
                  pl.BlockSpec((CONV_HALO, CONV_CH), lambda s: (jnp.minimum((s + 1) * hb, n_chunks * hb - 1), 0)),
                  pl.BlockSpec((1, 4, CONV_CH), lambda s: (layer, 0, 0)),
                  pl.BlockSpec((1, 1, CONV_CH), lambda s: (layer, 0, 0))],
        out_specs=pl.BlockSpec((CHUNK, CONV_CH), lambda s: (s, 0)),
        out_shape=jax.ShapeDtypeStruct((t_all, CONV_CH), F32),
        compiler_params=_cparams(("parallel",)),
        name="ssd_conv",
    )(rest, rest, rest, conv_w, conv_b)


def _ssd_kernel(x_ref, g_ref, gt_ref, gb_ref, gbt_ref, al_ref, alt_ref, o_ref, s_scr):
    d = pl.program_id(0)
    s = pl.program_id(1)

    @pl.when(s == 0)
    def _():
        s_scr[...] = jnp.zeros_like(s_scr)

    mask, mf = _scan_masks(d)
    g, gt = _direction_gates(d, g_ref, gt_ref)
    dt = _softplus(g + gb_ref[0])
    dtt = _softplus(gt + gbt_ref[0])
    inc = dt * (-jnp.exp(al_ref[0]))
    inct = dtt * (-jnp.exp(alt_ref[0]))
    cum, cumt = _masked_cumsum(mf, inc, inct)
    tot = jnp.sum(inc, axis=0, keepdims=True)

    hpg = SSM_H // SSM_G
    for gi in range(SSM_G):
        bm = x_ref[:, SSM_W + gi * SSM_N:SSM_W + (gi + 1) * SSM_N].astype(BF16)
        cm = x_ref[:, SSM_W + (SSM_G + gi) * SSM_N:SSM_W + (SSM_G + gi + 1) * SSM_N].astype(BF16)
        gmat = _dot_nt(cm, bm)
        for hh in range(hpg):
            h = gi * hpg + hh
            la_col = cum[:, G_DT + h:G_DT + h + 1]
            la_row = cumt[G_DT + h:G_DT + h + 1, :]
            la_last = tot[:, G_DT + h:G_DT + h + 1]
            dt_col = dt[:, G_DT + h:G_DT + h + 1]
            state = s_scr[h]
            decay = jnp.exp(jnp.where(mask, la_col - la_row, -jnp.inf))
            xdt = x_ref[:, h * SSM_P:(h + 1) * SSM_P] * dt_col
            y = (_dot((gmat * decay).astype(BF16), xdt.astype(BF16))
                 + _dot_nt(cm, state.astype(BF16)) * jnp.exp(la_col))
            o_ref[0, :, h * SSM_P:(h + 1) * SSM_P] = y
            tail = jnp.exp(la_last - la_col)
            s_scr[h] = state * jnp.exp(la_last) + _dot_tn((xdt * tail).astype(BF16), bm)


def ssd_scan(xbc_act, gates, gates_t, gbias, gbias_t, alog, alog_t):
    t_all = xbc_act.shape[0]
    n_chunks = t_all // CHUNK
    cm = lambda d, s: _scan_chunk(d, s, n_chunks)
    return pl.pallas_call(
        _ssd_kernel,
        grid=(2, n_chunks),
        in_specs=[pl.BlockSpec((CHUNK, CONV_CH), lambda d, s: (cm(d, s), 0)),
                  pl.BlockSpec((CHUNK, LANES), lambda d, s: (cm(d, s), 0)),
                  pl.BlockSpec((LANES, CHUNK), lambda d, s: (0, cm(d, s))),
                  pl.BlockSpec((1, 1, LANES), lambda d, s: (d, 0, 0)),
                  pl.BlockSpec((1, LANES, 1), lambda d, s: (d, 0, 0)),
                  pl.BlockSpec((1, 1, LANES), lambda d, s: (d, 0, 0)),
                  pl.BlockSpec((1, LANES, 1), lambda d, s: (d, 0, 0))],
        out_specs=pl.BlockSpec((1, CHUNK, SSM_W), lambda d, s: (d, cm(d, s), 0)),
        out_shape=jax.ShapeDtypeStruct((2, t_all, SSM_W), F32),
        scratch_shapes=[pltpu.VMEM((SSM_H, SSM_P, SSM_N), F32)],
        compiler_params=_cparams(("arbitrary", "arbitrary")),
        name="ssd_scan",
    )(xbc_act, gates, gates_t, gbias, gbias_t, alog, alog_t)


OUT_TM = 640
OUT_TN = 512
OUT_SUB = 128


def _out_proj_kernel(ona_ref, hs_ref, mo_ref, ys_ref, xs_ref, z_ref, mlw_ref, dsk_ref, ssw_ref,
                     w_ref, x_ref, g_ref, o_ref, a_scr):
    i = pl.program_id(0)
    j = pl.program_id(1)

    @pl.when(j == 0)
    def _():
        def body(r, carry):
            rows = pl.ds(pl.multiple_of(r * OUT_SUB, OUT_SUB), OUT_SUB)
            a_scr[rows, 0:NA_W] = ona_ref[rows, :]
            hsum = hs_ref[0, rows, :] + hs_ref[1, rows, :]
            gate = 1.0 / (1.0 + jnp.exp(-mo_ref[rows, :]))
            for h in range(ML_H):
                cs = slice(h * ML_DH, (h + 1) * ML_DH)
                hh = hsum[:, cs]
                mu = jnp.mean(hh, axis=-1, keepdims=True)
                var = jnp.mean(jnp.square(hh - mu), axis=-1, keepdims=True)
                hn = (hh - mu) * lax.rsqrt(var + EPS) * mlw_ref[0, :, cs]
                a_scr[rows, NA_W + h * ML_DH:NA_W + (h + 1) * ML_DH] = (gate[:, cs] * hn).astype(BF16)
            y = ys_ref[0, rows, :] + ys_ref[1, rows, :] + dsk_ref[0] * xs_ref[rows, :]
            y = y * _silu(z_ref[rows, :])
            gw = SSM_W // SSM_G
            for gi in range(SSM_G):
                cs = slice(gi * gw, (gi + 1) * gw)
                yg = y[:, cs]
                yn = yg * lax.rsqrt(jnp.mean(yg * yg, axis=-1, keepdims=True) + EPS) * ssw_ref[0, :, cs]
                a_scr[rows, NA_W + ML_W + gi * gw:NA_W + ML_W + (gi + 1) * gw] = yn.astype(BF16)
            return carry
        lax.fori_loop(0, OUT_TM // OUT_SUB, body, 0)

    acc = _dot(a_scr[...], w_ref[0])
    is_ctx = (i * OUT_TM + lax.broadcasted_iota(jnp.int32, (OUT_TM, 1), 0)) < CTX
    gate1 = jnp.where(is_ctx, g_ref[1:2, :], g_ref[0:1, :])
    o_ref[...] = x_ref[...] + gate1 * acc


def out_projection(o_na, hs, rest, ys, xbc_act, ml_norm_w, d_skip_vec, ssm_norm_w, w_out, xa, g1, layer):
    t_all, d = xa.shape
    cw = lambda off: off // ML_W
    vec = pl.BlockSpec((1, 1, ML_W), lambda i, j: (layer, 0, 0))
    return pl.pallas_call(
        _out_proj_kernel,
        grid=(t_all // OUT_TM, d // OUT_TN),
        in_specs=[pl.BlockSpec((OUT_TM, NA_W), lambda i, j: (i, 0)),
                  pl.BlockSpec((2, OUT_TM, ML_W), lambda i, j: (0, i, 0)),
                  pl.BlockSpec((OUT_TM, ML_W), lambda i, j: (i, cw(R_MO))),
                  pl.BlockSpec((2, OUT_TM, SSM_W), lambda i, j: (0, i, 0)),
                  pl.BlockSpec((OUT_TM, SSM_W), lambda i, j: (i, 0)),
                  pl.BlockSpec((OUT_TM, SSM_W), lambda i, j: (i, cw(R_Z))),
                  vec, vec, vec,
                  pl.BlockSpec((1, d, OUT_TN), lambda i, j: (layer, 0, j)),
                  pl.BlockSpec((OUT_TM, OUT_TN), lambda i, j: (i, j)),
                  pl.BlockSpec((2, OUT_TN), lambda i, j: (0, j))],
        out_specs=pl.BlockSpec((OUT_TM, OUT_TN), lambda i, j: (i, j)),
        out_shape=jax.ShapeDtypeStruct((t_all, d), F32),
        scratch_shapes=[pltpu.VMEM((OUT_TM, d), BF16)],
        compiler_params=_cparams(("arbitrary", "arbitrary")),
        name="out_projection",
    )(o_na, hs, rest, ys, xbc_act, rest, ml_norm_w, d_skip_vec, ssm_norm_w, w_out, xa, g1)


RT_TM = 640
RT_SUB = 128


def _router_kernel(x_ref, nw_ref, sc_ref, sh_ref, wr_ref, br_ref, h_ref, r_ref, cnt_ref, run_scr):
    i = pl.program_id(0)

    @pl.when(i == 0)
    def _():
        run_scr[...] = jnp.zeros_like(run_scr)

    t_row = lax.broadcasted_iota(jnp.int32, (RT_SUB, RT_SUB), 0)
    t_col = lax.broadcasted_iota(jnp.int32, (RT_SUB, RT_SUB), 1)
    earlier = (t_col < t_row).astype(BF16)
    wr_hi, wr_lo = _split2(wr_ref[0])

    def body(r, run):
        rows = pl.ds(pl.multiple_of(r * RT_SUB, RT_SUB), RT_SUB)
        h = _modulated_norm(x_ref[rows, :], nw_ref[...], sc_ref[...], sh_ref[...], i * RT_TM + r * RT_SUB)
        h_ref[rows, :] = h
        logit = _dot_split(h, wr_hi, wr_lo) + br_ref[0]
        lane = lax.broadcasted_iota(jnp.int32, logit.shape, 1)
        big = jnp.int32(LANES)
        is_g = lane < MOE_GROUPS
        lg = jnp.where(is_g, logit, -jnp.inf)
        gmax = jnp.max(lg, axis=-1, keepdims=True)
        g_sel = jnp.min(jnp.where(is_g & (lg == gmax), lane, big), axis=-1, keepdims=True)
        g_w = 1.0 / jnp.sum(jnp.exp(lg - gmax), axis=-1, keepdims=True)
        lo = MOE_GROUPS + g_sel * MOE_PER_GROUP
        in_g = (lane >= lo) & (lane < lo + MOE_PER_GROUP)
        le = jnp.where(in_g, logit, -jnp.inf)
        v1 = jnp.max(le, axis=-1, keepdims=True)
        i1 = jnp.min(jnp.where(in_g & (le == v1), lane, big), axis=-1, keepdims=True)
        le2 = jnp.where(lane == i1, -jnp.inf, le)
        v2 = jnp.max(le2, axis=-1, keepdims=True)
        i2 = jnp.min(jnp.where(in_g & (lane != i1) & (le2 == v2), lane, big), axis=-1, keepdims=True)
        e2 = jnp.exp(v2 - v1)
        w1 = g_w / (1.0 + e2)
        w2 = g_w * e2 / (1.0 + e2)
        oh1 = (lane == i1 - MOE_GROUPS).astype(F32)
        oh2 = (lane == i2 - MOE_GROUPS).astype(F32)
        oh = oh1 + oh2
        before = _dot(earlier, oh.astype(BF16)) + run
        rank1 = jnp.sum(before * oh1, axis=-1, keepdims=True)
        rank2 = jnp.sum(before * oh2, axis=-1, keepdims=True)
        out = jnp.where(lane == 0, (i1 - MOE_GROUPS).astype(F32),
                        jnp.where(lane == 1, (i2 - MOE_GROUPS).astype(F32),
                                  jnp.where(lane == 2, w1,
                                            jnp.where(lane == 3, w2,
                                                      jnp.where(lane == 4, rank1,
                                                                jnp.where(lane == 5, rank2, 0.0))))))
        r_ref[rows, :] = out
        return run + jnp.sum(oh, axis=0, keepdims=True)
    run = lax.fori_loop(0, RT_TM // RT_SUB, body, run_scr[...])
    run_scr[...] = run
    cnt_ref[...] = jnp.broadcast_to(run, cnt_ref.shape)


def moe_router(xa, nw, sc2, sh2, w_route, b_route, layer):
    t_all, d = xa.shape
    return pl.pallas_call(
        _router_kernel,
        grid=(t_all // RT_TM,),
        in_specs=[pl.BlockSpec((RT_TM, d), lambda i: (i, 0)),
                  pl.BlockSpec((1, d), lambda i: (0, 0)),
                  pl.BlockSpec((2, d), lambda i: (0, 0)),
                  pl.BlockSpec((2, d), lambda i: (0, 0)),
                  pl.BlockSpec((1, d, LANES), lambda i: (layer, 0, 0)),
                  pl.BlockSpec((1, 1, LANES), lambda i: (layer, 0, 0))],
        out_specs=[pl.BlockSpec((RT_TM, d), lambda i: (i, 0)),
                   pl.BlockSpec((RT_TM, LANES), lambda i: (i, 0)),
                   pl.BlockSpec((8, LANES), lambda i: (0, 0))],
        out_shape=[jax.ShapeDtypeStruct((t_all, d), F32),
                   jax.ShapeDtypeStruct((t_all, LANES), F32),
                   jax.ShapeDtypeStruct((8, LANES), F32)],
        scratch_shapes=[pltpu.VMEM((1, LANES), F32)],
        compiler_params=_cparams(("arbitrary",)),
        name="moe_router",
    )(xa, nw, sc2, sh2, w_route, b_route)


EXP_SLOTS = 3


def _expert_kernel(be_ref, nu_ref, st_ref, h_hbm, w1_ref, w3_ref, w2_ref, o_ref, xbuf, sem):
    b = pl.program_id(0)
    n_used = nu_ref[0]

    def row_copy(blk, slot, t):
        tok = st_ref[blk * MOE_BM + t]
        return pltpu.make_async_copy(h_hbm.at[pl.ds(tok, 1), :], xbuf.at[slot, pl.ds(t, 1), :], sem.at[slot])

    @pl.when(b == 0)
    def _():
        def body(t, carry):
            row_copy(0, 0, t).start()
            row_copy(1, 1, t).start()
            return carry
        lax.fori_loop(0, MOE_BM, body, 0)

    @pl.when(b + 2 < n_used)
    def _():
        for t in range(MOE_BM):
            row_copy(b + 2, (b + 2) % EXP_SLOTS, t).start()

    @pl.when((b == 1) & (n_used == 1))
    def _():
        pltpu.make_async_copy(h_hbm.at[pl.ds(0, MOE_BM), :], xbuf.at[1], sem.at[1]).wait()

    @pl.when(b < n_used)
    def _():
        slot = b % EXP_SLOTS
        pltpu.make_async_copy(h_hbm.at[pl.ds(0, MOE_BM), :], xbuf.at[slot], sem.at[slot]).wait()
        x = xbuf[slot].astype(BF16)
        hid = _silu(_dot(x, w1_ref[0, 0])) * _dot(x, w3_ref[0, 0])
        o_ref[...] = _dot(hid.astype(BF16), w2_ref[0, 0])

    @pl.when(b >= n_used)
    def _():
        o_ref[...] = jnp.zeros_like(o_ref)


def expert_blocks(block_e, n_used, slot_tok, h2, w1, w3, w2, layer):
    d = h2.shape[1]
    ff = w1.shape[-1]
    n_slots = slot_tok.shape[0]
    n_blocks = n_slots // MOE_BM
    return pl.pallas_call(
        _expert_kernel,
        grid_spec=pltpu.PrefetchScalarGridSpec(
            num_scalar_prefetch=3,
            grid=(n_blocks,),
            in_specs=[pl.BlockSpec(memory_space=pl.ANY),
                      pl.BlockSpec((1, 1, d, ff), lambda b, be, nu, st: (layer, be[b], 0, 0)),
                      pl.BlockSpec((1, 1, d, ff), lambda b, be, nu, st: (layer, be[b], 0, 0)),
                      pl.BlockSpec((1, 1, ff, d), lambda b, be, nu, st: (layer, be[b], 0, 0))],
            out_specs=pl.BlockSpec((MOE_BM, d), lambda b, be, nu, st: (b, 0)),
            scratch_shapes=[pltpu.VMEM((EXP_SLOTS, MOE_BM, d), F32),
                            pltpu.SemaphoreType.DMA((EXP_SLOTS,))]),
        out_shape=jax.ShapeDtypeStruct((n_slots, d), F32),
        compiler_params=_cparams(("arbitrary",)),
        name="moe_experts",
    )(block_e, n_used, slot_tok, h2, w1, w3, w2)


def _combine_kernel(dest_ref, x_ref, r_ref, g_ref, fw_ref, y_hbm, o_ref, ybuf, sem, *,
                    row_block0, n_tiles, final_norm):
    i = pl.program_id(0)

    def row_copy(tile, slot, t, k):
        row = dest_ref[(tile + row_block0) * (2 * CHUNK) + 2 * t + k]
        return pltpu.make_async_copy(y_hbm.at[pl.ds(row, 1), :],
                                     ybuf.at[slot, pl.ds(k * CHUNK + t, 1), :], sem.at[slot])

    @pl.when(i == 0)
    def _():
        def body(t, carry):
            row_copy(0, 0, t, 0).start()
            row_copy(0, 0, t, 1).start()
            return carry
        lax.fori_loop(0, CHUNK, body, 0)

    @pl.when(i + 1 < n_tiles)
    def _():
        for t in range(CHUNK):
            row_copy(i + 1, (i + 1) % 2, t, 0).start()
            row_copy(i + 1, (i + 1) % 2, t, 1).start()

    slot = i % 2
    pltpu.make_async_copy(y_hbm.at[pl.ds(0, 2 * CHUNK), :], ybuf.at[slot], sem.at[slot]).wait()

    rt = r_ref[...]
    y = rt[:, 2:3] * ybuf[slot, 0:CHUNK, :] + rt[:, 3:4] * ybuf[slot, CHUNK:2 * CHUNK, :]
    is_ctx = ((i + row_block0) * CHUNK + lax.broadcasted_iota(jnp.int32, (CHUNK, 1), 0)) < CTX
    gate2 = jnp.where(is_ctx, g_ref[1:2, :], g_ref[0:1, :])
    x = x_ref[...] + gate2 * y
    if final_norm:
        x = x * lax.rsqrt(jnp.mean(x * x, axis=-1, keepdims=True) + EPS) * fw_ref[...]
    o_ref[...] = x


def moe_combine(dest, xa, route, g2, final_w, y, final_norm):
    t_all, d = xa.shape
    rb0 = CTX // CHUNK if final_norm else 0
    n_out = t_all // CHUNK - rb0
    return pl.pallas_call(
        functools.partial(_combine_kernel, row_block0=rb0, n_tiles=n_out, final_norm=final_norm),
        grid_spec=pltpu.PrefetchScalarGridSpec(
            num_scalar_prefetch=1,
            grid=(n_out,),
            in_specs=[pl.BlockSpec((CHUNK, d), lambda i, ds: (i + rb0, 0)),
                      pl.BlockSpec((CHUNK, LANES), lambda i, ds: (i + rb0, 0)),
                      pl.BlockSpec((2, d), lambda i, ds: (0, 0)),
                      pl.BlockSpec((1, d), lambda i, ds: (0, 0)),
                      pl.BlockSpec(memory_space=pl.ANY)],
            out_specs=pl.BlockSpec((CHUNK, d), lambda i, ds: (i, 0)),
            scratch_shapes=[pltpu.VMEM((2, 2 * CHUNK, d), F32),
                            pltpu.SemaphoreType.DMA((2,))]),
        out_shape=jax.ShapeDtypeStruct((n_out * CHUNK, d), F32),
        compiler_params=_cparams(("arbitrary",)),
        name="moe_combine",
    )(dest, xa, route, g2, final_w, y)


def moe_dispatch(route, counts):
    n = route.shape[0]
    n_assign = 2 * n
    e_flat = route[:, 0:2].astype(jnp.int32).reshape(-1)
    rank_flat = route[:, 4:6].astype(jnp.int32).reshape(-1)
    padded = (counts + MOE_BM - 1) // MOE_BM * MOE_BM
    pends = jnp.cumsum(padded)
    pstarts = pends - padded
    experts = jnp.arange(N_EXPERTS, dtype=jnp.int32)
    dest = jnp.sum(jnp.where(e_flat[:, None] == experts[None, :], pstarts[None, :], 0), axis=1) + rank_flat
    n_blocks = -(-n_assign // MOE_BM) + N_EXPERTS
    n_slots = n_blocks * MOE_BM
    tok_flat = jnp.arange(n_assign, dtype=jnp.int32) // 2
    slot_tok = jnp.zeros((n_slots,), jnp.int32).at[dest].set(tok_flat)
    block_start = jnp.arange(n_blocks, dtype=jnp.int32) * MOE_BM
    block_e = jnp.minimum(jnp.sum((pends[None, :] <= block_start[:, None]).astype(jnp.int32), axis=1),
                          N_EXPERTS - 1)
    n_used = (pends[-1] // MOE_BM).astype(jnp.int32).reshape(1)
    return dest, slot_tok, block_e, n_used


def hier_moe(xa, nw, sc2, sh2, g2, w_route, b_route, w1, w3, w2, final_w, layer, final_norm):
    h2, route, cnt = moe_router(xa, nw, sc2, sh2, w_route, b_route, layer)
    dest, slot_tok, block_e, n_used = moe_dispatch(route, cnt[0, :N_EXPERTS].astype(jnp.int32))
    y = expert_blocks(block_e, n_used, slot_tok, h2, w1, w3, w2, layer)
    return moe_combine(dest, xa, route, g2, final_w, y, final_norm)


def rope_tables(t_lat):
    nf = ML_DH // 4
    inv = ROPE_THETA ** (-jnp.arange(nf, dtype=F32) / nf)
    t_idx = jnp.arange(t_lat)
    ang_r = (t_idx // GRID_W).astype(F32)[:, None] * inv
    ang_c = (t_idx % GRID_W).astype(F32)[:, None] * inv
    cos = jnp.concatenate([jnp.cos(ang_r)] * 2 + [jnp.cos(ang_c)] * 2, axis=-1)
    sin = jnp.concatenate([-jnp.sin(ang_r), jnp.sin(ang_r), -jnp.sin(ang_c), jnp.sin(ang_c)], axis=-1)
    cos = jnp.concatenate([jnp.ones((CTX, ML_DH), F32), cos], axis=0)
    sin = jnp.concatenate([jnp.zeros((CTX, ML_DH), F32), sin], axis=0)
    return cos, sin


def _pad_lanes(v):
    return jnp.pad(v, [(0, 0)] * (v.ndim - 1) + [(0, LANES - v.shape[-1])])


def kernel(x, c, ctx, c_ctx, w_mod, b_mod, norm1_w, norm2_w, w_in, w_out, na_rpb, ml_gate_b, ml_norm_w, ssm_conv_w, ssm_conv_b, ssm_dt_bias, ssm_a_log, ssm_d, ssm_norm_w, router_g_w, router_g_b, router_e_w, router_e_b, moe_w1, moe_w3, moe_w2, final_norm_w):
    depth = w_mod.shape[0]
    t_lat = x.shape[1]
    d = D_MODEL
    xa = jnp.concatenate([ctx[0], x[0]], axis=0)

    cond = jnp.zeros((8, d), F32).at[0].set(c[0]).at[1].set(c_ctx)
    mod = adaln_modulation(cond, w_mod, b_mod)[:, 0:2, :].reshape(depth, 2, 6, d)
    cos_tab, sin_tab = rope_tables(t_lat)

    o_mq = 3 * NA_W
    o_mg = o_mq + 4 * ML_W
    o_z = o_mg + 4 * ML_H
    o_xbc = o_z + SSM_W
    o_dt = o_xbc + CONV_CH
    w_main = jnp.concatenate([w_in[:, :, 0:o_mq], w_in[:, :, o_xbc:o_dt], w_in[:, :, o_mq:o_mg],
                              w_in[:, :, o_z:o_xbc]], axis=2).astype(BF16)
    w_mg, w_dt = w_in[:, :, o_mg:o_z], w_in[:, :, o_dt:]
    w_gate = _pad_lanes(jnp.concatenate([w_mg[:, :, 0:8], w_dt[:, :, 0:8], w_mg[:, :, 8:16], w_dt[:, :, 8:16]],
                                        axis=2))
    w_out_b = w_out.astype(BF16)
    w1_b, w3_b, w2_b = moe_w1.astype(BF16), moe_w3.astype(BF16), moe_w2.astype(BF16)
    w_route = _pad_lanes(jnp.concatenate([router_g_w, router_e_w], axis=2))
    b_route = _pad_lanes(jnp.concatenate([router_g_b, router_e_b], axis=1))[:, None, :]
    gb4 = ml_gate_b.reshape(depth, 2, 2 * ML_H)
    gbias = _pad_lanes(jnp.concatenate([gb4, ssm_dt_bias], axis=2))
    alog = _pad_lanes(jnp.concatenate([jnp.zeros_like(ssm_a_log), ssm_a_log], axis=2))
    d_skip_vec = jnp.repeat(ssm_d, SSM_P, axis=1)[:, None, :]
    conv_b = ssm_conv_b[:, None, :]

    out = None
    for l in range(depth):
        last = l == depth - 1
        sh1, sc1, g1, sh2, sc2, g2 = (mod[l, :, p, :] for p in range(6))

        qkv, rest, gates, gates_t = in_projection(xa, norm1_w[l][None], sc1, sh1, w_main, w_gate, l)
        o_na = neighbourhood_attention(qkv, na_bias_table(na_rpb[l]))
        hs = mlstm_scan(rest, gates, gates_t, gbias[l][:, None, :], gbias[l][:, :, None], cos_tab, sin_tab)
        xbc_act = ssd_conv(rest, ssm_conv_w, conv_b, l)
        ys = ssd_scan(xbc_act, gates, gates_t, gbias[l][:, None, :], gbias[l][:, :, None],
                      alog[l][:, None, :], alog[l][:, :, None])
        xa = out_projection(o_na, hs, rest, ys, xbc_act, ml_norm_w[:, None, :], d_skip_vec,
                            ssm_norm_w[:, None, :], w_out_b, xa, g1, l)
        res = hier_moe(xa, norm2_w[l][None], sc2, sh2, g2, w_route, b_route, w1_b, w3_b, w2_b,
                       final_norm_w[None], l, last)
        if last:
            out = res
        else:
            xa = res
    return out[None]
```

```python
import functools

import jax
import jax.numpy as jnp
import numpy as np
from jax import lax
from jax.experimental import pallas as pl
from jax.experimental.pallas import tpu as pltpu

F32 = jnp.float32
BF16 = jnp.bfloat16
HIGHEST = lax.Precision.HIGHEST

D_MODEL = 2048
GRID_W = 64
CTX = 256
NA_DH = 128
NA_W = 1024
NA_H = 8
NA_KR = 8
NA_KC = 16
ML_DH = 128
ML_W = 512
ML_H = 4
SSM_P = 64
SSM_W = 512
SSM_H = 8
SSM_G = 2
SSM_N = 128
CONV_CH = SSM_W + 2 * SSM_G * SSM_N
MOE_GROUPS = 4
MOE_PER_GROUP = 8
N_EXPERTS = 32
MOE_FF = 1024
ROPE_THETA = 10000.0
EPS = 1e-6

CHUNK = 256
LANES = 128
NEG = -1e30
MOE_BM = 512
VMEM_LIMIT = 56 * 1024 * 1024

R_XBC, R_MQ, R_MK, R_MV, R_MO, R_Z = 0, 1024, 1536, 2048, 2560, 3072
REST_W = 3584
QKV_W = 3 * NA_W
G_I, G_F, G_DT = 0, 4, 8
G_DIR = 16


def _cparams(sem):
    return pltpu.CompilerParams(dimension_semantics=sem, vmem_limit_bytes=VMEM_LIMIT)


def _silu(x):
    return x / (1.0 + jnp.exp(-x))


def _softplus(x):
    return jnp.maximum(x, 0.0) + jnp.log1p(jnp.exp(-jnp.abs(x)))


def _dot(a, b):
    return jnp.dot(a, b, preferred_element_type=F32)


def _dot_nt(a, b):
    return lax.dot_general(a, b, (((1,), (1,)), ((), ())), preferred_element_type=F32)


def _dot_tn(a, b):
    return lax.dot_general(a, b, (((0,), (0,)), ((), ())), preferred_element_type=F32)


def _split2(x):
    hi = x.astype(BF16)
    return hi, (x - hi.astype(F32)).astype(BF16)


def _dot_split(a, w_hi, w_lo):
    a_hi, a_lo = _split2(a)
    return _dot(a_hi, w_hi) + _dot(a_hi, w_lo) + _dot(a_lo, w_hi)


def _mod_kernel(c_ref, w_ref, b_ref, o_ref):
    o_ref[0] = jnp.dot(_silu(c_ref[...]), w_ref[0], preferred_element_type=F32, precision=HIGHEST) + b_ref[0]


def adaln_modulation(cond, w_mod, b_mod):
    depth, d, n = w_mod.shape
    tn = 512
    return pl.pallas_call(
        _mod_kernel,
        grid=(depth, n // tn),
        in_specs=[pl.BlockSpec((8, d), lambda l, j: (0, 0)),
                  pl.BlockSpec((1, d, tn), lambda l, j: (l, 0, j)),
                  pl.BlockSpec((1, 1, tn), lambda l, j: (l, 0, j))],
        out_specs=pl.BlockSpec((1, 8, tn), lambda l, j: (l, 0, j)),
        out_shape=jax.ShapeDtypeStruct((depth, 8, n), F32),
        compiler_params=_cparams(("parallel", "parallel")),
        name="adaln_modulation",
    )(cond, w_mod, b_mod.reshape(depth, 1, n))


def _modulated_norm(x, nw, sc2, sh2, row0):
    r = x.shape[0]
    y = x * lax.rsqrt(jnp.mean(x * x, axis=-1, keepdims=True) + EPS) * nw
    is_ctx = (row0 + lax.broadcasted_iota(jnp.int32, (r, 1), 0)) < CTX
    sc = jnp.where(is_ctx, sc2[1:2, :], sc2[0:1, :])
    sh = jnp.where(is_ctx, sh2[1:2, :], sh2[0:1, :])
    return y * (1.0 + sc) + sh


IN_TM = 1280
IN_TN = 512
IN_SUB = 256


def _in_proj_kernel(x_ref, nw_ref, sc_ref, sh_ref, wm_ref, wg_ref, qkv_ref, rest_ref, gate_ref, gate_t_ref, a_scr):
    i = pl.program_id(0)
    j = pl.program_id(1)
    n_qkv = QKV_W // IN_TN

    @pl.when(j == 0)
    def _():
        wg_hi, wg_lo = _split2(wg_ref[0])
        for r in range(IN_TM // IN_SUB):
            rows = slice(r * IN_SUB, (r + 1) * IN_SUB)
            h = _modulated_norm(x_ref[rows, :], nw_ref[...], sc_ref[...], sh_ref[...], i * IN_TM + r * IN_SUB)
            a_scr[rows, :] = h.astype(BF16)
            gate = _dot_split(h, wg_hi, wg_lo)
            gate_ref[rows, :] = gate
            gate_t_ref[:, rows] = gate.T

    acc = _dot(a_scr[...], wm_ref[0])

    @pl.when(j < n_qkv)
    def _():
        qkv_ref[...] = acc.astype(BF16)

    @pl.when(j >= n_qkv)
    def _():
        rest_ref[...] = acc


def in_projection(xa, nw, sc2, sh2, w_main, w_gate, layer):
    t_all, d = xa.shape
    n_main = w_main.shape[2]
    n_qkv = QKV_W // IN_TN
    return pl.pallas_call(
        _in_proj_kernel,
        grid=(t_all // IN_TM, n_main // IN_TN),
        in_specs=[pl.BlockSpec((IN_TM, d), lambda i, j: (i, 0)),
                  pl.BlockSpec((1, d), lambda i, j: (0, 0)),
                  pl.BlockSpec((2, d), lambda i, j: (0, 0)),
                  pl.BlockSpec((2, d), lambda i, j: (0, 0)),
                  pl.BlockSpec((1, d, IN_TN), lambda i, j: (layer, 0, j)),
                  pl.BlockSpec((1, d, LANES), lambda i, j: (layer, 0, 0))],
        out_specs=[pl.BlockSpec((IN_TM, IN_TN), lambda i, j: (i, jnp.minimum(j, n_qkv - 1))),
                   pl.BlockSpec((IN_TM, IN_TN), lambda i, j: (i, jnp.maximum(j - n_qkv, 0))),
                   pl.BlockSpec((IN_TM, LANES), lambda i, j: (i, 0)),
                   pl.BlockSpec((LANES, IN_TM), lambda i, j: (0, i))],
        out_shape=[jax.ShapeDtypeStruct((t_all, QKV_W), BF16),
                   jax.ShapeDtypeStruct((t_all, REST_W), F32),
                   jax.ShapeDtypeStruct((t_all, LANES), F32),
                   jax.ShapeDtypeStruct((LANES, t_all), F32)],
        scratch_shapes=[pltpu.VMEM((IN_TM, d), BF16)],
        compiler_params=_cparams(("arbitrary", "arbitrary")),
        name="in_projection",
    )(xa, nw, sc2, sh2, w_main, w_gate)


NA_RB = CHUNK // GRID_W
NA_UR = 12


NA_HB = 2


def _na_kernel(q_ref, k_ref, v_ref, bias_ref, o_ref, *, n_rows):
    rb = pl.program_id(1)
    scale = NA_DH ** -0.5

    @pl.when(rb == 0)
    def _():
        for hh in range(NA_HB):
            hs = slice(hh * NA_DH, (hh + 1) * NA_DH)
            s = _dot_nt(q_ref[:, hs], k_ref[0:CTX, hs]) * scale
            m = jnp.max(s, axis=-1, keepdims=True)
            p = jnp.exp(s - m)
            l = jnp.sum(p, axis=-1, keepdims=True)
            o_ref[:, hs] = (_dot(p.astype(BF16), v_ref[0:CTX, hs]) / l).astype(o_ref.dtype)

    @pl.when(rb > 0)
    def _():
        r0 = (rb - 1) * NA_RB
        u0 = jnp.clip(r0 - NA_KR // 2, 0, n_rows - NA_UR)
        case = jnp.where(r0 == 0, 1, jnp.where(r0 == n_rows - NA_RB, 2, 0))
        start = pl.multiple_of(CTX + u0 * GRID_W, GRID_W)
        for hh in range(NA_HB):
            hs = slice(hh * NA_DH, (hh + 1) * NA_DH)
            q = q_ref[:, hs]
            kc = k_ref[0:CTX, hs]
            vc = v_ref[0:CTX, hs]
            kw = k_ref[pl.ds(start, NA_UR * GRID_W), hs]
            vw = v_ref[pl.ds(start, NA_UR * GRID_W), hs]
            s = _dot_nt(q, kw) * scale + bias_ref[hh, case]
            sc = _dot_nt(q, kc) * scale
            m = jnp.maximum(jnp.max(s, axis=-1, keepdims=True), jnp.max(sc, axis=-1, keepdims=True))
            p = jnp.exp(s - m)
            pc = jnp.exp(sc - m)
            l = jnp.sum(p, axis=-1, keepdims=True) + jnp.sum(pc, axis=-1, keepdims=True)
            o = _dot(p.astype(BF16), vw) + _dot(pc.astype(BF16), vc)
            o_ref[:, hs] = (o / l).astype(o_ref.dtype)


def na_bias_table(rpb):
    h = rpb.shape[0]
    c = np.arange(GRID_W)
    cs = np.clip(c - NA_KC // 2, 0, GRID_W - NA_KC)
    kcol = np.arange(GRID_W)
    inside = (kcol[None, :] >= cs[:, None]) & (kcol[None, :] < cs[:, None] + NA_KC)
    off = kcol[None, :] - c[:, None] + NA_KC - 1
    onehot = (off[None] == np.arange(2 * NA_KC - 1)[:, None, None]) & inside[None]
    band = jnp.einsum('hro,ock->hrck', rpb.astype(F32), jnp.asarray(onehot, F32), precision=HIGHEST)
    band = band + jnp.asarray(np.where(inside, 0.0, NEG), F32)
    neg = jnp.full((h, GRID_W, GRID_W), NEG, F32)
    cases = ([(rr, NA_KR // 2 - 1) for rr in range(NA_RB)],
             [(0, NA_KR - 1 - rr) for rr in range(NA_RB)],
             [(NA_UR - NA_KR, NA_KR // 2 - 1 - rr) for rr in range(NA_RB)])
    tabs = []
    for case in cases:
        rows = []
        for w_off, d0 in case:
            rows.append(jnp.concatenate(
                [band[:, d0 + u - w_off] if 0 <= u - w_off < NA_KR else neg for u in range(NA_UR)], axis=-1))
        tabs.append(jnp.concatenate(rows, axis=1))
    return jnp.stack(tabs, axis=1)


def neighbourhood_attention(qkv, bias_tab):
    t_all = qkv.shape[0]
    n_rows = (t_all - CTX) // GRID_W
    n_rb = t_all // CHUNK
    return pl.pallas_call(
        functools.partial(_na_kernel, n_rows=n_rows),
        grid=(NA_H // NA_HB, n_rb),
        in_specs=[pl.BlockSpec((CHUNK, NA_HB * NA_DH), lambda h, rb: (rb, h)),
                  pl.BlockSpec((t_all, NA_HB * NA_DH), lambda h, rb: (0, NA_H // NA_HB + h)),
                  pl.BlockSpec((t_all, NA_HB * NA_DH), lambda h, rb: (0, 2 * (NA_H // NA_HB) + h)),
                  pl.BlockSpec((NA_HB, 3, CHUNK, NA_UR * GRID_W), lambda h, rb: (h, 0, 0, 0))],
        out_specs=pl.BlockSpec((CHUNK, NA_HB * NA_DH), lambda h, rb: (rb, h)),
        out_shape=jax.ShapeDtypeStruct((t_all, NA_W), BF16),
        compiler_params=_cparams(("parallel", "parallel")),
        name="neighbourhood_attention",
    )(qkv, qkv, qkv, bias_tab)


def _scan_chunk(d, s, n_chunks):
    return jnp.where(d == 0, s, jnp.where(s == 0, 0, n_chunks - s))


def _scan_masks(d):
    row = lax.broadcasted_iota(jnp.int32, (CHUNK, CHUNK), 0)
    col = lax.broadcasted_iota(jnp.int32, (CHUNK, CHUNK), 1)
    mask = jnp.where(d == 0, row - col, col - row) >= 0
    return mask, mask.astype(BF16)


def _split3(x):
    hi = x.astype(BF16)
    r = x - hi.astype(F32)
    mid = r.astype(BF16)
    return hi, mid, (r - mid.astype(F32)).astype(BF16)


def _masked_cumsum(mb, x, xt):
    cum = sum(_dot(mb, p) for p in _split3(x))
    cumt = sum(_dot_nt(p, mb) for p in _split3(xt))
    return cum, cumt


def _direction_gates(d, g_ref, gt_ref):
    g = g_ref[...]
    gt = gt_ref[...]
    g = jnp.where(d == 0, g, pltpu.roll(g, LANES - G_DIR, 1))
    gt = jnp.where(d == 0, gt, pltpu.roll(gt, LANES - G_DIR, 0))
    return g, gt


def _rope(x, cos, sin_signed):
    lane = lax.broadcasted_iota(jnp.int32, x.shape, 1)
    nf = ML_DH // 4
    partner = jnp.where((lane % (2 * nf)) < nf, pltpu.roll(x, ML_DH - nf, 1), pltpu.roll(x, nf, 1))
    return x * cos + partner * sin_signed


def _mlstm_kernel(q_ref, k_ref, v_ref, g_ref, gt_ref, gb_ref, gbt_ref, cos_ref, sin_ref, o_ref,
                  c_scr, n_scr, m_scr):
    d = pl.program_id(0)
    s = pl.program_id(1)

    @pl.when(s == 0)
    def _():
        c_scr[...] = jnp.zeros_like(c_scr)
        n_scr[...] = jnp.zeros_like(n_scr)
        m_scr[...] = jnp.zeros_like(m_scr)

    mask, mf = _scan_masks(d)
    g, gt = _direction_gates(d, g_ref, gt_ref)
    g = g + gb_ref[0]
    gt = gt + gbt_ref[0]
    lf = -_softplus(-g)
    lft = -_softplus(-gt)
    cum, cumt = _masked_cumsum(mf, lf, lft)
    tot = jnp.sum(lf, axis=0, keepdims=True)
    cos = cos_ref[...]
    sin = sin_ref[...]

    for h in range(ML_H):
        hs = slice(h * ML_DH, (h + 1) * ML_DH)
        q = _rope(q_ref[:, hs], cos, sin)
        k = _rope(k_ref[:, hs], cos, sin) * (ML_DH ** -0.5)
        v = v_ref[:, hs]
        qb, kb, vb = q.astype(BF16), k.astype(BF16), v.astype(BF16)
        bt_col = cum[:, G_F + h:G_F + h + 1]
        bt_row = cumt[G_F + h:G_F + h + 1, :]
        ig_col = g[:, G_I + h:G_I + h + 1]
        ig_row = gt[G_I + h:G_I + h + 1, :]
        b_last = tot[:, G_F + h:G_F + h + 1]
        m_prev = m_scr[h]
        cmat = c_scr[h]
        nvec = n_scr[h]

        log_d = jnp.where(mask, bt_col - bt_row + ig_row, -jnp.inf)
        inter = bt_col + m_prev
        m_t = jnp.maximum(jnp.max(log_d, axis=-1, keepdims=True), inter)
        wts = _dot_nt(qb, kb) * jnp.exp(log_d - m_t)
        sc = jnp.exp(inter - m_t)
        num = _dot(wts.astype(BF16), vb) + _dot_nt(qb, cmat.astype(BF16)) * sc
        den = jnp.sum(wts, axis=-1, keepdims=True) + jnp.sum(q * nvec, axis=-1, keepdims=True) * sc
        den = jnp.maximum(jnp.abs(den), jnp.exp(-m_t))
        o_ref[0, :, hs] = num / den

        tail = b_last - bt_col + ig_col
        m_new = jnp.maximum(b_last + m_prev, jnp.max(tail, axis=0, keepdims=True))
        wgt = jnp.exp(tail - m_new)
        decay = jnp.exp(b_last + m_prev - m_new)
        c_scr[h] = decay * cmat + _dot_tn((v * wgt).astype(BF16), kb)
        n_scr[h] = decay * nvec + jnp.sum(wgt * k, axis=0, keepdims=True)
        m_scr[h] = m_new


def mlstm_scan(rest, gates, gates_t, gbias, gbias_t, cos_tab, sin_tab):
    t_all = rest.shape[0]
    n_chunks = t_all // CHUNK
    cm = lambda d, s: _scan_chunk(d, s, n_chunks)
    col = lambda off: off // ML_W
    return pl.pallas_call(
        _mlstm_kernel,
        grid=(2, n_chunks),
        in_specs=[pl.BlockSpec((CHUNK, ML_W), lambda d, s: (cm(d, s), col(R_MQ))),
                  pl.BlockSpec((CHUNK, ML_W), lambda d, s: (cm(d, s), col(R_MK))),
                  pl.BlockSpec((CHUNK, ML_W), lambda d, s: (cm(d, s), col(R_MV))),
                  pl.BlockSpec((CHUNK, LANES), lambda d, s: (cm(d, s), 0)),
                  pl.BlockSpec((LANES, CHUNK), lambda d, s: (0, cm(d, s))),
                  pl.BlockSpec((1, 1, LANES), lambda d, s: (d, 0, 0)),
                  pl.BlockSpec((1, LANES, 1), lambda d, s: (d, 0, 0)),
                  pl.BlockSpec((CHUNK, ML_DH), lambda d, s: (cm(d, s), 0)),
                  pl.BlockSpec((CHUNK, ML_DH), lambda d, s: (cm(d, s), 0))],
        out_specs=pl.BlockSpec((1, CHUNK, ML_W), lambda d, s: (d, cm(d, s), 0)),
        out_shape=jax.ShapeDtypeStruct((2, t_all, ML_W), F32),
        scratch_shapes=[pltpu.VMEM((ML_H, ML_DH, ML_DH), F32),
                        pltpu.VMEM((ML_H, 1, ML_DH), F32),
                        pltpu.VMEM((ML_H, 1, 1), F32)],
        compiler_params=_cparams(("arbitrary", "arbitrary")),
        name="mlstm_scan",
    )(rest, rest, rest, gates, gates_t, gbias, gbias_t, cos_tab, sin_tab)


CONV_HALO = 8


def _conv_kernel(x_ref, p_ref, n_ref, w_ref, b_ref, o_ref, *, n_chunks):
    s = pl.program_id(0)
    x = x_ref[...]
    row = lax.broadcasted_iota(jnp.int32, x.shape, 0)
    prev = jnp.where(s >= 2, p_ref[...], 0.0)
    nxt = jnp.where((s >= 1) & (s <= n_chunks - 2), n_ref[...], 0.0)
    xm1 = jnp.where(row == 0, prev[CONV_HALO - 1:CONV_HALO, :], pltpu.roll(x, 1, 0))
    xm2 = jnp.where(row == 0, prev[CONV_HALO - 2:CONV_HALO - 1, :],
                    jnp.where(row == 1, prev[CONV_HALO - 1:CONV_HALO, :], pltpu.roll(x, 2, 0)))
    xp1 = jnp.where(row == CHUNK - 1, nxt[0:1, :], pltpu.roll(x, CHUNK - 1, 0))
    w = w_ref[0]
    y = w[0:1, :] * xm2 + w[1:2, :] * xm1 + w[2:3, :] * x + w[3:4, :] * xp1 + b_ref[0]
    o_ref[...] = _silu(y)


def ssd_conv(rest, conv_w, conv_b, layer):
    t_all = rest.shape[0]
    n_chunks = t_all // CHUNK
    hb = CHUNK // CONV_HALO
    return pl.pallas_call(
        functools.partial(_conv_kernel, n_chunks=n_chunks),
        grid=(n_chunks,),
        in_specs=[pl.BlockSpec((CHUNK, CONV_CH), lambda s: (s, 0)),
                  pl.BlockSpec((CONV_HALO, CONV_CH), lambda s: (jnp.maximum(s * hb - 1, 0), 0)),
                  pl.BlockSpec((CONV_HALO, CONV_CH), lambda s: (jnp.minimum((s + 1) * hb, n_chunks * hb - 1), 0)),
                  pl.BlockSpec((1, 4, CONV_CH), lambda s: (layer, 0, 0)),
                  pl.BlockSpec((1, 1, CONV_CH), lambda s: (layer, 0, 0))],
        out_specs=pl.BlockSpec((CHUNK, CONV_CH), lambda s: (s, 0)),
        out_shape=jax.ShapeDtypeStruct((t_all, CONV_CH), F32),
        compiler_params=_cparams(("parallel",)),
        name="ssd_conv",
    )(rest, rest, rest, conv_w, conv_b)


def _ssd_kernel(x_ref, g_ref, gt_ref, gb_ref, gbt_ref, al_ref, alt_ref, o_ref, s_scr):
    d = pl.program_id(0)
    s = pl.program_id(1)

    @pl.when(s == 0)
    def _():
        s_scr[...] = jnp.zeros_like(s_scr)

    mask, mf = _scan_masks(d)
    g, gt = _direction_gates(d, g_ref, gt_ref)
    dt = _softplus(g + gb_ref[0])
    dtt = _softplus(gt + gbt_ref[0])
    inc = dt * (-jnp.exp(al_ref[0]))
    inct = dtt * (-jnp.exp(alt_ref[0]))
    cum, cumt = _masked_cumsum(mf, inc, inct)
    tot = jnp.sum(inc, axis=0, keepdims=True)

    hpg = SSM_H // SSM_G
    for gi in range(SSM_G):
        bm = x_ref[:, SSM_W + gi * SSM_N:SSM_W + (gi + 1) * SSM_N].astype(BF16)
        cm = x_ref[:, SSM_W + (SSM_G + gi) * SSM_N:SSM_W + (SSM_G + gi + 1) * SSM_N].astype(BF16)
        gmat = _dot_nt(cm, bm)
        for hh in range(hpg):
            h = gi * hpg + hh
            la_col = cum[:, G_DT + h:G_DT + h + 1]
            la_row = cumt[G_DT + h:G_DT + h + 1, :]
            la_last = tot[:, G_DT + h:G_DT + h + 1]
            dt_col = dt[:, G_DT + h:G_DT + h + 1]
            state = s_scr[h]
            decay = jnp.exp(jnp.where(mask, la_col - la_row, -jnp.inf))
            xdt = x_ref[:, h * SSM_P:(h + 1) * SSM_P] * dt_col
            y = (_dot((gmat * decay).astype(BF16), xdt.astype(BF16))
                 + _dot_nt(cm, state.astype(BF16)) * jnp.exp(la_col))
            o_ref[0, :, h * SSM_P:(h + 1) * SSM_P] = y
            tail = jnp.exp(la_last - la_col)
            s_scr[h] = state * jnp.exp(la_last) + _dot_tn((xdt * tail).astype(BF16), bm)


def ssd_scan(xbc_act, gates, gates_t, gbias, gbias_t, alog, alog_t):
    t_all = xbc_act.shape[0]
    n_chunks = t_all // CHUNK
    cm = lambda d, s: _scan_chunk(d, s, n_chunks)
    return pl.pallas_call(
        _ssd_kernel,
        grid=(2, n_chunks),
        in_specs=[pl.BlockSpec((CHUNK, CONV_CH), lambda d, s: (cm(d, s), 0)),
                  pl.BlockSpec((CHUNK, LANES), lambda d, s: (cm(d, s), 0)),
                  pl.BlockSpec((LANES, CHUNK), lambda d, s: (0, cm(d, s))),
                  pl.BlockSpec((1, 1, LANES), lambda d, s: (d, 0, 0)),
                  pl.BlockSpec((1, LANES, 1), lambda d, s: (d, 0, 0)),
                  pl.BlockSpec((1, 1, LANES), lambda d, s: (d, 0, 0)),
                  pl.BlockSpec((1, LANES, 1), lambda d, s: (d, 0, 0))],
        out_specs=pl.BlockSpec((1, CHUNK, SSM_W), lambda d, s: (d, cm(d, s), 0)),
        out_shape=jax.ShapeDtypeStruct((2, t_all, SSM_W), F32),
        scratch_shapes=[pltpu.VMEM((SSM_H, SSM_P, SSM_N), F32)],
        compiler_params=_cparams(("arbitrary", "arbitrary")),
        name="ssd_scan",
    )(xbc_act, gates, gates_t, gbias, gbias_t, alog, alog_t)


OUT_TM = 640
OUT_TN = 512
OUT_SUB = 128


def _out_proj_kernel(ona_ref, hs_ref, mo_ref, ys_ref, xs_ref, z_ref, mlw_ref, dsk_ref, ssw_ref,
                     w_ref, x_ref, g_ref, o_ref, a_scr):
    i = pl.program_id(0)
    j = pl.program_id(1)

    @pl.when(j == 0)
    def _():
        def body(r, carry):
            rows = pl.ds(pl.multiple_of(r * OUT_SUB, OUT_SUB), OUT_SUB)
            a_scr[rows, 0:NA_W] = ona_ref[rows, :]
            hsum = hs_ref[0, rows, :] + hs_ref[1, rows, :]
            gate = 1.0 / (1.0 + jnp.exp(-mo_ref[rows, :]))
            for h in range(ML_H):
                cs = slice(h * ML_DH, (h + 1) * ML_DH)
                hh = hsum[:, cs]
                mu = jnp.mean(hh, axis=-1, keepdims=True)
                var = jnp.mean(jnp.square(hh - mu), axis=-1, keepdims=True)
                hn = (hh - mu) * lax.rsqrt(var + EPS) * mlw_ref[0, :, cs]
                a_scr[rows, NA_W + h * ML_DH:NA_W + (h + 1) * ML_DH] = (gate[:, cs] * hn).astype(BF16)
            y = ys_ref[0, rows, :] + ys_ref[1, rows, :] + dsk_ref[0] * xs_ref[rows, :]
            y = y * _silu(z_ref[rows, :])
            gw = SSM_W // SSM_G
            for gi in range(SSM_G):
                cs = slice(gi * gw, (gi + 1) * gw)
                yg = y[:, cs]
                yn = yg * lax.rsqrt(jnp.mean(yg * yg, axis=-1, keepdims=True) + EPS) * ssw_ref[0, :, cs]
                a_scr[rows, NA_W + ML_W + gi * gw:NA_W + ML_W + (gi + 1) * gw] = yn.astype(BF16)
            return carry
        lax.fori_loop(0, OUT_TM // OUT_SUB, body, 0)

    acc = _dot(a_scr[...], w_ref[0])
    is_ctx = (i * OUT_TM + lax.broadcasted_iota(jnp.int32, (OUT_TM, 1), 0)) < CTX
    gate1 = jnp.where(is_ctx, g_ref[1:2, :], g_ref[0:1, :])
    o_ref[...] = x_ref[...] + gate1 * acc


def out_projection(o_na, hs, rest, ys, xbc_act, ml_norm_w, d_skip_vec, ssm_norm_w, w_out, xa, g1, layer):
    t_all, d = xa.shape
    cw = lambda off: off // ML_W
    vec = pl.BlockSpec((1, 1, ML_W), lambda i, j: (layer, 0, 0))
    return pl.pallas_call(
        _out_proj_kernel,
        grid=(t_all // OUT_TM, d // OUT_TN),
        in_specs=[pl.BlockSpec((OUT_TM, NA_W), lambda i, j: (i, 0)),
                  pl.BlockSpec((2, OUT_TM, ML_W), lambda i, j: (0, i, 0)),
                  pl.BlockSpec((OUT_TM, ML_W), lambda i, j: (i, cw(R_MO))),
                  pl.BlockSpec((2, OUT_TM, SSM_W), lambda i, j: (0, i, 0)),
                  pl.BlockSpec((OUT_TM, SSM_W), lambda i, j: (i, 0)),
                  pl.BlockSpec((OUT_TM, SSM_W), lambda i, j: (i, cw(R_Z))),
                  vec, vec, vec,
                  pl.BlockSpec((1, d, OUT_TN), lambda i, j: (layer, 0, j)),
                  pl.BlockSpec((OUT_TM, OUT_TN), lambda i, j: (i, j)),
                  pl.BlockSpec((2, OUT_TN), lambda i, j: (0, j))],
        out_specs=pl.BlockSpec((OUT_TM, OUT_TN), lambda i, j: (i, j)),
        out_shape=jax.ShapeDtypeStruct((t_all, d), F32),
        scratch_shapes=[pltpu.VMEM((OUT_TM, d), BF16)],
        compiler_params=_cparams(("arbitrary", "arbitrary")),
        name="out_projection",
    )(o_na, hs, rest, ys, xbc_act, rest, ml_norm_w, d_skip_vec, ssm_norm_w, w_out, xa, g1)


RT_TM = 640
RT_SUB = 128


def _router_kernel(x_ref, nw_ref, sc_ref, sh_ref, wr_ref, br_ref, h_ref, r_ref, cnt_ref, run_scr):
    i = pl.program_id(0)

    @pl.when(i == 0)
    def _():
        run_scr[...] = jnp.zeros_like(run_scr)

    t_row = lax.broadcasted_iota(jnp.int32, (RT_SUB, RT_SUB), 0)
    t_col = lax.broadcasted_iota(jnp.int32, (RT_SUB, RT_SUB), 1)
    earlier = (t_col < t_row).astype(BF16)
    wr_hi, wr_lo = _split2(wr_ref[0])

    def body(r, run):
        rows = pl.ds(pl.multiple_of(r * RT_SUB, RT_SUB), RT_SUB)
        h = _modulated_norm(x_ref[rows, :], nw_ref[...], sc_ref[...], sh_ref[...], i * RT_TM + r * RT_SUB)
        h_ref[rows, :] = h
        logit = _dot_split(h, wr_hi, wr_lo) + br_ref[0]
        lane = lax.broadcasted_iota(jnp.int32, logit.shape, 1)
        big = jnp.int32(LANES)
        is_g = lane < MOE_GROUPS
        lg = jnp.where(is_g, logit, -jnp.inf)
        gmax = jnp.max(lg, axis=-1, keepdims=True)
        g_sel = jnp.min(jnp.where(is_g & (lg == gmax), lane, big), axis=-1, keepdims=True)
        g_w = 1.0 / jnp.sum(jnp.exp(lg - gmax), axis=-1, keepdims=True)
        lo = MOE_GROUPS + g_sel * MOE_PER_GROUP
        in_g = (lane >= lo) & (lane < lo + MOE_PER_GROUP)
        le = jnp.where(in_g, logit, -jnp.inf)
        v1 = jnp.max(le, axis=-1, keepdims=True)
        i1 = jnp.min(jnp.where(in_g & (le == v1), lane, big), axis=-1, keepdims=True)
        le2 = jnp.where(lane == i1, -jnp.inf, le)
        v2 = jnp.max(le2, axis=-1, keepdims=True)
        i2 = jnp.min(jnp.where(in_g & (lane != i1) & (le2 == v2), lane, big), axis=-1, keepdims=True)
        e2 = jnp.exp(v2 - v1)
        w1 = g_w / (1.0 + e2)
        w2 = g_w * e2 / (1.0 + e2)
        oh1 = (lane == i1 - MOE_GROUPS).astype(F32)
        oh2 = (lane == i2 - MOE_GROUPS).astype(F32)
        oh = oh1 + oh2
        before = _dot(earlier, oh.astype(BF16)) + run
        rank1 = jnp.sum(before * oh1, axis=-1, keepdims=True)
        rank2 = jnp.sum(before * oh2, axis=-1, keepdims=True)
        out = jnp.where(lane == 0, (i1 - MOE_GROUPS).astype(F32),
                        jnp.where(lane == 1, (i2 - MOE_GROUPS).astype(F32),
                                  jnp.where(lane == 2, w1,
                                            jnp.where(lane == 3, w2,
                                                      jnp.where(lane == 4, rank1,
                                                                jnp.where(lane == 5, rank2, 0.0))))))
        r_ref[rows, :] = out
        return run + jnp.sum(oh, axis=0, keepdims=True)
    run = lax.fori_loop(0, RT_TM // RT_SUB, body, run_scr[...])
    run_scr[...] = run
    cnt_ref[...] = jnp.broadcast_to(run, cnt_ref.shape)


def moe_router(xa, nw, sc2, sh2, w_route, b_route, layer):
    t_all, d = xa.shape
    return pl.pallas_call(
        _router_kernel,
        grid=(t_all // RT_TM,),
        in_specs=[pl.BlockSpec((RT_TM, d), lambda i: (i, 0)),
                  pl.BlockSpec((1, d), lambda i: (0, 0)),
                  pl.BlockSpec((2, d), lambda i: (0, 0)),
                  pl.BlockSpec((2, d), lambda i: (0, 0)),
                  pl.BlockSpec((1, d, LANES), lambda i: (layer, 0, 0)),
                  pl.BlockSpec((1, 1, LANES), lambda i: (layer, 0, 0))],
        out_specs=[pl.BlockSpec((RT_TM, d), lambda i: (i, 0)),
                   pl.BlockSpec((RT_TM, LANES), lambda i: (i, 0)),
                   pl.BlockSpec((8, LANES), lambda i: (0, 0))],
        out_shape=[jax.ShapeDtypeStruct((t_all, d), F32),
                   jax.ShapeDtypeStruct((t_all, LANES), F32),
                   jax.ShapeDtypeStruct((8, LANES), F32)],
        scratch_shapes=[pltpu.VMEM((1, LANES), F32)],
        compiler_params=_cparams(("arbitrary",)),
        name="moe_router",
    )(xa, nw, sc2, sh2, w_route, b_route)


MOE_FF_SPLIT = 2


def _expert_kernel(be_ref, nu_ref, st_ref, h_hbm, w1_ref, w3_ref, w2_ref, o_ref, xbuf, xb_scr, sem):
    b = pl.program_id(0)
    part = pl.program_id(1)
    n_used = nu_ref[0]

    def row_copy(blk, slot, t):
        tok = st_ref[blk * MOE_BM + t]
        return pltpu.make_async_copy(h_hbm.at[pl.ds(tok, 1), :], xbuf.at[slot, pl.ds(t, 1), :], sem.at[slot])

    @pl.when((b == 0) & (part == 0))
    def _():
        def body(t, carry):
            row_copy(0, 0, t).start()
            return carry
        lax.fori_loop(0, MOE_BM, body, 0)

    @pl.when((part == MOE_FF_SPLIT - 1) & (b + 1 < n_used))
    def _():
        for t in range(MOE_BM):
            row_copy(b + 1, (b + 1) % 2, t).start()

    @pl.when(b < n_used)
    def _():
        @pl.when(part == 0)
        def _():
            slot = b % 2
            pltpu.make_async_copy(h_hbm.at[pl.ds(0, MOE_BM), :], xbuf.at[slot], sem.at[slot]).wait()
            xb_scr[...] = xbuf[slot].astype(BF16)

        x = xb_scr[...]
        hid = _silu(_dot(x, w1_ref[0, 0].astype(BF16))) * _dot(x, w3_ref[0, 0].astype(BF16))
        y = _dot(hid.astype(BF16), w2_ref[0, 0].astype(BF16))

        @pl.when(part == 0)
        def _():
            o_ref[...] = y

        @pl.when(part > 0)
        def _():
            o_ref[...] += y

    @pl.when(b >= n_used)
    def _():
        o_ref[...] = jnp.zeros_like(o_ref)


def expert_blocks(block_e, n_used, slot_tok, h2, w1, w3, w2, layer):
    d = h2.shape[1]
    ff = w1.shape[-1]
    fp = ff // MOE_FF_SPLIT
    n_slots = slot_tok.shape[0]
    n_blocks = n_slots // MOE_BM
    part_of = lambda b, p, nu: jnp.where(b < nu[0], p, MOE_FF_SPLIT - 1)
    return pl.pallas_call(
        _expert_kernel,
        grid_spec=pltpu.PrefetchScalarGridSpec(
            num_scalar_prefetch=3,
            grid=(n_blocks, MOE_FF_SPLIT),
            in_specs=[pl.BlockSpec(memory_space=pl.ANY),
                      pl.BlockSpec((1, 1, d, fp), lambda b, p, be, nu, st: (layer, be[b], 0, part_of(b, p, nu))),
                      pl.BlockSpec((1, 1, d, fp), lambda b, p, be, nu, st: (layer, be[b], 0, part_of(b, p, nu))),
                      pl.BlockSpec((1, 1, fp, d), lambda b, p, be, nu, st: (layer, be[b], part_of(b, p, nu), 0))],
            out_specs=pl.BlockSpec((MOE_BM, d), lambda b, p, be, nu, st: (b, 0)),
            scratch_shapes=[pltpu.VMEM((2, MOE_BM, d), F32),
                            pltpu.VMEM((MOE_BM, d), BF16),
                            pltpu.SemaphoreType.DMA((2,))]),
        out_shape=jax.ShapeDtypeStruct((n_slots, d), F32),
        compiler_params=_cparams(("arbitrary", "arbitrary")),
        name="moe_experts",
    )(block_e, n_used, slot_tok, h2, w1, w3, w2)


def _combine_kernel(dest_ref, x_ref, r_ref, g_ref, fw_ref, y_hbm, o_ref, ybuf, sem, *,
                    row_block0, n_tiles, final_norm):
    i = pl.program_id(0)

    def row_copy(tile, slot, t, k):
        row = dest_ref[(tile + row_block0) * (2 * CHUNK) + 2 * t + k]
        return pltpu.make_async_copy(y_hbm.at[pl.ds(row, 1), :],
                                     ybuf.at[slot, pl.ds(k * CHUNK + t, 1), :], sem.at[slot])

    @pl.when(i == 0)
    def _():
        def body(t, carry):
            row_copy(0, 0, t, 0).start()
            row_copy(0, 0, t, 1).start()
            return carry
        lax.fori_loop(0, CHUNK, body, 0)

    @pl.when(i + 1 < n_tiles)
    def _():
        for t in range(CHUNK):
            row_copy(i + 1, (i + 1) % 2, t, 0).start()
            row_copy(i + 1, (i + 1) % 2, t, 1).start()

    slot = i % 2
    pltpu.make_async_copy(y_hbm.at[pl.ds(0, 2 * CHUNK), :], ybuf.at[slot], sem.at[slot]).wait()

    rt = r_ref[...]
    y = rt[:, 2:3] * ybuf[slot, 0:CHUNK, :] + rt[:, 3:4] * ybuf[slot, CHUNK:2 * CHUNK, :]
    is_ctx = ((i + row_block0) * CHUNK + lax.broadcasted_iota(jnp.int32, (CHUNK, 1), 0)) < CTX
    gate2 = jnp.where(is_ctx, g_ref[1:2, :], g_ref[0:1, :])
    x = x_ref[...] + gate2 * y
    if final_norm:
        x = x * lax.rsqrt(jnp.mean(x * x, axis=-1, keepdims=True) + EPS) * fw_ref[...]
    o_ref[...] = x


def moe_combine(dest, xa, route, g2, final_w, y, final_norm):
    t_all, d = xa.shape
    rb0 = CTX // CHUNK if final_norm else 0
    n_out = t_all // CHUNK - rb0
    return pl.pallas_call(
        functools.partial(_combine_kernel, row_block0=rb0, n_tiles=n_out, final_norm=final_norm),
        grid_spec=pltpu.PrefetchScalarGridSpec(
            num_scalar_prefetch=1,
            grid=(n_out,),
            in_specs=[pl.BlockSpec((CHUNK, d), lambda i, ds: (i + rb0, 0)),
                      pl.BlockSpec((CHUNK, LANES), lambda i, ds: (i + rb0, 0)),
                      pl.BlockSpec((2, d), lambda i, ds: (0, 0)),
                      pl.BlockSpec((1, d), lambda i, ds: (0, 0)),
                      pl.BlockSpec(memory_space=pl.ANY)],
            out_specs=pl.BlockSpec((CHUNK, d), lambda i, ds: (i, 0)),
            scratch_shapes=[pltpu.VMEM((2, 2 * CHUNK, d), F32),
                            pltpu.SemaphoreType.DMA((2,))]),
        out_shape=jax.ShapeDtypeStruct((n_out * CHUNK, d), F32),
        compiler_params=_cparams(("arbitrary",)),
        name="moe_combine",
    )(dest, xa, route, g2, final_w, y)


def moe_dispatch(route, counts):
    n = route.shape[0]
    n_assign = 2 * n
    e_flat = route[:, 0:2].astype(jnp.int32).reshape(-1)
    rank_flat = route[:, 4:6].astype(jnp.int32).reshape(-1)
    padded = (counts + MOE_BM - 1) // MOE_BM * MOE_BM
    pends = jnp.cumsum(padded)
    pstarts = pends - padded
    experts = jnp.arange(N_EXPERTS, dtype=jnp.int32)
    dest = jnp.sum(jnp.where(e_flat[:, None] == experts[None, :], pstarts[None, :], 0), axis=1) + rank_flat
    n_blocks = -(-n_assign // MOE_BM) + N_EXPERTS
    n_slots = n_blocks * MOE_BM
    tok_flat = jnp.arange(n_assign, dtype=jnp.int32) // 2
    slot_tok = jnp.zeros((n_slots,), jnp.int32).at[dest].set(tok_flat)
    block_start = jnp.arange(n_blocks, dtype=jnp.int32) * MOE_BM
    block_e = jnp.minimum(jnp.sum((pends[None, :] <= block_start[:, None]).astype(jnp.int32), axis=1),
                          N_EXPERTS - 1)
    n_used = (pends[-1] // MOE_BM).astype(jnp.int32).reshape(1)
    return dest, slot_tok, block_e, n_used


def hier_moe(xa, nw, sc2, sh2, g2, w_route, b_route, w1, w3, w2, final_w, layer, final_norm):
    h2, route, cnt = moe_router(xa, nw, sc2, sh2, w_route, b_route, layer)
    dest, slot_tok, block_e, n_used = moe_dispatch(route, cnt[0, :N_EXPERTS].astype(jnp.int32))
    y = expert_blocks(block_e, n_used, slot_tok, h2, w1, w3, w2, layer)
    return moe_combine(dest, xa, route, g2, final_w, y, final_norm)


def rope_tables(t_lat):
    nf = ML_DH // 4
    inv = ROPE_THETA ** (-jnp.arange(nf, dtype=F32) / nf)
    t_idx = jnp.arange(t_lat)
    ang_r = (t_idx // GRID_W).astype(F32)[:, None] * inv
    ang_c = (t_idx % GRID_W).astype(F32)[:, None] * inv
    cos = jnp.concatenate([jnp.cos(ang_r)] * 2 + [jnp.cos(ang_c)] * 2, axis=-1)
    sin = jnp.concatenate([-jnp.sin(ang_r), jnp.sin(ang_r), -jnp.sin(ang_c), jnp.sin(ang_c)], axis=-1)
    cos = jnp.concatenate([jnp.ones((CTX, ML_DH), F32), cos], axis=0)
    sin = jnp.concatenate([jnp.zeros((CTX, ML_DH), F32), sin], axis=0)
    return cos, sin


def _pad_lanes(v):
    return jnp.pad(v, [(0, 0)] * (v.ndim - 1) + [(0, LANES - v.shape[-1])])


def kernel(x, c, ctx, c_ctx, w_mod, b_mod, norm1_w, norm2_w, w_in, w_out, na_rpb, ml_gate_b, ml_norm_w, ssm_conv_w, ssm_conv_b, ssm_dt_bias, ssm_a_log, ssm_d, ssm_norm_w, router_g_w, router_g_b, router_e_w, router_e_b, moe_w1, moe_w3, moe_w2, final_norm_w):
    depth = w_mod.shape[0]
    t_lat = x.shape[1]
    d = D_MODEL
    xa = jnp.concatenate([ctx[0], x[0]], axis=0)

    cond = jnp.zeros((8, d), F32).at[0].set(c[0]).at[1].set(c_ctx)
    mod = adaln_modulation(cond, w_mod, b_mod)[:, 0:2, :].reshape(depth, 2, 6, d)
    cos_tab, sin_tab = rope_tables(t_lat)

    o_mq = 3 * NA_W
    o_mg = o_mq + 4 * ML_W
    o_z = o_mg + 4 * ML_H
    o_xbc = o_z + SSM_W
    o_dt = o_xbc + CONV_CH
    w_main = jnp.concatenate([w_in[:, :, 0:o_mq], w_in[:, :, o_xbc:o_dt], w_in[:, :, o_mq:o_mg],
                              w_in[:, :, o_z:o_xbc]], axis=2).astype(BF16)
    w_mg, w_dt = w_in[:, :, o_mg:o_z], w_in[:, :, o_dt:]
    w_gate = _pad_lanes(jnp.concatenate([w_mg[:, :, 0:8], w_dt[:, :, 0:8], w_mg[:, :, 8:16], w_dt[:, :, 8:16]],
                                        axis=2))
    w_out_b = w_out.astype(BF16)
    w_route = _pad_lanes(jnp.concatenate([router_g_w, router_e_w], axis=2))
    b_route = _pad_lanes(jnp.concatenate([router_g_b, router_e_b], axis=1))[:, None, :]
    gb4 = ml_gate_b.reshape(depth, 2, 2 * ML_H)
    gbias = _pad_lanes(jnp.concatenate([gb4, ssm_dt_bias], axis=2))
    alog = _pad_lanes(jnp.concatenate([jnp.zeros_like(ssm_a_log), ssm_a_log], axis=2))
    d_skip_vec = jnp.repeat(ssm_d, SSM_P, axis=1)[:, None, :]
    conv_b = ssm_conv_b[:, None, :]

    out = None
    for l in range(depth):
        last = l == depth - 1
        sh1, sc1, g1, sh2, sc2, g2 = (mod[l, :, p, :] for p in range(6))

        qkv, rest, gates, gates_t = in_projection(xa, norm1_w[l][None], sc1, sh1, w_main, w_gate, l)
        o_na = neighbourhood_attention(qkv, na_bias_table(na_rpb[l]))
        hs = mlstm_scan(rest, gates, gates_t, gbias[l][:, None, :], gbias[l][:, :, None], cos_tab, sin_tab)
        xbc_act = ssd_conv(rest, ssm_conv_w, conv_b, l)
        ys = ssd_scan(xbc_act, gates, gates_t, gbias[l][:, None, :], gbias[l][:, :, None],
                      alog[l][:, None, :], alog[l][:, :, None])
        xa = out_projection(o_na, hs, rest, ys, xbc_act, ml_norm_w[:, None, :], d_skip_vec,
                            ssm_norm_w[:, None, :], w_out_b, xa, g1, l)
        res = hier_moe(xa, norm2_w[l][None], sc2, sh2, g2, w_route, b_route, moe_w1, moe_w3, moe_w2,
                       final_norm_w[None], l, last)
        if last:
            out = res
        else:
            xa = res
    return out[None]
```

```python
import functools

import jax
import jax.numpy as jnp
import numpy as np
from jax import lax
from jax.experimental import pallas as pl
from jax.experimental.pallas import tpu as pltpu

F32 = jnp.float32
BF16 = jnp.bfloat16
HIGHEST = lax.Precision.HIGHEST

D_MODEL = 2048
GRID_W = 64
CTX = 256
NA_DH = 128
NA_W = 1024
NA_H = 8
NA_KR = 8
NA_KC = 16
ML_DH = 128
ML_W = 512
ML_H = 4
SSM_P = 64
SSM_W = 512
SSM_H = 8
SSM_G = 2
SSM_N = 128
CONV_CH = SSM_W + 2 * SSM_G * SSM_N
MOE_GROUPS = 4
MOE_PER_GROUP = 8
N_EXPERTS = 32
MOE_FF = 1024
ROPE_THETA = 10000.0
EPS = 1e-6

CHUNK = 256
LANES = 128
NEG = -1e30
MOE_BM = 256
VMEM_LIMIT = 56 * 1024 * 1024

R_XBC, R_MQ, R_MK, R_MV, R_MO, R_Z = 0, 1024, 1536, 2048, 2560, 3072
REST_W = 3584
QKV_W = 3 * NA_W
G_I, G_F, G_DT = 0, 4, 8
G_DIR = 16


def _cparams(sem):
    return pltpu.CompilerParams(dimension_semantics=sem, vmem_limit_bytes=VMEM_LIMIT)


def _silu(x):
    return x / (1.0 + jnp.exp(-x))


def _softplus(x):
    return jnp.maximum(x, 0.0) + jnp.log1p(jnp.exp(-jnp.abs(x)))


def _dot(a, b):
    return jnp.dot(a, b, preferred_element_type=F32)


def _dot_nt(a, b):
    return lax.dot_general(a, b, (((1,), (1,)), ((), ())), preferred_element_type=F32)


def _dot_tn(a, b):
    return lax.dot_general(a, b, (((0,), (0,)), ((), ())), preferred_element_type=F32)


def _split2(x):
    hi = x.astype(BF16)
    return hi, (x - hi.astype(F32)).astype(BF16)


def _dot_split(a, w_hi, w_lo):
    a_hi, a_lo = _split2(a)
    return _dot(a_hi, w_hi) + _dot(a_hi, w_lo) + _dot(a_lo, w_hi)


def _mod_kernel(c_ref, w_ref, b_ref, o_ref):
    o_ref[0] = jnp.dot(_silu(c_ref[...]), w_ref[0], preferred_element_type=F32, precision=HIGHEST) + b_ref[0]


def adaln_modulation(cond, w_mod, b_mod):
    depth, d, n = w_mod.shape
    tn = 512
    return pl.pallas_call(
        _mod_kernel,
        grid=(depth, n // tn),
        in_specs=[pl.BlockSpec((8, d), lambda l, j: (0, 0)),
                  pl.BlockSpec((1, d, tn), lambda l, j: (l, 0, j)),
                  pl.BlockSpec((1, 1, tn), lambda l, j: (l, 0, j))],
        out_specs=pl.BlockSpec((1, 8, tn), lambda l, j: (l, 0, j)),
        out_shape=jax.ShapeDtypeStruct((depth, 8, n), F32),
        compiler_params=_cparams(("parallel", "parallel")),
        name="adaln_modulation",
    )(cond, w_mod, b_mod.reshape(depth, 1, n))


def _modulated_norm(x, nw, sc2, sh2, row0):
    r = x.shape[0]
    y = x * lax.rsqrt(jnp.mean(x * x, axis=-1, keepdims=True) + EPS) * nw
    is_ctx = (row0 + lax.broadcasted_iota(jnp.int32, (r, 1), 0)) < CTX
    sc = jnp.where(is_ctx, sc2[1:2, :], sc2[0:1, :])
    sh = jnp.where(is_ctx, sh2[1:2, :], sh2[0:1, :])
    return y * (1.0 + sc) + sh


IN_TM = 1280
IN_TN = 512
IN_SUB = 256


def _in_proj_kernel(x_ref, nw_ref, sc_ref, sh_ref, wm_ref, wg_ref, qkv_ref, rest_ref, gate_ref, gate_t_ref, a_scr):
    i = pl.program_id(0)
    j = pl.program_id(1)
    n_qkv = QKV_W // IN_TN

    @pl.when(j == 0)
    def _():
        wg_hi, wg_lo = _split2(wg_ref[0])
        for r in range(IN_TM // IN_SUB):
            rows = slice(r * IN_SUB, (r + 1) * IN_SUB)
            h = _modulated_norm(x_ref[rows, :], nw_ref[...], sc_ref[...], sh_ref[...], i * IN_TM + r * IN_SUB)
            a_scr[rows, :] = h.astype(BF16)
            gate = _dot_split(h, wg_hi, wg_lo)
            gate_ref[rows, :] = gate
            gate_t_ref[:, rows] = gate.T

    acc = _dot(a_scr[...], wm_ref[0])

    @pl.when(j < n_qkv)
    def _():
        qkv_ref[...] = acc.astype(BF16)

    @pl.when(j >= n_qkv)
    def _():
        rest_ref[...] = acc


def in_projection(xa, nw, sc2, sh2, w_main, w_gate, layer):
    t_all, d = xa.shape
    n_main = w_main.shape[2]
    n_qkv = QKV_W // IN_TN
    return pl.pallas_call(
        _in_proj_kernel,
        grid=(t_all // IN_TM, n_main // IN_TN),
        in_specs=[pl.BlockSpec((IN_TM, d), lambda i, j: (i, 0)),
                  pl.BlockSpec((1, d), lambda i, j: (0, 0)),
                  pl.BlockSpec((2, d), lambda i, j: (0, 0)),
                  pl.BlockSpec((2, d), lambda i, j: (0, 0)),
                  pl.BlockSpec((1, d, IN_TN), lambda i, j: (layer, 0, j)),
                  pl.BlockSpec((1, d, LANES), lambda i, j: (layer, 0, 0))],
        out_specs=[pl.BlockSpec((IN_TM, IN_TN), lambda i, j: (i, jnp.minimum(j, n_qkv - 1))),
                   pl.BlockSpec((IN_TM, IN_TN), lambda i, j: (i, jnp.maximum(j - n_qkv, 0))),
                   pl.BlockSpec((IN_TM, LANES), lambda i, j: (i, 0)),
                   pl.BlockSpec((LANES, IN_TM), lambda i, j: (0, i))],
        out_shape=[jax.ShapeDtypeStruct((t_all, QKV_W), BF16),
                   jax.ShapeDtypeStruct((t_all, REST_W), F32),
                   jax.ShapeDtypeStruct((t_all, LANES), F32),
                   jax.ShapeDtypeStruct((LANES, t_all), F32)],
        scratch_shapes=[pltpu.VMEM((IN_TM, d), BF16)],
        compiler_params=_cparams(("arbitrary", "arbitrary")),
        name="in_projection",
    )(xa, nw, sc2, sh2, w_main, w_gate)


NA_RB = CHUNK // GRID_W
NA_UR = 12


NA_HB = 2


def _na_kernel(q_ref, k_ref, v_ref, bias_ref, o_ref, *, n_rows):
    rb = pl.program_id(1)
    scale = NA_DH ** -0.5

    @pl.when(rb == 0)
    def _():
        for hh in range(NA_HB):
            hs = slice(hh * NA_DH, (hh + 1) * NA_DH)
            s = _dot_nt(q_ref[:, hs], k_ref[0:CTX, hs]) * scale
            m = jnp.max(s, axis=-1, keepdims=True)
            p = jnp.exp(s - m)
            l = jnp.sum(p, axis=-1, keepdims=True)
            o_ref[:, hs] = (_dot(p.astype(BF16), v_ref[0:CTX, hs]) / l).astype(o_ref.dtype)

    @pl.when(rb > 0)
    def _():
        r0 = (rb - 1) * NA_RB
        u0 = jnp.clip(r0 - NA_KR // 2, 0, n_rows - NA_UR)
        case = jnp.where(r0 == 0, 1, jnp.where(r0 == n_rows - NA_RB, 2, 0))
        start = pl.multiple_of(CTX + u0 * GRID_W, GRID_W)
        for hh in range(NA_HB):
            hs = slice(hh * NA_DH, (hh + 1) * NA_DH)
            q = q_ref[:, hs]
            kc = k_ref[0:CTX, hs]
            vc = v_ref[0:CTX, hs]
            kw = k_ref[pl.ds(start, NA_UR * GRID_W), hs]
            vw = v_ref[pl.ds(start, NA_UR * GRID_W), hs]
            s = _dot_nt(q, kw) * scale + bias_ref[hh, case]
            sc = _dot_nt(q, kc) * scale
            m = jnp.maximum(jnp.max(s, axis=-1, keepdims=True), jnp.max(sc, axis=-1, keepdims=True))
            p = jnp.exp(s - m)
            pc = jnp.exp(sc - m)
            l = jnp.sum(p, axis=-1, keepdims=True) + jnp.sum(pc, axis=-1, keepdims=True)
            o = _dot(p.astype(BF16), vw) + _dot(pc.astype(BF16), vc)
            o_ref[:, hs] = (o / l).astype(o_ref.dtype)


def na_bias_table(rpb):
    h = rpb.shape[0]
    c = np.arange(GRID_W)
    cs = np.clip(c - NA_KC // 2, 0, GRID_W - NA_KC)
    kcol = np.arange(GRID_W)
    inside = (kcol[None, :] >= cs[:, None]) & (kcol[None, :] < cs[:, None] + NA_KC)
    off = kcol[None, :] - c[:, None] + NA_KC - 1
    onehot = (off[None] == np.arange(2 * NA_KC - 1)[:, None, None]) & inside[None]
    band = jnp.einsum('hro,ock->hrck', rpb.astype(F32), jnp.asarray(onehot, F32), precision=HIGHEST)
    band = band + jnp.asarray(np.where(inside, 0.0, NEG), F32)
    neg = jnp.full((h, GRID_W, GRID_W), NEG, F32)
    cases = ([(rr, NA_KR // 2 - 1) for rr in range(NA_RB)],
             [(0, NA_KR - 1 - rr) for rr in range(NA_RB)],
             [(NA_UR - NA_KR, NA_KR // 2 - 1 - rr) for rr in range(NA_RB)])
    tabs = []
    for case in cases:
        rows = []
        for w_off, d0 in case:
            rows.append(jnp.concatenate(
                [band[:, d0 + u - w_off] if 0 <= u - w_off < NA_KR else neg for u in range(NA_UR)], axis=-1))
        tabs.append(jnp.concatenate(rows, axis=1))
    return jnp.stack(tabs, axis=1)


def neighbourhood_attention(qkv, bias_tab):
    t_all = qkv.shape[0]
    n_rows = (t_all - CTX) // GRID_W
    n_rb = t_all // CHUNK
    return pl.pallas_call(
        functools.partial(_na_kernel, n_rows=n_rows),
        grid=(NA_H // NA_HB, n_rb),
        in_specs=[pl.BlockSpec((CHUNK, NA_HB * NA_DH), lambda h, rb: (rb, h)),
                  pl.BlockSpec((t_all, NA_HB * NA_DH), lambda h, rb: (0, NA_H // NA_HB + h)),
                  pl.BlockSpec((t_all, NA_HB * NA_DH), lambda h, rb: (0, 2 * (NA_H // NA_HB) + h)),
                  pl.BlockSpec((NA_HB, 3, CHUNK, NA_UR * GRID_W), lambda h, rb: (h, 0, 0, 0))],
        out_specs=pl.BlockSpec((CHUNK, NA_HB * NA_DH), lambda h, rb: (rb, h)),
        out_shape=jax.ShapeDtypeStruct((t_all, NA_W), BF16),
        compiler_params=_cparams(("parallel", "parallel")),
        name="neighbourhood_attention",
    )(qkv, qkv, qkv, bias_tab)


def _scan_chunk(d, s, n_chunks):
    return jnp.where(d == 0, s, jnp.where(s == 0, 0, n_chunks - s))


def _scan_masks(d):
    row = lax.broadcasted_iota(jnp.int32, (CHUNK, CHUNK), 0)
    col = lax.broadcasted_iota(jnp.int32, (CHUNK, CHUNK), 1)
    mask = jnp.where(d == 0, row - col, col - row) >= 0
    return mask, mask.astype(BF16)


def _split3(x):
    hi = x.astype(BF16)
    r = x - hi.astype(F32)
    mid = r.astype(BF16)
    return hi, mid, (r - mid.astype(F32)).astype(BF16)


def _masked_cumsum(mb, x, xt):
    cum = sum(_dot(mb, p) for p in _split3(x))
    cumt = sum(_dot_nt(p, mb) for p in _split3(xt))
    return cum, cumt


def _direction_gates(d, g_ref, gt_ref):
    g = g_ref[...]
    g = jnp.where(d == 0, g, pltpu.roll(g, LANES - G_DIR, 1))
    gt = jnp.where(d == 0, gt_ref[0:G_DIR, :], gt_ref[G_DIR:2 * G_DIR, :])
    return g, gt


def _rope(x, cos, sin_signed):
    return x * cos + pltpu.roll(x, ML_DH // 2, 1) * sin_signed


def _mlstm_kernel(q_ref, k_ref, v_ref, g_ref, gt_ref, gb_ref, gbt_ref, cos_ref, sin_ref, o_ref,
                  c_scr, n_scr, m_scr):
    d = pl.program_id(0)
    s = pl.program_id(1)

    @pl.when(s == 0)
    def _():
        c_scr[...] = jnp.zeros_like(c_scr)
        n_scr[...] = jnp.zeros_like(n_scr)
        m_scr[...] = jnp.zeros_like(m_scr)

    mask, mf = _scan_masks(d)
    g, gt = _direction_gates(d, g_ref, gt_ref)
    g = g + gb_ref[0]
    gt = gt + gbt_ref[0, 0:G_DIR]
    lf = -_softplus(-g)
    lft = -_softplus(-gt)
    cum, cumt = _masked_cumsum(mf, lf, lft)
    tot = jnp.sum(lf, axis=0, keepdims=True)
    cos = cos_ref[...]
    sin = sin_ref[...]

    for h in range(ML_H):
        hs = slice(h * ML_DH, (h + 1) * ML_DH)
        q = _rope(q_ref[:, hs], cos, sin)
        k = _rope(k_ref[:, hs], cos, sin) * (ML_DH ** -0.5)
        v = v_ref[:, hs]
        qb, kb, vb = q.astype(BF16), k.astype(BF16), v.astype(BF16)
        bt_col = cum[:, G_F + h:G_F + h + 1]
        bt_row = cumt[G_F + h:G_F + h + 1, :]
        ig_col = g[:, G_I + h:G_I + h + 1]
        ig_row = gt[G_I + h:G_I + h + 1, :]
        b_last = tot[:, G_F + h:G_F + h + 1]
        m_prev = m_scr[h]
        cmat = c_scr[h]
        nvec = n_scr[h]

        log_d = jnp.where(mask, bt_col - bt_row + ig_row, -jnp.inf)
        inter = bt_col + m_prev
        m_t = jnp.maximum(jnp.max(log_d, axis=-1, keepdims=True), inter)
        wts = _dot_nt(qb, kb) * jnp.exp(log_d - m_t)
        sc = jnp.exp(inter - m_t)
        num = _dot(wts.astype(BF16), vb) + _dot_nt(qb, cmat.astype(BF16)) * sc
        den = jnp.sum(wts, axis=-1, keepdims=True) + jnp.sum(q * nvec, axis=-1, keepdims=True) * sc
        den = jnp.maximum(jnp.abs(den), jnp.exp(-m_t))
        o_ref[0, :, hs] = num / den

        tail = b_last - bt_col + ig_col
        m_new = jnp.maximum(b_last + m_prev, jnp.max(tail, axis=0, keepdims=True))
        wgt = jnp.exp(tail - m_new)
        decay = jnp.exp(b_last + m_prev - m_new)
        c_scr[h] = decay * cmat + _dot_tn((v * wgt).astype(BF16), kb)
        n_scr[h] = decay * nvec + jnp.sum(wgt * k, axis=0, keepdims=True)
        m_scr[h] = m_new


def mlstm_scan(rest, gates, gates_t, gbias, gbias_t, cos_tab, sin_tab):
    t_all = rest.shape[0]
    n_chunks = t_all // CHUNK
    cm = lambda d, s: _scan_chunk(d, s, n_chunks)
    col = lambda off: off // ML_W
    return pl.pallas_call(
        _mlstm_kernel,
        grid=(2, n_chunks),
        in_specs=[pl.BlockSpec((CHUNK, ML_W), lambda d, s: (cm(d, s), col(R_MQ))),
                  pl.BlockSpec((CHUNK, ML_W), lambda d, s: (cm(d, s), col(R_MK))),
                  pl.BlockSpec((CHUNK, ML_W), lambda d, s: (cm(d, s), col(R_MV))),
                  pl.BlockSpec((CHUNK, LANES), lambda d, s: (cm(d, s), 0)),
                  pl.BlockSpec((LANES, CHUNK), lambda d, s: (0, cm(d, s))),
                  pl.BlockSpec((1, 1, LANES), lambda d, s: (d, 0, 0)),
                  pl.BlockSpec((1, LANES, 1), lambda d, s: (d, 0, 0)),
                  pl.BlockSpec((CHUNK, ML_DH), lambda d, s: (cm(d, s), 0)),
                  pl.BlockSpec((CHUNK, ML_DH), lambda d, s: (cm(d, s), 0))],
        out_specs=pl.BlockSpec((1, CHUNK, ML_W), lambda d, s: (d, cm(d, s), 0)),
        out_shape=jax.ShapeDtypeStruct((2, t_all, ML_W), F32),
        scratch_shapes=[pltpu.VMEM((ML_H, ML_DH, ML_DH), F32),
                        pltpu.VMEM((ML_H, 1, ML_DH), F32),
                        pltpu.VMEM((ML_H, 1, 1), F32)],
        compiler_params=_cparams(("arbitrary", "arbitrary")),
        name="mlstm_scan",
    )(rest, rest, rest, gates, gates_t, gbias, gbias_t, cos_tab, sin_tab)


CONV_HALO = 8


def _conv_kernel(x_ref, p_ref, n_ref, w_ref, b_ref, o_ref, *, n_chunks):
    s = pl.program_id(0)
    x = x_ref[...]
    row = lax.broadcasted_iota(jnp.int32, x.shape, 0)
    prev = jnp.where(s >= 2, p_ref[...], 0.0)
    nxt = jnp.where((s >= 1) & (s <= n_chunks - 2), n_ref[...], 0.0)
    xm1 = jnp.where(row == 0, prev[CONV_HALO - 1:CONV_HALO, :], pltpu.roll(x, 1, 0))
    xm2 = jnp.where(row == 0, prev[CONV_HALO - 2:CONV_HALO - 1, :],
                    jnp.where(row == 1, prev[CONV_HALO - 1:CONV_HALO, :], pltpu.roll(x, 2, 0)))
    xp1 = jnp.where(row == CHUNK - 1, nxt[0:1, :], pltpu.roll(x, CHUNK - 1, 0))
    w = w_ref[0]
    y = w[0:1, :] * xm2 + w[1:2, :] * xm1 + w[2:3, :] * x + w[3:4, :] * xp1 + b_ref[0]
    o_ref[...] = _silu(y)


def ssd_conv(rest, conv_w, conv_b, layer):
    t_all = rest.shape[0]
    n_chunks = t_all // CHUNK
    hb = CHUNK // CONV_HALO
    return pl.pallas_call(
        functools.partial(_conv_kernel, n_chunks=n_chunks),
        grid=(n_chunks,),
        in_specs=[pl.BlockSpec((CHUNK, CONV_CH), lambda s: (s, 0)),
                  pl.BlockSpec((CONV_HALO, CONV_CH), lambda s: (jnp.maximum(s * hb - 1, 0), 0)),
                  pl.BlockSpec((CONV_HALO, CONV_CH), lambda s: (jnp.minimum((s + 1) * hb, n_chunks * hb - 1), 0)),
                  pl.BlockSpec((1, 4, CONV_CH), lambda s: (layer, 0, 0)),
                  pl.BlockSpec((1, 1, CONV_CH), lambda s: (layer, 0, 0))],
        out_specs=pl.BlockSpec((CHUNK, CONV_CH), lambda s: (s, 0)),
        out_shape=jax.ShapeDtypeStruct((t_all, CONV_CH), F32),
        compiler_params=_cparams(("parallel",)),
        name="ssd_conv",
    )(rest, rest, rest, conv_w, conv_b)


def _ssd_kernel(x_ref, g_ref, gt_ref, gb_ref, gbt_ref, al_ref, alt_ref, o_ref, s_scr):
    d = pl.program_id(0)
    s = pl.program_id(1)

    @pl.when(s == 0)
    def _():
        s_scr[...] = jnp.zeros_like(s_scr)

    mask, mf = _scan_masks(d)
    g, gt = _direction_gates(d, g_ref, gt_ref)
    dt = _softplus(g + gb_ref[0])
    dtt = _softplus(gt + gbt_ref[0, 0:G_DIR])
    inc = dt * (-jnp.exp(al_ref[0]))
    inct = dtt * (-jnp.exp(alt_ref[0, 0:G_DIR]))
    cum, cumt = _masked_cumsum(mf, inc, inct)
    tot = jnp.sum(inc, axis=0, keepdims=True)
    e_cum = jnp.exp(cum)
    e_tail = jnp.exp(tot - cum)
    e_tot = jnp.exp(tot)

    first = lax.broadcasted_iota(jnp.int32, (CHUNK, 2 * SSM_P), 1) < SSM_P
    first_row = lax.broadcasted_iota(jnp.int32, (2 * SSM_P, 1), 0) < SSM_P
    col = lambda a, h: a[:, G_DT + h:G_DT + h + 1]
    hpg = SSM_H // SSM_G
    for gi in range(SSM_G):
        bm = x_ref[:, SSM_W + gi * SSM_N:SSM_W + (gi + 1) * SSM_N].astype(BF16)
        cm = x_ref[:, SSM_W + (SSM_G + gi) * SSM_N:SSM_W + (SSM_G + gi + 1) * SSM_N].astype(BF16)
        gmat = _dot_nt(cm, bm)
        for pp in range(hpg // 2):
            h0 = gi * hpg + 2 * pp
            h1 = h0 + 1
            cols = slice(h0 * SSM_P, (h0 + 2) * SSM_P)
            xdt = x_ref[:, cols] * jnp.where(first, col(dt, h0), col(dt, h1))
            xdt_b = xdt.astype(BF16)
            state = s_scr[h0 // 2]
            ys = []
            for h in (h0, h1):
                decay = jnp.exp(jnp.where(mask, col(cum, h) - cumt[G_DT + h:G_DT + h + 1, :], -jnp.inf))
                ys.append(_dot((gmat * decay).astype(BF16), xdt_b))
            carried = _dot_nt(cm, state.astype(BF16)) * jnp.where(first, col(e_cum, h0), col(e_cum, h1))
            o_ref[0, :, cols] = jnp.where(first, ys[0], ys[1]) + carried
            tail = jnp.where(first, col(e_tail, h0), col(e_tail, h1))
            keep = jnp.where(first_row, col(e_tot, h0), col(e_tot, h1))
            s_scr[h0 // 2] = state * keep + _dot_tn((xdt * tail).astype(BF16), bm)


def ssd_scan(xbc_act, gates, gates_t, gbias, gbias_t, alog, alog_t):
    t_all = xbc_act.shape[0]
    n_chunks = t_all // CHUNK
    cm = lambda d, s: _scan_chunk(d, s, n_chunks)
    return pl.pallas_call(
        _ssd_kernel,
        grid=(2, n_chunks),
        in_specs=[pl.BlockSpec((CHUNK, CONV_CH), lambda d, s: (cm(d, s), 0)),
                  pl.BlockSpec((CHUNK, LANES), lambda d, s: (cm(d, s), 0)),
                  pl.BlockSpec((LANES, CHUNK), lambda d, s: (0, cm(d, s))),
                  pl.BlockSpec((1, 1, LANES), lambda d, s: (d, 0, 0)),
                  pl.BlockSpec((1, LANES, 1), lambda d, s: (d, 0, 0)),
                  pl.BlockSpec((1, 1, LANES), lambda d, s: (d, 0, 0)),
                  pl.BlockSpec((1, LANES, 1), lambda d, s: (d, 0, 0))],
        out_specs=pl.BlockSpec((1, CHUNK, SSM_W), lambda d, s: (d, cm(d, s), 0)),
        out_shape=jax.ShapeDtypeStruct((2, t_all, SSM_W), F32),
        scratch_shapes=[pltpu.VMEM((SSM_H // 2, 2 * SSM_P, SSM_N), F32)],
        compiler_params=_cparams(("arbitrary", "arbitrary")),
        name="ssd_scan",
    )(xbc_act, gates, gates_t, gbias, gbias_t, alog, alog_t)


OUT_TM = 640
OUT_TN = 512
OUT_SUB = 128


def _out_proj_kernel(ona_ref, hs_ref, mo_ref, ys_ref, xs_ref, z_ref, mlw_ref, dsk_ref, ssw_ref,
                     w_ref, x_ref, g_ref, o_ref, a_scr):
    i = pl.program_id(0)
    j = pl.program_id(1)

    @pl.when(j == 0)
    def _():
        def body(r, carry):
            rows = pl.ds(pl.multiple_of(r * OUT_SUB, OUT_SUB), OUT_SUB)
            a_scr[rows, 0:NA_W] = ona_ref[rows, :]
            hsum = hs_ref[0, rows, :] + hs_ref[1, rows, :]
            gate = 1.0 / (1.0 + jnp.exp(-mo_ref[rows, :]))
            for h in range(ML_H):
                cs = slice(h * ML_DH, (h + 1) * ML_DH)
                hh = hsum[:, cs]
                mu = jnp.mean(hh, axis=-1, keepdims=True)
                var = jnp.mean(jnp.square(hh - mu), axis=-1, keepdims=True)
                hn = (hh - mu) * lax.rsqrt(var + EPS) * mlw_ref[0, :, cs]
                a_scr[rows, NA_W + h * ML_DH:NA_W + (h + 1) * ML_DH] = (gate[:, cs] * hn).astype(BF16)
            y = ys_ref[0, rows, :] + ys_ref[1, rows, :] + dsk_ref[0] * xs_ref[rows, :]
            y = y * _silu(z_ref[rows, :])
            gw = SSM_W // SSM_G
            for gi in range(SSM_G):
                cs = slice(gi * gw, (gi + 1) * gw)
                yg = y[:, cs]
                yn = yg * lax.rsqrt(jnp.mean(yg * yg, axis=-1, keepdims=True) + EPS) * ssw_ref[0, :, cs]
                a_scr[rows, NA_W + ML_W + gi * gw:NA_W + ML_W + (gi + 1) * gw] = yn.astype(BF16)
            return carry
        lax.fori_loop(0, OUT_TM // OUT_SUB, body, 0)

    acc = _dot(a_scr[...], w_ref[0])
    is_ctx = (i * OUT_TM + lax.broadcasted_iota(jnp.int32, (OUT_TM, 1), 0)) < CTX
    gate1 = jnp.where(is_ctx, g_ref[1:2, :], g_ref[0:1, :])
    o_ref[...] = x_ref[...] + gate1 * acc


def out_projection(o_na, hs, rest, ys, xbc_act, ml_norm_w, d_skip_vec, ssm_norm_w, w_out, xa, g1, layer):
    t_all, d = xa.shape
    cw = lambda off: off // ML_W
    vec = pl.BlockSpec((1, 1, ML_W), lambda i, j: (layer, 0, 0))
    return pl.pallas_call(
        _out_proj_kernel,
        grid=(t_all // OUT_TM, d // OUT_TN),
        in_specs=[pl.BlockSpec((OUT_TM, NA_W), lambda i, j: (i, 0)),
                  pl.BlockSpec((2, OUT_TM, ML_W), lambda i, j: (0, i, 0)),
                  pl.BlockSpec((OUT_TM, ML_W), lambda i, j: (i, cw(R_MO))),
                  pl.BlockSpec((2, OUT_TM, SSM_W), lambda i, j: (0, i, 0)),
                  pl.BlockSpec((OUT_TM, SSM_W), lambda i, j: (i, 0)),
                  pl.BlockSpec((OUT_TM, SSM_W), lambda i, j: (i, cw(R_Z))),
                  vec, vec, vec,
                  pl.BlockSpec((1, d, OUT_TN), lambda i, j: (layer, 0, j)),
                  pl.BlockSpec((OUT_TM, OUT_TN), lambda i, j: (i, j)),
                  pl.BlockSpec((2, OUT_TN), lambda i, j: (0, j))],
        out_specs=pl.BlockSpec((OUT_TM, OUT_TN), lambda i, j: (i, j)),
        out_shape=jax.ShapeDtypeStruct((t_all, d), F32),
        scratch_shapes=[pltpu.VMEM((OUT_TM, d), BF16)],
        compiler_params=_cparams(("arbitrary", "arbitrary")),
        name="out_projection",
    )(o_na, hs, rest, ys, xbc_act, rest, ml_norm_w, d_skip_vec, ssm_norm_w, w_out, xa, g1)


RT_TM = 640
RT_SUB = 128


def _router_kernel(x_ref, nw_ref, sc_ref, sh_ref, wr_ref, br_ref, h_ref, r_ref, cnt_ref, run_scr):
    i = pl.program_id(0)

    @pl.when(i == 0)
    def _():
        run_scr[...] = jnp.zeros_like(run_scr)

    t_row = lax.broadcasted_iota(jnp.int32, (RT_SUB, RT_SUB), 0)
    t_col = lax.broadcasted_iota(jnp.int32, (RT_SUB, RT_SUB), 1)
    earlier = (t_col < t_row).astype(BF16)
    wr_hi, wr_lo = _split2(wr_ref[0])

    def body(r, run):
        rows = pl.ds(pl.multiple_of(r * RT_SUB, RT_SUB), RT_SUB)
        h = _modulated_norm(x_ref[rows, :], nw_ref[...], sc_ref[...], sh_ref[...], i * RT_TM + r * RT_SUB)
        h_ref[rows, :] = h
        logit = _dot_split(h, wr_hi, wr_lo) + br_ref[0]
        lane = lax.broadcasted_iota(jnp.int32, logit.shape, 1)
        big = jnp.int32(LANES)
        is_g = lane < MOE_GROUPS
        lg = jnp.where(is_g, logit, -jnp.inf)
        gmax = jnp.max(lg, axis=-1, keepdims=True)
        g_sel = jnp.min(jnp.where(is_g & (lg == gmax), lane, big), axis=-1, keepdims=True)
        g_w = 1.0 / jnp.sum(jnp.exp(lg - gmax), axis=-1, keepdims=True)
        lo = MOE_GROUPS + g_sel * MOE_PER_GROUP
        in_g = (lane >= lo) & (lane < lo + MOE_PER_GROUP)
        le = jnp.where(in_g, logit, -jnp.inf)
        v1 = jnp.max(le, axis=-1, keepdims=True)
        i1 = jnp.min(jnp.where(in_g & (le == v1), lane, big), axis=-1, keepdims=True)
        le2 = jnp.where(lane == i1, -jnp.inf, le)
        v2 = jnp.max(le2, axis=-1, keepdims=True)
        i2 = jnp.min(jnp.where(in_g & (lane != i1) & (le2 == v2), lane, big), axis=-1, keepdims=True)
        e2 = jnp.exp(v2 - v1)
        w1 = g_w / (1.0 + e2)
        w2 = g_w * e2 / (1.0 + e2)
        oh1 = (lane == i1 - MOE_GROUPS).astype(F32)
        oh2 = (lane == i2 - MOE_GROUPS).astype(F32)
        oh = oh1 + oh2
        before = _dot(earlier, oh.astype(BF16)) + run
        rank1 = jnp.sum(before * oh1, axis=-1, keepdims=True)
        rank2 = jnp.sum(before * oh2, axis=-1, keepdims=True)
        out = jnp.where(lane == 0, (i1 - MOE_GROUPS).astype(F32),
                        jnp.where(lane == 1, (i2 - MOE_GROUPS).astype(F32),
                                  jnp.where(lane == 2, w1,
                                            jnp.where(lane == 3, w2,
                                                      jnp.where(lane == 4, rank1,
                                                                jnp.where(lane == 5, rank2, 0.0))))))
        r_ref[rows, :] = out
        return run + jnp.sum(oh, axis=0, keepdims=True)
    run = lax.fori_loop(0, RT_TM // RT_SUB, body, run_scr[...])
    run_scr[...] = run
    cnt_ref[...] = jnp.broadcast_to(run, cnt_ref.shape)


def moe_router(xa, nw, sc2, sh2, w_route, b_route, layer):
    t_all, d = xa.shape
    return pl.pallas_call(
        _router_kernel,
        grid=(t_all // RT_TM,),
        in_specs=[pl.BlockSpec((RT_TM, d), lambda i: (i, 0)),
                  pl.BlockSpec((1, d), lambda i: (0, 0)),
                  pl.BlockSpec((2, d), lambda i: (0, 0)),
                  pl.BlockSpec((2, d), lambda i: (0, 0)),
                  pl.BlockSpec((1, d, LANES), lambda i: (layer, 0, 0)),
                  pl.BlockSpec((1, 1, LANES), lambda i: (layer, 0, 0))],
        out_specs=[pl.BlockSpec((RT_TM, d), lambda i: (i, 0)),
                   pl.BlockSpec((RT_TM, LANES), lambda i: (i, 0)),
                   pl.BlockSpec((8, LANES), lambda i: (0, 0))],
        out_shape=[jax.ShapeDtypeStruct((t_all, d), F32),
                   jax.ShapeDtypeStruct((t_all, LANES), F32),
                   jax.ShapeDtypeStruct((8, LANES), F32)],
        scratch_shapes=[pltpu.VMEM((1, LANES), F32)],
        compiler_params=_cparams(("arbitrary",)),
        name="moe_router",
    )(xa, nw, sc2, sh2, w_route, b_route)


EXP_SLOTS = 3


def _expert_kernel(be_ref, nu_ref, st_ref, h_hbm, w1_ref, w3_ref, w2_ref, o_ref, xbuf, sem):
    b = pl.program_id(0)
    n_used = nu_ref[0]

    def row_copy(blk, slot, t):
        tok = st_ref[blk * MOE_BM + t]
        return pltpu.make_async_copy(h_hbm.at[pl.ds(tok, 1), :], xbuf.at[slot, pl.ds(t, 1), :], sem.at[slot])

    @pl.when(b == 0)
    def _():
        def body(t, carry):
            row_copy(0, 0, t).start()
            row_copy(1, 1, t).start()
            return carry
        lax.fori_loop(0, MOE_BM, body, 0)

    @pl.when(b + 2 < n_used)
    def _():
        for t in range(MOE_BM):
            row_copy(b + 2, (b + 2) % EXP_SLOTS, t).start()

    @pl.when((b == 1) & (n_used == 1))
    def _():
        pltpu.make_async_copy(h_hbm.at[pl.ds(0, MOE_BM), :], xbuf.at[1], sem.at[1]).wait()

    @pl.when(b < n_used)
    def _():
        slot = b % EXP_SLOTS
        pltpu.make_async_copy(h_hbm.at[pl.ds(0, MOE_BM), :], xbuf.at[slot], sem.at[slot]).wait()
        x = xbuf[slot].astype(BF16)
        hid = _silu(_dot(x, w1_ref[0, 0])) * _dot(x, w3_ref[0, 0])
        o_ref[...] = _dot(hid.astype(BF16), w2_ref[0, 0])

    @pl.when(b >= n_used)
    def _():
        o_ref[...] = jnp.zeros_like(o_ref)


def expert_blocks(block_e, n_used, slot_tok, h2, w1, w3, w2, layer):
    d = h2.shape[1]
    ff = w1.shape[-1]
    n_slots = slot_tok.shape[0]
    n_blocks = n_slots // MOE_BM
    return pl.pallas_call(
        _expert_kernel,
        grid_spec=pltpu.PrefetchScalarGridSpec(
            num_scalar_prefetch=3,
            grid=(n_blocks,),
            in_specs=[pl.BlockSpec(memory_space=pl.ANY),
                      pl.BlockSpec((1, 1, d, ff), lambda b, be, nu, st: (layer, be[b], 0, 0)),
                      pl.BlockSpec((1, 1, d, ff), lambda b, be, nu, st: (layer, be[b], 0, 0)),
                      pl.BlockSpec((1, 1, ff, d), lambda b, be, nu, st: (layer, be[b], 0, 0))],
            out_specs=pl.BlockSpec((MOE_BM, d), lambda b, be, nu, st: (b, 0)),
            scratch_shapes=[pltpu.VMEM((EXP_SLOTS, MOE_BM, d), F32),
                            pltpu.SemaphoreType.DMA((EXP_SLOTS,))]),
        out_shape=jax.ShapeDtypeStruct((n_slots, d), F32),
        compiler_params=_cparams(("arbitrary",)),
        name="moe_experts",
    )(block_e, n_used, slot_tok, h2, w1, w3, w2)


def _combine_kernel(dest_ref, x_ref, r_ref, g_ref, fw_ref, y_hbm, o_ref, ybuf, sem, *,
                    row_block0, n_tiles, final_norm):
    i = pl.program_id(0)

    def row_copy(tile, slot, t, k):
        row = dest_ref[(tile + row_block0) * (2 * CHUNK) + 2 * t + k]
        return pltpu.make_async_copy(y_hbm.at[pl.ds(row, 1), :],
                                     ybuf.at[slot, pl.ds(k * CHUNK + t, 1), :], sem.at[slot])

    @pl.when(i == 0)
    def _():
        def body(t, carry):
            row_copy(0, 0, t, 0).start()
            row_copy(0, 0, t, 1).start()
            return carry
        lax.fori_loop(0, CHUNK, body, 0)

    @pl.when(i + 1 < n_tiles)
    def _():
        for t in range(CHUNK):
            row_copy(i + 1, (i + 1) % 2, t, 0).start()
            row_copy(i + 1, (i + 1) % 2, t, 1).start()

    slot = i % 2
    pltpu.make_async_copy(y_hbm.at[pl.ds(0, 2 * CHUNK), :], ybuf.at[slot], sem.at[slot]).wait()

    rt = r_ref[...]
    y = rt[:, 2:3] * ybuf[slot, 0:CHUNK, :] + rt[:, 3:4] * ybuf[slot, CHUNK:2 * CHUNK, :]
    is_ctx = ((i + row_block0) * CHUNK + lax.broadcasted_iota(jnp.int32, (CHUNK, 1), 0)) < CTX
    gate2 = jnp.where(is_ctx, g_ref[1:2, :], g_ref[0:1, :])
    x = x_ref[...] + gate2 * y
    if final_norm:
        x = x * lax.rsqrt(jnp.mean(x * x, axis=-1, keepdims=True) + EPS) * fw_ref[...]
    o_ref[...] = x


def moe_combine(dest, xa, route, g2, final_w, y, final_norm):
    t_all, d = xa.shape
    rb0 = CTX // CHUNK if final_norm else 0
    n_out = t_all // CHUNK - rb0
    return pl.pallas_call(
        functools.partial(_combine_kernel, row_block0=rb0, n_tiles=n_out, final_norm=final_norm),
        grid_spec=pltpu.PrefetchScalarGridSpec(
            num_scalar_prefetch=1,
            grid=(n_out,),
            in_specs=[pl.BlockSpec((CHUNK, d), lambda i, ds: (i + rb0, 0)),
                      pl.BlockSpec((CHUNK, LANES), lambda i, ds: (i + rb0, 0)),
                      pl.BlockSpec((2, d), lambda i, ds: (0, 0)),
                      pl.BlockSpec((1, d), lambda i, ds: (0, 0)),
                      pl.BlockSpec(memory_space=pl.ANY)],
            out_specs=pl.BlockSpec((CHUNK, d), lambda i, ds: (i, 0)),
            scratch_shapes=[pltpu.VMEM((2, 2 * CHUNK, d), F32),
                            pltpu.SemaphoreType.DMA((2,))]),
        out_shape=jax.ShapeDtypeStruct((n_out * CHUNK, d), F32),
        compiler_params=_cparams(("arbitrary",)),
        name="moe_combine",
    )(dest, xa, route, g2, final_w, y)


def moe_dispatch(route, counts):
    n = route.shape[0]
    n_assign = 2 * n
    e_flat = route[:, 0:2].astype(jnp.int32).reshape(-1)
    rank_flat = route[:, 4:6].astype(jnp.int32).reshape(-1)
    padded = (counts + MOE_BM - 1) // MOE_BM * MOE_BM
    pends = jnp.cumsum(padded)
    pstarts = pends - padded
    experts = jnp.arange(N_EXPERTS, dtype=jnp.int32)
    dest = jnp.sum(jnp.where(e_flat[:, None] == experts[None, :], pstarts[None, :], 0), axis=1) + rank_flat
    n_blocks = -(-n_assign // MOE_BM) + N_EXPERTS
    n_slots = n_blocks * MOE_BM
    tok_flat = jnp.arange(n_assign, dtype=jnp.int32) // 2
    slot_tok = jnp.zeros((n_slots,), jnp.int32).at[dest].set(tok_flat)
    block_start = jnp.arange(n_blocks, dtype=jnp.int32) * MOE_BM
    block_e = jnp.minimum(jnp.sum((pends[None, :] <= block_start[:, None]).astype(jnp.int32), axis=1),
                          N_EXPERTS - 1)
    n_used = (pends[-1] // MOE_BM).astype(jnp.int32).reshape(1)
    return dest, slot_tok, block_e, n_used


def hier_moe(xa, nw, sc2, sh2, g2, w_route, b_route, w1, w3, w2, final_w, layer, final_norm):
    h2, route, cnt = moe_router(xa, nw, sc2, sh2, w_route, b_route, layer)
    dest, slot_tok, block_e, n_used = moe_dispatch(route, cnt[0, :N_EXPERTS].astype(jnp.int32))
    y = expert_blocks(block_e, n_used, slot_tok, h2, w1, w3, w2, layer)
    return moe_combine(dest, xa, route, g2, final_w, y, final_norm)


def rope_tables(t_lat):
    nf = ML_DH // 4
    inv = ROPE_THETA ** (-jnp.arange(nf, dtype=F32) / nf)
    t_idx = jnp.arange(t_lat)
    ang_r = (t_idx // GRID_W).astype(F32)[:, None] * inv
    ang_c = (t_idx % GRID_W).astype(F32)[:, None] * inv
    cos = jnp.concatenate([jnp.cos(ang_r), jnp.cos(ang_c)] * 2, axis=-1)
    sin = jnp.concatenate([-jnp.sin(ang_r), -jnp.sin(ang_c), jnp.sin(ang_r), jnp.sin(ang_c)], axis=-1)
    cos = jnp.concatenate([jnp.ones((CTX, ML_DH), F32), cos], axis=0)
    sin = jnp.concatenate([jnp.zeros((CTX, ML_DH), F32), sin], axis=0)
    return cos, sin


def _pad_lanes(v):
    return jnp.pad(v, [(0, 0)] * (v.ndim - 1) + [(0, LANES - v.shape[-1])])


def kernel(x, c, ctx, c_ctx, w_mod, b_mod, norm1_w, norm2_w, w_in, w_out, na_rpb, ml_gate_b, ml_norm_w, ssm_conv_w, ssm_conv_b, ssm_dt_bias, ssm_a_log, ssm_d, ssm_norm_w, router_g_w, router_g_b, router_e_w, router_e_b, moe_w1, moe_w3, moe_w2, final_norm_w):
    depth = w_mod.shape[0]
    t_lat = x.shape[1]
    d = D_MODEL
    xa = jnp.concatenate([ctx[0], x[0]], axis=0)

    cond = jnp.zeros((8, d), F32).at[0].set(c[0]).at[1].set(c_ctx)
    mod = adaln_modulation(cond, w_mod, b_mod)[:, 0:2, :].reshape(depth, 2, 6, d)
    cos_tab, sin_tab = rope_tables(t_lat)

    o_mq = 3 * NA_W
    o_mg = o_mq + 4 * ML_W
    o_z = o_mg + 4 * ML_H
    o_xbc = o_z + SSM_W
    o_dt = o_xbc + CONV_CH
    w_qk = w_in[:, :, o_mq:o_mq + 2 * ML_W].reshape(depth, d, 2 * ML_H, 2, 2, ML_DH // 4)
    w_qk = jnp.swapaxes(w_qk, 3, 4).reshape(depth, d, 2 * ML_W)
    w_main = jnp.concatenate([w_in[:, :, 0:o_mq], w_in[:, :, o_xbc:o_dt], w_qk, w_in[:, :, o_mq + 2 * ML_W:o_mg],
                              w_in[:, :, o_z:o_xbc]], axis=2).astype(BF16)
    w_mg, w_dt = w_in[:, :, o_mg:o_z], w_in[:, :, o_dt:]
    w_gate = _pad_lanes(jnp.concatenate([w_mg[:, :, 0:8], w_dt[:, :, 0:8], w_mg[:, :, 8:16], w_dt[:, :, 8:16]],
                                        axis=2))
    w_out_b = w_out.astype(BF16)
    w1_b, w3_b, w2_b = moe_w1.astype(BF16), moe_w3.astype(BF16), moe_w2.astype(BF16)
    w_route = _pad_lanes(jnp.concatenate([router_g_w, router_e_w], axis=2))
    b_route = _pad_lanes(jnp.concatenate([router_g_b, router_e_b], axis=1))[:, None, :]
    gb4 = ml_gate_b.reshape(depth, 2, 2 * ML_H)
    gbias = _pad_lanes(jnp.concatenate([gb4, ssm_dt_bias], axis=2))
    alog = _pad_lanes(jnp.concatenate([jnp.zeros_like(ssm_a_log), ssm_a_log], axis=2))
    d_skip_vec = jnp.repeat(ssm_d, SSM_P, axis=1)[:, None, :]
    conv_b = ssm_conv_b[:, None, :]

    out = None
    for l in range(depth):
        last = l == depth - 1
        sh1, sc1, g1, sh2, sc2, g2 = (mod[l, :, p, :] for p in range(6))

        qkv, rest, gates, gates_t = in_projection(xa, norm1_w[l][None], sc1, sh1, w_main, w_gate, l)
        o_na = neighbourhood_attention(qkv, na_bias_table(na_rpb[l]))
        hs = mlstm_scan(rest, gates, gates_t, gbias[l][:, None, :], gbias[l][:, :, None], cos_tab, sin_tab)
        xbc_act = ssd_conv(rest, ssm_conv_w, conv_b, l)
        ys = ssd_scan(xbc_act, gates, gates_t, gbias[l][:, None, :], gbias[l][:, :, None],
                      alog[l][:, None, :], alog[l][:, :, None])
        xa = out_projection(o_na, hs, rest, ys, xbc_act, ml_norm_w[:, None, :], d_skip_vec,
                            ssm_norm_w[:, None, :], w_out_b, xa, g1, l)
        res = hier_moe(xa, norm2_w[l][None], sc2, sh2, g2, w_route, b_route, w1_b, w3_b, w2_b,
                       final_norm_w[None], l, last)
        if last:
            out = res
        else:
            xa = res
    return out[None]
```

```python
import functools

import jax
import jax.numpy as jnp
import numpy as np
from jax import lax
from jax.experimental import pallas as pl
from jax.experimental.pallas import tpu as pltpu

F32 = jnp.float32
BF16 = jnp.bfloat16
HIGHEST = lax.Precision.HIGHEST

D_MODEL = 2048
GRID_W = 64
CTX = 256
NA_DH = 128
NA_W = 1024
NA_H = 8
NA_KR = 8
NA_KC = 16
ML_DH = 128
ML_W = 512
ML_H = 4
SSM_P = 64
SSM_W = 512
SSM_H = 8
SSM_G = 2
SSM_N = 128
CONV_CH = SSM_W + 2 * SSM_G * SSM_N
MOE_GROUPS = 4
MOE_PER_GROUP = 8
N_EXPERTS = 32
MOE_FF = 1024
ROPE_THETA = 10000.0
EPS = 1e-6

CHUNK = 256
LANES = 128
NEG = -1e30
MOE_BM = 256
VMEM_LIMIT = 56 * 1024 * 1024

R_XBC, R_MQ, R_MK, R_MV, R_MO, R_Z = 0, 1024, 1536, 2048, 2560, 3072
REST_W = 3584
QKV_W = 3 * NA_W
G_I, G_F, G_DT = 0, 4, 8
G_DIR = 16


def _cparams(sem):
    return pltpu.CompilerParams(dimension_semantics=sem, vmem_limit_bytes=VMEM_LIMIT)


def _silu(x):
    return x / (1.0 + jnp.exp(-x))


def _softplus(x):
    return jnp.maximum(x, 0.0) + jnp.log1p(jnp.exp(-jnp.abs(x)))


def _dot(a, b):
    return jnp.dot(a, b, preferred_element_type=F32)


def _dot_nt(a, b):
    return lax.dot_general(a, b, (((1,), (1,)), ((), ())), preferred_element_type=F32)


def _dot_tn(a, b):
    return lax.dot_general(a, b, (((0,), (0,)), ((), ())), preferred_element_type=F32)


def _split2(x):
    hi = x.astype(BF16)
    return hi, (x - hi.astype(F32)).astype(BF16)


def _dot_split(a, w_hi, w_lo):
    a_hi, a_lo = _split2(a)
    return _dot(a_hi, w_hi) + _dot(a_hi, w_lo) + _dot(a_lo, w_hi)


def _mod_kernel(c_ref, w_ref, b_ref, o_ref):
    o_ref[0] = jnp.dot(_silu(c_ref[...]), w_ref[0], preferred_element_type=F32, precision=HIGHEST) + b_ref[0]


def adaln_modulation(cond, w_mod, b_mod):
    depth, d, n = w_mod.shape
    tn = 512
    return pl.pallas_call(
        _mod_kernel,
        grid=(depth, n // tn),
        in_specs=[pl.BlockSpec((8, d), lambda l, j: (0, 0)),
                  pl.BlockSpec((1, d, tn), lambda l, j: (l, 0, j)),
                  pl.BlockSpec((1, 1, tn), lambda l, j: (l, 0, j))],
        out_specs=pl.BlockSpec((1, 8, tn), lambda l, j: (l, 0, j)),
        out_shape=jax.ShapeDtypeStruct((depth, 8, n), F32),
        compiler_params=_cparams(("parallel", "parallel")),
        name="adaln_modulation",
    )(cond, w_mod, b_mod.reshape(depth, 1, n))


def _modulated_norm(x, nw, sc2, sh2, row0):
    r = x.shape[0]
    y = x * lax.rsqrt(jnp.mean(x * x, axis=-1, keepdims=True) + EPS) * nw
    is_ctx = (row0 + lax.broadcasted_iota(jnp.int32, (r, 1), 0)) < CTX
    sc = jnp.where(is_ctx, sc2[1:2, :], sc2[0:1, :])
    sh = jnp.where(is_ctx, sh2[1:2, :], sh2[0:1, :])
    return y * (1.0 + sc) + sh


IN_TM = 1280
IN_TN = 512
IN_SUB = 256


def _in_proj_kernel(x_ref, nw_ref, sc_ref, sh_ref, wm_ref, wg_ref, qkv_ref, rest_ref, gate_ref, gate_t_ref, a_scr):
    i = pl.program_id(0)
    j = pl.program_id(1)
    n_qkv = QKV_W // IN_TN

    @pl.when(j == 0)
    def _():
        wg_hi, wg_lo = _split2(wg_ref[0])
        for r in range(IN_TM // IN_SUB):
            rows = slice(r * IN_SUB, (r + 1) * IN_SUB)
            h = _modulated_norm(x_ref[rows, :], nw_ref[...], sc_ref[...], sh_ref[...], i * IN_TM + r * IN_SUB)
            a_scr[rows, :] = h.astype(BF16)
            gate = _dot_split(h, wg_hi, wg_lo)
            gate_ref[rows, :] = gate
            gate_t_ref[:, rows] = gate.T

    acc = _dot(a_scr[...], wm_ref[0])

    @pl.when(j < n_qkv)
    def _():
        qkv_ref[...] = acc.astype(BF16)

    @pl.when(j >= n_qkv)
    def _():
        rest_ref[...] = acc


def in_projection(xa, nw, sc2, sh2, w_main, w_gate, layer):
    t_all, d = xa.shape
    n_main = w_main.shape[2]
    n_qkv = QKV_W // IN_TN
    return pl.pallas_call(
        _in_proj_kernel,
        grid=(t_all // IN_TM, n_main // IN_TN),
        in_specs=[pl.BlockSpec((IN_TM, d), lambda i, j: (i, 0)),
                  pl.BlockSpec((1, d), lambda i, j: (0, 0)),
                  pl.BlockSpec((2, d), lambda i, j: (0, 0)),
                  pl.BlockSpec((2, d), lambda i, j: (0, 0)),
                  pl.BlockSpec((1, d, IN_TN), lambda i, j: (layer, 0, j)),
                  pl.BlockSpec((1, d, LANES), lambda i, j: (layer, 0, 0))],
        out_specs=[pl.BlockSpec((IN_TM, IN_TN), lambda i, j: (i, jnp.minimum(j, n_qkv - 1))),
                   pl.BlockSpec((IN_TM, IN_TN), lambda i, j: (i, jnp.maximum(j - n_qkv, 0))),
                   pl.BlockSpec((IN_TM, LANES), lambda i, j: (i, 0)),
                   pl.BlockSpec((LANES, IN_TM), lambda i, j: (0, i))],
        out_shape=[jax.ShapeDtypeStruct((t_all, QKV_W), BF16),
                   jax.ShapeDtypeStruct((t_all, REST_W), F32),
                   jax.ShapeDtypeStruct((t_all, LANES), F32),
                   jax.ShapeDtypeStruct((LANES, t_all), F32)],
        scratch_shapes=[pltpu.VMEM((IN_TM, d), BF16)],
        compiler_params=_cparams(("arbitrary", "arbitrary")),
        name="in_projection",
    )(xa, nw, sc2, sh2, w_main, w_gate)


NA_RB = CHUNK // GRID_W
NA_UR = 12


NA_HB = 2


def _na_kernel(q_ref, k_ref, v_ref, bias_ref, o_ref, *, n_rows):
    rb = pl.program_id(1)
    scale = NA_DH ** -0.5

    @pl.when(rb == 0)
    def _():
        for hh in range(NA_HB):
            hs = slice(hh * NA_DH, (hh + 1) * NA_DH)
            s = _dot_nt(q_ref[:, hs], k_ref[0:CTX, hs]) * scale
            m = jnp.max(s, axis=-1, keepdims=True)
            p = jnp.exp(s - m)
            l = jnp.sum(p, axis=-1, keepdims=True)
            o_ref[:, hs] = (_dot(p.astype(BF16), v_ref[0:CTX, hs]) / l).astype(o_ref.dtype)

    @pl.when(rb > 0)
    def _():
        r0 = (rb - 1) * NA_RB
        u0 = jnp.clip(r0 - NA_KR // 2, 0, n_rows - NA_UR)
        case = jnp.where(r0 == 0, 1, jnp.where(r0 == n_rows - NA_RB, 2, 0))
        start = pl.multiple_of(CTX + u0 * GRID_W, GRID_W)
        for hh in range(NA_HB):
            hs = slice(hh * NA_DH, (hh + 1) * NA_DH)
            q = q_ref[:, hs]
            kc = k_ref[0:CTX, hs]
            vc = v_ref[0:CTX, hs]
            kw = k_ref[pl.ds(start, NA_UR * GRID_W), hs]
            vw = v_ref[pl.ds(start, NA_UR * GRID_W), hs]
            s = _dot_nt(q, kw) * scale + bias_ref[hh, case]
            sc = _dot_nt(q, kc) * scale
            m = jnp.maximum(jnp.max(s, axis=-1, keepdims=True), jnp.max(sc, axis=-1, keepdims=True))
            p = jnp.exp(s - m)
            pc = jnp.exp(sc - m)
            l = jnp.sum(p, axis=-1, keepdims=True) + jnp.sum(pc, axis=-1, keepdims=True)
            o = _dot(p.astype(BF16), vw) + _dot(pc.astype(BF16), vc)
            o_ref[:, hs] = (o / l).astype(o_ref.dtype)


def na_bias_table(rpb):
    h = rpb.shape[0]
    c = np.arange(GRID_W)
    cs = np.clip(c - NA_KC // 2, 0, GRID_W - NA_KC)
    kcol = np.arange(GRID_W)
    inside = (kcol[None, :] >= cs[:, None]) & (kcol[None, :] < cs[:, None] + NA_KC)
    off = kcol[None, :] - c[:, None] + NA_KC - 1
    onehot = (off[None] == np.arange(2 * NA_KC - 1)[:, None, None]) & inside[None]
    band = jnp.einsum('hro,ock->hrck', rpb.astype(F32), jnp.asarray(onehot, F32), precision=HIGHEST)
    band = band + jnp.asarray(np.where(inside, 0.0, NEG), F32)
    neg = jnp.full((h, GRID_W, GRID_W), NEG, F32)
    cases = ([(rr, NA_KR // 2 - 1) for rr in range(NA_RB)],
             [(0, NA_KR - 1 - rr) for rr in range(NA_RB)],
             [(NA_UR - NA_KR, NA_KR // 2 - 1 - rr) for rr in range(NA_RB)])
    tabs = []
    for case in cases:
        rows = []
        for w_off, d0 in case:
            rows.append(jnp.concatenate(
                [band[:, d0 + u - w_off] if 0 <= u - w_off < NA_KR else neg for u in range(NA_UR)], axis=-1))
        tabs.append(jnp.concatenate(rows, axis=1))
    return jnp.stack(tabs, axis=1)


def neighbourhood_attention(qkv, bias_tab):
    t_all = qkv.shape[0]
    n_rows = (t_all - CTX) // GRID_W
    n_rb = t_all // CHUNK
    return pl.pallas_call(
        functools.partial(_na_kernel, n_rows=n_rows),
        grid=(NA_H // NA_HB, n_rb),
        in_specs=[pl.BlockSpec((CHUNK, NA_HB * NA_DH), lambda h, rb: (rb, h)),
                  pl.BlockSpec((t_all, NA_HB * NA_DH), lambda h, rb: (0, NA_H // NA_HB + h)),
                  pl.BlockSpec((t_all, NA_HB * NA_DH), lambda h, rb: (0, 2 * (NA_H // NA_HB) + h)),
                  pl.BlockSpec((NA_HB, 3, CHUNK, NA_UR * GRID_W), lambda h, rb: (h, 0, 0, 0))],
        out_specs=pl.BlockSpec((CHUNK, NA_HB * NA_DH), lambda h, rb: (rb, h)),
        out_shape=jax.ShapeDtypeStruct((t_all, NA_W), BF16),
        compiler_params=_cparams(("parallel", "parallel")),
        name="neighbourhood_attention",
    )(qkv, qkv, qkv, bias_tab)


def _scan_chunk(d, s, n_chunks):
    return jnp.where(d == 0, s, jnp.where(s == 0, 0, n_chunks - s))


def _scan_masks(d):
    row = lax.broadcasted_iota(jnp.int32, (CHUNK, CHUNK), 0)
    col = lax.broadcasted_iota(jnp.int32, (CHUNK, CHUNK), 1)
    mask = jnp.where(d == 0, row - col, col - row) >= 0
    return mask, mask.astype(BF16)


def _split3(x):
    hi = x.astype(BF16)
    r = x - hi.astype(F32)
    mid = r.astype(BF16)
    return hi, mid, (r - mid.astype(F32)).astype(BF16)


def _masked_cumsum(mb, x, xt):
    cum = sum(_dot(mb, p) for p in _split3(x))
    cumt = sum(_dot_nt(p, mb) for p in _split3(xt))
    return cum, cumt


def _direction_gates(d, g_ref, gt_ref):
    g = g_ref[...]
    g = jnp.where(d == 0, g, pltpu.roll(g, LANES - G_DIR, 1))
    gt = jnp.where(d == 0, gt_ref[0:G_DIR, :], gt_ref[G_DIR:2 * G_DIR, :])
    return g, gt


def _rope(x, cos, sin_signed):
    return x * cos + pltpu.roll(x, ML_DH // 2, 1) * sin_signed


def _mlstm_kernel(q_ref, k_ref, v_ref, g_ref, gt_ref, gb_ref, gbt_ref, cos_ref, sin_ref, o_ref,
                  c_scr, n_scr, m_scr):
    d = pl.program_id(0)
    s = pl.program_id(1)

    @pl.when(s == 0)
    def _():
        c_scr[...] = jnp.zeros_like(c_scr)
        n_scr[...] = jnp.zeros_like(n_scr)
        m_scr[...] = jnp.zeros_like(m_scr)

    mask, mf = _scan_masks(d)
    g, gt = _direction_gates(d, g_ref, gt_ref)
    g = g + gb_ref[0]
    gt = gt + gbt_ref[0, 0:G_DIR]
    lf = -_softplus(-g)
    lft = -_softplus(-gt)
    cum, cumt = _masked_cumsum(mf, lf, lft)
    tot = jnp.sum(lf, axis=0, keepdims=True)
    cos = cos_ref[...]
    sin = sin_ref[...]

    for h in range(ML_H):
        hs = slice(h * ML_DH, (h + 1) * ML_DH)
        q = _rope(q_ref[:, hs], cos, sin)
        k = _rope(k_ref[:, hs], cos, sin) * (ML_DH ** -0.5)
        v = v_ref[:, hs]
        qb, kb, vb = q.astype(BF16), k.astype(BF16), v.astype(BF16)
        bt_col = cum[:, G_F + h:G_F + h + 1]
        bt_row = cumt[G_F + h:G_F + h + 1, :]
        ig_col = g[:, G_I + h:G_I + h + 1]
        ig_row = gt[G_I + h:G_I + h + 1, :]
        b_last = tot[:, G_F + h:G_F + h + 1]
        m_prev = m_scr[h]
        cmat = c_scr[h]
        nvec = n_scr[h]

        log_d = jnp.where(mask, bt_col - bt_row + ig_row, -jnp.inf)
        inter = bt_col + m_prev
        m_t = jnp.maximum(jnp.max(log_d, axis=-1, keepdims=True), inter)
        wts = _dot_nt(qb, kb) * jnp.exp(log_d - m_t)
        sc = jnp.exp(inter - m_t)
        num = _dot(wts.astype(BF16), vb) + _dot_nt(qb, cmat.astype(BF16)) * sc
        den = jnp.sum(wts, axis=-1, keepdims=True) + jnp.sum(q * nvec, axis=-1, keepdims=True) * sc
        den = jnp.maximum(jnp.abs(den), jnp.exp(-m_t))
        o_ref[0, :, hs] = num / den

        tail = b_last - bt_col + ig_col
        m_new = jnp.maximum(b_last + m_prev, jnp.max(tail, axis=0, keepdims=True))
        wgt = jnp.exp(tail - m_new)
        decay = jnp.exp(b_last + m_prev - m_new)
        c_scr[h] = decay * cmat + _dot_tn((v * wgt).astype(BF16), kb)
        n_scr[h] = decay * nvec + jnp.sum(wgt * k, axis=0, keepdims=True)
        m_scr[h] = m_new


def mlstm_scan(rest, gates, gates_t, gbias, gbias_t, cos_tab, sin_tab):
    t_all = rest.shape[0]
    n_chunks = t_all // CHUNK
    cm = lambda d, s: _scan_chunk(d, s, n_chunks)
    col = lambda off: off // ML_W
    return pl.pallas_call(
        _mlstm_kernel,
        grid=(2, n_chunks),
        in_specs=[pl.BlockSpec((CHUNK, ML_W), lambda d, s: (cm(d, s), col(R_MQ))),
                  pl.BlockSpec((CHUNK, ML_W), lambda d, s: (cm(d, s), col(R_MK))),
                  pl.BlockSpec((CHUNK, ML_W), lambda d, s: (cm(d, s), col(R_MV))),
                  pl.BlockSpec((CHUNK, LANES), lambda d, s: (cm(d, s), 0)),
                  pl.BlockSpec((LANES, CHUNK), lambda d, s: (0, cm(d, s))),
                  pl.BlockSpec((1, 1, LANES), lambda d, s: (d, 0, 0)),
                  pl.BlockSpec((1, LANES, 1), lambda d, s: (d, 0, 0)),
                  pl.BlockSpec((CHUNK, ML_DH), lambda d, s: (cm(d, s), 0)),
                  pl.BlockSpec((CHUNK, ML_DH), lambda d, s: (cm(d, s), 0))],
        out_specs=pl.BlockSpec((1, CHUNK, ML_W), lambda d, s: (d, cm(d, s), 0)),
        out_shape=jax.ShapeDtypeStruct((2, t_all, ML_W), F32),
        scratch_shapes=[pltpu.VMEM((ML_H, ML_DH, ML_DH), F32),
                        pltpu.VMEM((ML_H, 1, ML_DH), F32),
                        pltpu.VMEM((ML_H, 1, 1), F32)],
        compiler_params=_cparams(("arbitrary", "arbitrary")),
        name="mlstm_scan",
    )(rest, rest, rest, gates, gates_t, gbias, gbias_t, cos_tab, sin_tab)


CONV_HALO = 8


def _conv_kernel(x_ref, p_ref, n_ref, w_ref, b_ref, o_ref, *, n_chunks):
    s = pl.program_id(0)
    x = x_ref[...]
    row = lax.broadcasted_iota(jnp.int32, x.shape, 0)
    prev = jnp.where(s >= 2, p_ref[...], 0.0)
    nxt = jnp.where((s >= 1) & (s <= n_chunks - 2), n_ref[...], 0.0)
    xm1 = jnp.where(row == 0, prev[CONV_HALO - 1:CONV_HALO, :], pltpu.roll(x, 1, 0))
    xm2 = jnp.where(row == 0, prev[CONV_HALO - 2:CONV_HALO - 1, :],
                    jnp.where(row == 1, prev[CONV_HALO - 1:CONV_HALO, :], pltpu.roll(x, 2, 0)))
    xp1 = jnp.where(row == CHUNK - 1, nxt[0:1, :], pltpu.roll(x, CHUNK - 1, 0))
    w = w_ref[0]
    y = w[0:1, :] * xm2 + w[1:2, :] * xm1 + w[2:3, :] * x + w[3:4, :] * xp1 + b_ref[0]
    o_ref[...] = _silu(y)


def ssd_conv(rest, conv_w, conv_b, layer):
    t_all = rest.shape[0]
    n_chunks = t_all // CHUNK
    hb = CHUNK // CONV_HALO
    return pl.pallas_call(
        functools.partial(_conv_kernel, n_chunks=n_chunks),
        grid=(n_chunks,),
        in_specs=[pl.BlockSpec((CHUNK, CONV_CH), lambda s: (s, 0)),
                  pl.BlockSpec((CONV_HALO, CONV_CH), lambda s: (jnp.maximum(s * hb - 1, 0), 0)),
                  pl.BlockSpec((CONV_HALO, CONV_CH), lambda s: (jnp.minimum((s + 1) * hb, n_chunks * hb - 1), 0)),
                  pl.BlockSpec((1, 4, CONV_CH), lambda s: (layer, 0, 0)),
                  pl.BlockSpec((1, 1, CONV_CH), lambda s: (layer, 0, 0))],
        out_specs=pl.BlockSpec((CHUNK, CONV_CH), lambda s: (s, 0)),
        out_shape=jax.ShapeDtypeStruct((t_all, CONV_CH), F32),
        compiler_params=_cparams(("parallel",)),
        name="ssd_conv",
    )(rest, rest, rest, conv_w, conv_b)


def _ssd_kernel(x_ref, g_ref, gt_ref, gb_ref, gbt_ref, al_ref, alt_ref, o_ref, s_scr):
    d = pl.program_id(0)
    s = pl.program_id(1)

    @pl.when(s == 0)
    def _():
        s_scr[...] = jnp.zeros_like(s_scr)

    mask, mf = _scan_masks(d)
    g, gt = _direction_gates(d, g_ref, gt_ref)
    dt = _softplus(g + gb_ref[0])
    dtt = _softplus(gt + gbt_ref[0, 0:G_DIR])
    inc = dt * (-jnp.exp(al_ref[0]))
    inct = dtt * (-jnp.exp(alt_ref[0, 0:G_DIR]))
    cum, cumt = _masked_cumsum(mf, inc, inct)
    tot = jnp.sum(inc, axis=0, keepdims=True)
    e_cum = jnp.exp(cum)
    e_tail = jnp.exp(tot - cum)
    e_tot = jnp.exp(tot)

    first = lax.broadcasted_iota(jnp.int32, (CHUNK, 2 * SSM_P), 1) < SSM_P
    first_row = lax.broadcasted_iota(jnp.int32, (2 * SSM_P, 1), 0) < SSM_P
    col = lambda a, h: a[:, G_DT + h:G_DT + h + 1]
    hpg = SSM_H // SSM_G
    for gi in range(SSM_G):
        bm = x_ref[:, SSM_W + gi * SSM_N:SSM_W + (gi + 1) * SSM_N].astype(BF16)
        cm = x_ref[:, SSM_W + (SSM_G + gi) * SSM_N:SSM_W + (SSM_G + gi + 1) * SSM_N].astype(BF16)
        gmat = _dot_nt(cm, bm)
        for pp in range(hpg // 2):
            h0 = gi * hpg + 2 * pp
            h1 = h0 + 1
            cols = slice(h0 * SSM_P, (h0 + 2) * SSM_P)
            xdt = x_ref[:, cols] * jnp.where(first, col(dt, h0), col(dt, h1))
            xdt_b = xdt.astype(BF16)
            state = s_scr[h0 // 2]
            ys = []
            for h in (h0, h1):
                decay = jnp.exp(jnp.where(mask, col(cum, h) - cumt[G_DT + h:G_DT + h + 1, :], -jnp.inf))
                ys.append(_dot((gmat * decay).astype(BF16), xdt_b))
            carried = _dot_nt(cm, state.astype(BF16)) * jnp.where(first, col(e_cum, h0), col(e_cum, h1))
            o_ref[0, :, cols] = jnp.where(first, ys[0], ys[1]) + carried
            tail = jnp.where(first, col(e_tail, h0), col(e_tail, h1))
            keep = jnp.where(first_row, col(e_tot, h0), col(e_tot, h1))
            s_scr[h0 // 2] = state * keep + _dot_tn((xdt * tail).astype(BF16), bm)


def ssd_scan(xbc_act, gates, gates_t, gbias, gbias_t, alog, alog_t):
    t_all = xbc_act.shape[0]
    n_chunks = t_all // CHUNK
    cm = lambda d, s: _scan_chunk(d, s, n_chunks)
    return pl.pallas_call(
        _ssd_kernel,
        grid=(2, n_chunks),
        in_specs=[pl.BlockSpec((CHUNK, CONV_CH), lambda d, s: (cm(d, s), 0)),
                  pl.BlockSpec((CHUNK, LANES), lambda d, s: (cm(d, s), 0)),
                  pl.BlockSpec((LANES, CHUNK), lambda d, s: (0, cm(d, s))),
                  pl.BlockSpec((1, 1, LANES), lambda d, s: (d, 0, 0)),
                  pl.BlockSpec((1, LANES, 1), lambda d, s: (d, 0, 0)),
                  pl.BlockSpec((1, 1, LANES), lambda d, s: (d, 0, 0)),
                  pl.BlockSpec((1, LANES, 1), lambda d, s: (d, 0, 0))],
        out_specs=pl.BlockSpec((1, CHUNK, SSM_W), lambda d, s: (d, cm(d, s), 0)),
        out_shape=jax.ShapeDtypeStruct((2, t_all, SSM_W), F32),
        scratch_shapes=[pltpu.VMEM((SSM_H // 2, 2 * SSM_P, SSM_N), F32)],
        compiler_params=_cparams(("arbitrary", "arbitrary")),
        name="ssd_scan",
    )(xbc_act, gates, gates_t, gbias, gbias_t, alog, alog_t)


OUT_TM = 640
OUT_TN = 512
OUT_SUB = 128


def _out_proj_kernel(ona_ref, hs_ref, mo_ref, ys_ref, xs_ref, z_ref, mlw_ref, dsk_ref, ssw_ref,
                     w_ref, x_ref, g_ref, o_ref, a_scr):
    i = pl.program_id(0)
    j = pl.program_id(1)

    @pl.when(j == 0)
    def _():
        def body(r, carry):
            rows = pl.ds(pl.multiple_of(r * OUT_SUB, OUT_SUB), OUT_SUB)
            a_scr[rows, 0:NA_W] = ona_ref[rows, :]
            hsum = hs_ref[0, rows, :] + hs_ref[1, rows, :]
            gate = 1.0 / (1.0 + jnp.exp(-mo_ref[rows, :]))
            for h in range(ML_H):
                cs = slice(h * ML_DH, (h + 1) * ML_DH)
                hh = hsum[:, cs]
                mu = jnp.mean(hh, axis=-1, keepdims=True)
                var = jnp.mean(jnp.square(hh - mu), axis=-1, keepdims=True)
                hn = (hh - mu) * lax.rsqrt(var + EPS) * mlw_ref[0, :, cs]
                a_scr[rows, NA_W + h * ML_DH:NA_W + (h + 1) * ML_DH] = (gate[:, cs] * hn).astype(BF16)
            y = ys_ref[0, rows, :] + ys_ref[1, rows, :] + dsk_ref[0] * xs_ref[rows, :]
            y = y * _silu(z_ref[rows, :])
            gw = SSM_W // SSM_G
            for gi in range(SSM_G):
                cs = slice(gi * gw, (gi + 1) * gw)
                yg = y[:, cs]
                yn = yg * lax.rsqrt(jnp.mean(yg * yg, axis=-1, keepdims=True) + EPS) * ssw_ref[0, :, cs]
                a_scr[rows, NA_W + ML_W + gi * gw:NA_W + ML_W + (gi + 1) * gw] = yn.astype(BF16)
            return carry
        lax.fori_loop(0, OUT_TM // OUT_SUB, body, 0)

    acc = _dot(a_scr[...], w_ref[0])
    is_ctx = (i * OUT_TM + lax.broadcasted_iota(jnp.int32, (OUT_TM, 1), 0)) < CTX
    gate1 = jnp.where(is_ctx, g_ref[1:2, :], g_ref[0:1, :])
    o_ref[...] = x_ref[...] + gate1 * acc


def out_projection(o_na, hs, rest, ys, xbc_act, ml_norm_w, d_skip_vec, ssm_norm_w, w_out, xa, g1, layer):
    t_all, d = xa.shape
    cw = lambda off: off // ML_W
    vec = pl.BlockSpec((1, 1, ML_W), lambda i, j: (layer, 0, 0))
    return pl.pallas_call(
        _out_proj_kernel,
        grid=(t_all // OUT_TM, d // OUT_TN),
        in_specs=[pl.BlockSpec((OUT_TM, NA_W), lambda i, j: (i, 0)),
                  pl.BlockSpec((2, OUT_TM, ML_W), lambda i, j: (0, i, 0)),
                  pl.BlockSpec((OUT_TM, ML_W), lambda i, j: (i, cw(R_MO))),
                  pl.BlockSpec((2, OUT_TM, SSM_W), lambda i, j: (0, i, 0)),
                  pl.BlockSpec((OUT_TM, SSM_W), lambda i, j: (i, 0)),
                  pl.BlockSpec((OUT_TM, SSM_W), lambda i, j: (i, cw(R_Z))),
                  vec, vec, vec,
                  pl.BlockSpec((1, d, OUT_TN), lambda i, j: (layer, 0, j)),
                  pl.BlockSpec((OUT_TM, OUT_TN), lambda i, j: (i, j)),
                  pl.BlockSpec((2, OUT_TN), lambda i, j: (0, j))],
        out_specs=pl.BlockSpec((OUT_TM, OUT_TN), lambda i, j: (i, j)),
        out_shape=jax.ShapeDtypeStruct((t_all, d), F32),
        scratch_shapes=[pltpu.VMEM((OUT_TM, d), BF16)],
        compiler_params=_cparams(("arbitrary", "arbitrary")),
        name="out_projection",
    )(o_na, hs, rest, ys, xbc_act, rest, ml_norm_w, d_skip_vec, ssm_norm_w, w_out, xa, g1)


RT_TM = 640
RT_SUB = 128


def _router_kernel(x_ref, nw_ref, sc_ref, sh_ref, wr_ref, br_ref, h_ref, r_ref, cnt_ref, run_scr):
    i = pl.program_id(0)

    @pl.when(i == 0)
    def _():
        run_scr[...] = jnp.zeros_like(run_scr)

    t_row = lax.broadcasted_iota(jnp.int32, (RT_SUB, RT_SUB), 0)
    t_col = lax.broadcasted_iota(jnp.int32, (RT_SUB, RT_SUB), 1)
    earlier = (t_col < t_row).astype(BF16)
    wr_hi, wr_lo = _split2(wr_ref[0])

    def body(r, run):
        rows = pl.ds(pl.multiple_of(r * RT_SUB, RT_SUB), RT_SUB)
        h = _modulated_norm(x_ref[rows, :], nw_ref[...], sc_ref[...], sh_ref[...], i * RT_TM + r * RT_SUB)
        h_ref[rows, :] = h
        logit = _dot_split(h, wr_hi, wr_lo) + br_ref[0]
        lane = lax.broadcasted_iota(jnp.int32, logit.shape, 1)
        big = jnp.int32(LANES)
        is_g = lane < MOE_GROUPS
        lg = jnp.where(is_g, logit, -jnp.inf)
        gmax = jnp.max(lg, axis=-1, keepdims=True)
        g_sel = jnp.min(jnp.where(is_g & (lg == gmax), lane, big), axis=-1, keepdims=True)
        g_w = 1.0 / jnp.sum(jnp.exp(lg - gmax), axis=-1, keepdims=True)
        lo = MOE_GROUPS + g_sel * MOE_PER_GROUP
        in_g = (lane >= lo) & (lane < lo + MOE_PER_GROUP)
        le = jnp.where(in_g, logit, -jnp.inf)
        v1 = jnp.max(le, axis=-1, keepdims=True)
        i1 = jnp.min(jnp.where(in_g & (le == v1), lane, big), axis=-1, keepdims=True)
        le2 = jnp.where(lane == i1, -jnp.inf, le)
        v2 = jnp.max(le2, axis=-1, keepdims=True)
        i2 = jnp.min(jnp.where(in_g & (lane != i1) & (le2 == v2), lane, big), axis=-1, keepdims=True)
        e2 = jnp.exp(v2 - v1)
        w1 = g_w / (1.0 + e2)
        w2 = g_w * e2 / (1.0 + e2)
        oh1 = (lane == i1 - MOE_GROUPS).astype(F32)
        oh2 = (lane == i2 - MOE_GROUPS).astype(F32)
        oh = oh1 + oh2
        before = _dot(earlier, oh.astype(BF16)) + run
        rank1 = jnp.sum(before * oh1, axis=-1, keepdims=True)
        rank2 = jnp.sum(before * oh2, axis=-1, keepdims=True)
        out = jnp.where(lane == 0, (i1 - MOE_GROUPS).astype(F32),
                        jnp.where(lane == 1, (i2 - MOE_GROUPS).astype(F32),
                                  jnp.where(lane == 2, w1,
                                            jnp.where(lane == 3, w2,
                                                      jnp.where(lane == 4, rank1,
                                                                jnp.where(lane == 5, rank2, 0.0))))))
        r_ref[rows, :] = out
        return run + jnp.sum(oh, axis=0, keepdims=True)
    run = lax.fori_loop(0, RT_TM // RT_SUB, body, run_scr[...])
    run_scr[...] = run
    cnt_ref[...] = jnp.broadcast_to(run, cnt_ref.shape)


def moe_router(xa, nw, sc2, sh2, w_route, b_route, layer):
    t_all, d = xa.shape
    return pl.pallas_call(
        _router_kernel,
        grid=(t_all // RT_TM,),
        in_specs=[pl.BlockSpec((RT_TM, d), lambda i: (i, 0)),
                  pl.BlockSpec((1, d), lambda i: (0, 0)),
                  pl.BlockSpec((2, d), lambda i: (0, 0)),
                  pl.BlockSpec((2, d), lambda i: (0, 0)),
                  pl.BlockSpec((1, d, LANES), lambda i: (layer, 0, 0)),
                  pl.BlockSpec((1, 1, LANES), lambda i: (layer, 0, 0))],
        out_specs=[pl.BlockSpec((RT_TM, d), lambda i: (i, 0)),
                   pl.BlockSpec((RT_TM, LANES), lambda i: (i, 0)),
                   pl.BlockSpec((8, LANES), lambda i: (0, 0))],
        out_shape=[jax.ShapeDtypeStruct((t_all, d), F32),
                   jax.ShapeDtypeStruct((t_all, LANES), F32),
                   jax.ShapeDtypeStruct((8, LANES), F32)],
        scratch_shapes=[pltpu.VMEM((1, LANES), F32)],
        compiler_params=_cparams(("arbitrary",)),
        name="moe_router",
    )(xa, nw, sc2, sh2, w_route, b_route)


EXP_SLOTS = 3


W_CHUNKS = 12
WA_ROWS = D_MODEL // 4
WB_ROWS = MOE_FF // 4
SCHED_W = 4


def _expert_kernel(be_ref, nu_ref, st_ref, sc_ref, h_hbm, w1_hbm, w3_hbm, w2_hbm, o_ref,
                   xbuf, wb1, wb3, wb2, stg_a, stg_b, sem, sem_a, sem_b, *, layer):
    b = pl.program_id(0)
    n_used = nu_ref[0]
    cur = sc_ref[SCHED_W * b]
    nxt = sc_ref[SCHED_W * b + 1]
    c0 = sc_ref[SCHED_W * b + 2]
    c1 = sc_ref[SCHED_W * b + 3]

    def chunk_copy(e, k, kind):
        st = k % 2
        if kind == 2:
            src = w2_hbm.at[layer, e, pl.ds(pl.multiple_of((k - 8) * WB_ROWS, WB_ROWS), WB_ROWS), :]
            return pltpu.make_async_copy(src, stg_b.at[st], sem_b.at[st])
        w_hbm = w1_hbm if kind == 0 else w3_hbm
        src = w_hbm.at[layer, e, pl.ds(pl.multiple_of((k - 4 * kind) * WA_ROWS, WA_ROWS), WA_ROWS), :]
        return pltpu.make_async_copy(src, stg_a.at[st], sem_a.at[st])

    def by_kind(k, fn):
        @pl.when(k < 4)
        def _():
            fn(0)

        @pl.when((k >= 4) & (k < 8))
        def _():
            fn(1)

        @pl.when(k >= 8)
        def _():
            fn(2)

    def start_chunk(e, k):
        by_kind(k, lambda kind: chunk_copy(e, k, kind).start())

    def land_chunk(e, k, slot):
        def fn(kind):
            chunk_copy(e, k, kind).wait()
            if kind == 2:
                rows = pl.ds(pl.multiple_of((k - 8) * WB_ROWS, WB_ROWS), WB_ROWS)
                wb2[slot, rows, :] = stg_b[k % 2].astype(BF16)
            else:
                rows = pl.ds(pl.multiple_of((k - 4 * kind) * WA_ROWS, WA_ROWS), WA_ROWS)
                dst = wb1 if kind == 0 else wb3
                dst[slot, rows, :] = stg_a[k % 2].astype(BF16)
        by_kind(k, fn)

    def stream(e, slot, k0, k1):
        def body(k, carry):
            land_chunk(e, k, slot)

            @pl.when(k + 2 < W_CHUNKS)
            def _():
                start_chunk(e, k + 2)
            return carry
        lax.fori_loop(k0, k1, body, 0)

    @pl.when(b == 0)
    def _():
        e0 = be_ref[0]
        start_chunk(e0, 0)
        start_chunk(e0, 1)
        stream(e0, cur, 0, W_CHUNKS)

    @pl.when((c0 == 0) & (c1 > 0))
    def _():
        start_chunk(nxt, 0)
        start_chunk(nxt, 1)

    stream(nxt, 1 - cur, c0, c1)

    def row_copy(blk, slot, t):
        tok = st_ref[blk * MOE_BM + t]
        return pltpu.make_async_copy(h_hbm.at[pl.ds(tok, 1), :], xbuf.at[slot, pl.ds(t, 1), :], sem.at[slot])

    @pl.when(b == 0)
    def _():
        def body(t, carry):
            row_copy(0, 0, t).start()
            row_copy(1, 1, t).start()
            return carry
        lax.fori_loop(0, MOE_BM, body, 0)

    @pl.when(b + 2 < n_used)
    def _():
        for t in range(MOE_BM):
            row_copy(b + 2, (b + 2) % EXP_SLOTS, t).start()

    @pl.when((b == 1) & (n_used == 1))
    def _():
        pltpu.make_async_copy(h_hbm.at[pl.ds(0, MOE_BM), :], xbuf.at[1], sem.at[1]).wait()

    @pl.when(b < n_used)
    def _():
        slot = b % EXP_SLOTS
        pltpu.make_async_copy(h_hbm.at[pl.ds(0, MOE_BM), :], xbuf.at[slot], sem.at[slot]).wait()
        x = xbuf[slot].astype(BF16)
        hid = _silu(_dot(x, wb1[cur])) * _dot(x, wb3[cur])
        o_ref[...] = _dot(hid.astype(BF16), wb2[cur])

    @pl.when(b >= n_used)
    def _():
        o_ref[...] = jnp.zeros_like(o_ref)


def expert_blocks(block_e, n_used, slot_tok, sched, h2, w1, w3, w2, layer):
    d = h2.shape[1]
    ff = w1.shape[-1]
    n_slots = slot_tok.shape[0]
    n_blocks = n_slots // MOE_BM
    any_spec = pl.BlockSpec(memory_space=pl.ANY)
    return pl.pallas_call(
        functools.partial(_expert_kernel, layer=layer),
        grid_spec=pltpu.PrefetchScalarGridSpec(
            num_scalar_prefetch=4,
            grid=(n_blocks,),
            in_specs=[any_spec, any_spec, any_spec, any_spec],
            out_specs=pl.BlockSpec((MOE_BM, d), lambda b, be, nu, st, sc: (b, 0)),
            scratch_shapes=[pltpu.VMEM((EXP_SLOTS, MOE_BM, d), F32),
                            pltpu.VMEM((2, d, ff), BF16),
                            pltpu.VMEM((2, d, ff), BF16),
                            pltpu.VMEM((2, ff, d), BF16),
                            pltpu.VMEM((2, WA_ROWS, ff), F32),
                            pltpu.VMEM((2, WB_ROWS, d), F32),
                            pltpu.SemaphoreType.DMA((EXP_SLOTS,)),
                            pltpu.SemaphoreType.DMA((2,)),
                            pltpu.SemaphoreType.DMA((2,))]),
        out_shape=jax.ShapeDtypeStruct((n_slots, d), F32),
        compiler_params=_cparams(("arbitrary",)),
        name="moe_experts",
    )(block_e, n_used, slot_tok, sched, h2, w1, w3, w2)


def _combine_kernel(dest_ref, x_ref, r_ref, g_ref, fw_ref, y_hbm, o_ref, ybuf, sem, *,
                    row_block0, n_tiles, final_norm):
    i = pl.program_id(0)

    def row_copy(tile, slot, t, k):
        row = dest_ref[(tile + row_block0) * (2 * CHUNK) + 2 * t + k]
        return pltpu.make_async_copy(y_hbm.at[pl.ds(row, 1), :],
                                     ybuf.at[slot, pl.ds(k * CHUNK + t, 1), :], sem.at[slot])

    @pl.when(i == 0)
    def _():
        def body(t, carry):
            row_copy(0, 0, t, 0).start()
            row_copy(0, 0, t, 1).start()
            return carry
        lax.fori_loop(0, CHUNK, body, 0)

    @pl.when(i + 1 < n_tiles)
    def _():
        for t in range(CHUNK):
            row_copy(i + 1, (i + 1) % 2, t, 0).start()
            row_copy(i + 1, (i + 1) % 2, t, 1).start()

    slot = i % 2
    pltpu.make_async_copy(y_hbm.at[pl.ds(0, 2 * CHUNK), :], ybuf.at[slot], sem.at[slot]).wait()

    rt = r_ref[...]
    y = rt[:, 2:3] * ybuf[slot, 0:CHUNK, :] + rt[:, 3:4] * ybuf[slot, CHUNK:2 * CHUNK, :]
    is_ctx = ((i + row_block0) * CHUNK + lax.broadcasted_iota(jnp.int32, (CHUNK, 1), 0)) < CTX
    gate2 = jnp.where(is_ctx, g_ref[1:2, :], g_ref[0:1, :])
    x = x_ref[...] + gate2 * y
    if final_norm:
        x = x * lax.rsqrt(jnp.mean(x * x, axis=-1, keepdims=True) + EPS) * fw_ref[...]
    o_ref[...] = x


def moe_combine(dest, xa, route, g2, final_w, y, final_norm):
    t_all, d = xa.shape
    rb0 = CTX // CHUNK if final_norm else 0
    n_out = t_all // CHUNK - rb0
    return pl.pallas_call(
        functools.partial(_combine_kernel, row_block0=rb0, n_tiles=n_out, final_norm=final_norm),
        grid_spec=pltpu.PrefetchScalarGridSpec(
            num_scalar_prefetch=1,
            grid=(n_out,),
            in_specs=[pl.BlockSpec((CHUNK, d), lambda i, ds: (i + rb0, 0)),
                      pl.BlockSpec((CHUNK, LANES), lambda i, ds: (i + rb0, 0)),
                      pl.BlockSpec((2, d), lambda i, ds: (0, 0)),
                      pl.BlockSpec((1, d), lambda i, ds: (0, 0)),
                      pl.BlockSpec(memory_space=pl.ANY)],
            out_specs=pl.BlockSpec((CHUNK, d), lambda i, ds: (i, 0)),
            scratch_shapes=[pltpu.VMEM((2, 2 * CHUNK, d), F32),
                            pltpu.SemaphoreType.DMA((2,))]),
        out_shape=jax.ShapeDtypeStruct((n_out * CHUNK, d), F32),
        compiler_params=_cparams(("arbitrary",)),
        name="moe_combine",
    )(dest, xa, route, g2, final_w, y)


def moe_dispatch(route, counts):
    n = route.shape[0]
    n_assign = 2 * n
    e_flat = route[:, 0:2].astype(jnp.int32).reshape(-1)
    rank_flat = route[:, 4:6].astype(jnp.int32).reshape(-1)
    padded = (counts + MOE_BM - 1) // MOE_BM * MOE_BM
    pends = jnp.cumsum(padded)
    pstarts = pends - padded
    experts = jnp.arange(N_EXPERTS, dtype=jnp.int32)
    dest = jnp.sum(jnp.where(e_flat[:, None] == experts[None, :], pstarts[None, :], 0), axis=1) + rank_flat
    n_blocks = -(-n_assign // MOE_BM) + N_EXPERTS
    n_slots = n_blocks * MOE_BM
    tok_flat = jnp.arange(n_assign, dtype=jnp.int32) // 2
    slot_tok = jnp.zeros((n_slots,), jnp.int32).at[dest].set(tok_flat)
    block_start = jnp.arange(n_blocks, dtype=jnp.int32) * MOE_BM
    block_e = jnp.minimum(jnp.sum((pends[None, :] <= block_start[:, None]).astype(jnp.int32), axis=1),
                          N_EXPERTS - 1)
    n_used = (pends[-1] // MOE_BM).astype(jnp.int32).reshape(1)

    nb = padded // MOE_BM
    nonempty = nb > 0
    order = jnp.cumsum(nonempty.astype(jnp.int32)) - 1
    later = nonempty[None, :] & (experts[None, :] > experts[:, None])
    nxt_e = jnp.min(jnp.where(later, experts[None, :], N_EXPERTS), axis=1)
    nxt_e = jnp.where(nxt_e == N_EXPERTS, -1, nxt_e)
    first_blk = pstarts // MOE_BM
    i_blk = jnp.arange(n_blocks, dtype=jnp.int32) - first_blk[block_e]
    nb_b = jnp.maximum(nb[block_e], 1)
    live = (jnp.arange(n_blocks) < n_used[0]) & (nxt_e[block_e] >= 0)
    c0 = jnp.where(live, W_CHUNKS * i_blk // nb_b, 0)
    c1 = jnp.where(live, W_CHUNKS * (i_blk + 1) // nb_b, 0)
    sched = jnp.stack([order[block_e] % 2, nxt_e[block_e], c0, c1], axis=1).astype(jnp.int32).reshape(-1)
    return dest, slot_tok, block_e, n_used, sched


def hier_moe(xa, nw, sc2, sh2, g2, w_route, b_route, w1, w3, w2, final_w, layer, final_norm):
    h2, route, cnt = moe_router(xa, nw, sc2, sh2, w_route, b_route, layer)
    dest, slot_tok, block_e, n_used, sched = moe_dispatch(route, cnt[0, :N_EXPERTS].astype(jnp.int32))
    y = expert_blocks(block_e, n_used, slot_tok, sched, h2, w1, w3, w2, layer)
    return moe_combine(dest, xa, route, g2, final_w, y, final_norm)


def rope_tables(t_lat):
    nf = ML_DH // 4
    inv = ROPE_THETA ** (-jnp.arange(nf, dtype=F32) / nf)
    t_idx = jnp.arange(t_lat)
    ang_r = (t_idx // GRID_W).astype(F32)[:, None] * inv
    ang_c = (t_idx % GRID_W).astype(F32)[:, None] * inv
    cos = jnp.concatenate([jnp.cos(ang_r), jnp.cos(ang_c)] * 2, axis=-1)
    sin = jnp.concatenate([-jnp.sin(ang_r), -jnp.sin(ang_c), jnp.sin(ang_r), jnp.sin(ang_c)], axis=-1)
    cos = jnp.concatenate([jnp.ones((CTX, ML_DH), F32), cos], axis=0)
    sin = jnp.concatenate([jnp.zeros((CTX, ML_DH), F32), sin], axis=0)
    return cos, sin


def _pad_lanes(v):
    return jnp.pad(v, [(0, 0)] * (v.ndim - 1) + [(0, LANES - v.shape[-1])])


def kernel(x, c, ctx, c_ctx, w_mod, b_mod, norm1_w, norm2_w, w_in, w_out, na_rpb, ml_gate_b, ml_norm_w, ssm_conv_w, ssm_conv_b, ssm_dt_bias, ssm_a_log, ssm_d, ssm_norm_w, router_g_w, router_g_b, router_e_w, router_e_b, moe_w1, moe_w3, moe_w2, final_norm_w):
    depth = w_mod.shape[0]
    t_lat = x.shape[1]
    d = D_MODEL
    xa = jnp.concatenate([ctx[0], x[0]], axis=0)

    cond = jnp.zeros((8, d), F32).at[0].set(c[0]).at[1].set(c_ctx)
    mod = adaln_modulation(cond, w_mod, b_mod)[:, 0:2, :].reshape(depth, 2, 6, d)
    cos_tab, sin_tab = rope_tables(t_lat)

    o_mq = 3 * NA_W
    o_mg = o_mq + 4 * ML_W
    o_z = o_mg + 4 * ML_H
    o_xbc = o_z + SSM_W
    o_dt = o_xbc + CONV_CH
    w_qk = w_in[:, :, o_mq:o_mq + 2 * ML_W].reshape(depth, d, 2 * ML_H, 2, 2, ML_DH // 4)
    w_qk = jnp.swapaxes(w_qk, 3, 4).reshape(depth, d, 2 * ML_W)
    w_main = jnp.concatenate([w_in[:, :, 0:o_mq], w_in[:, :, o_xbc:o_dt], w_qk, w_in[:, :, o_mq + 2 * ML_W:o_mg],
                              w_in[:, :, o_z:o_xbc]], axis=2).astype(BF16)
    w_mg, w_dt = w_in[:, :, o_mg:o_z], w_in[:, :, o_dt:]
    w_gate = _pad_lanes(jnp.concatenate([w_mg[:, :, 0:8], w_dt[:, :, 0:8], w_mg[:, :, 8:16], w_dt[:, :, 8:16]],
                                        axis=2))
    w_out_b = w_out.astype(BF16)
    w_route = _pad_lanes(jnp.concatenate([router_g_w, router_e_w], axis=2))
    b_route = _pad_lanes(jnp.concatenate([router_g_b, router_e_b], axis=1))[:, None, :]
    gb4 = ml_gate_b.reshape(depth, 2, 2 * ML_H)
    gbias = _pad_lanes(jnp.concatenate([gb4, ssm_dt_bias], axis=2))
    alog = _pad_lanes(jnp.concatenate([jnp.zeros_like(ssm_a_log), ssm_a_log], axis=2))
    d_skip_vec = jnp.repeat(ssm_d, SSM_P, axis=1)[:, None, :]
    conv_b = ssm_conv_b[:, None, :]

    out = None
    for l in range(depth):
        last = l == depth - 1
        sh1, sc1, g1, sh2, sc2, g2 = (mod[l, :, p, :] for p in range(6))

        qkv, rest, gates, gates_t = in_projection(xa, norm1_w[l][None], sc1, sh1, w_main, w_gate, l)
        o_na = neighbourhood_attention(qkv, na_bias_table(na_rpb[l]))
        hs = mlstm_scan(rest, gates, gates_t, gbias[l][:, None, :], gbias[l][:, :, None], cos_tab, sin_tab)
        xbc_act = ssd_conv(rest, ssm_conv_w, conv_b, l)
        ys = ssd_scan(xbc_act, gates, gates_t, gbias[l][:, None, :], gbias[l][:, :, None],
                      alog[l][:, None, :], alog[l][:, :, None])
        xa = out_projection(o_na, hs, rest, ys, xbc_act, ml_norm_w[:, None, :], d_skip_vec,
                            ssm_norm_w[:, None, :], w_out_b, xa, g1, l)
        res = hier_moe(xa, norm2_w[l][None], sc2, sh2, g2, w_route, b_route, moe_w1, moe_w3, moe_w2,
                       final_norm_w[None], l, last)
        if last:
            out = res
        else:
            xa = res
    return out[None]
```

```python
import functools

import jax
import jax.numpy as jnp
import numpy as np
from jax import lax
from jax.experimental import pallas as pl
from jax.experimental.pallas import tpu as pltpu

F32 = jnp.float32
BF16 = jnp.bfloat16
HIGHEST = lax.Precision.HIGHEST

D_MODEL = 2048
GRID_W = 64
CTX = 256
NA_DH = 128
NA_W = 1024
NA_H = 8
NA_KR = 8
NA_KC = 16
ML_DH = 128
ML_W = 512
ML_H = 4
SSM_P = 64
SSM_W = 512
SSM_H = 8
SSM_G = 2
SSM_N = 128
CONV_CH = SSM_W + 2 * SSM_G * SSM_N
MOE_GROUPS = 4
MOE_PER_GROUP = 8
N_EXPERTS = 32
MOE_FF = 1024
ROPE_THETA = 10000.0
EPS = 1e-6

CHUNK = 256
LANES = 128
NEG = -1e30
MOE_BM = 256
VMEM_LIMIT = 56 * 1024 * 1024

R_XBC, R_MQ, R_MK, R_MV, R_MO, R_Z = 0, 1024, 1536, 2048, 2560, 3072
REST_W = 3584
QKV_W = 3 * NA_W
G_I, G_F, G_DT = 0, 4, 8
G_DIR = 16


def _cparams(sem):
    return pltpu.CompilerParams(dimension_semantics=sem, vmem_limit_bytes=VMEM_LIMIT)


def _silu(x):
    return x / (1.0 + jnp.exp(-x))


def _softplus(x):
    return jnp.maximum(x, 0.0) + jnp.log1p(jnp.exp(-jnp.abs(x)))


def _dot(a, b):
    return jnp.dot(a, b, preferred_element_type=F32)


def _dot_nt(a, b):
    return lax.dot_general(a, b, (((1,), (1,)), ((), ())), preferred_element_type=F32)


def _dot_tn(a, b):
    return lax.dot_general(a, b, (((0,), (0,)), ((), ())), preferred_element_type=F32)


def _split2(x):
    hi = x.astype(BF16)
    return hi, (x - hi.astype(F32)).astype(BF16)


def _dot_split(a, w_hi, w_lo):
    a_hi, a_lo = _split2(a)
    return _dot(a_hi, w_hi) + _dot(a_hi, w_lo) + _dot(a_lo, w_hi)


def _mod_kernel(c_ref, w_ref, b_ref, o_ref):
    o_ref[0] = jnp.dot(_silu(c_ref[...]), w_ref[0], preferred_element_type=F32, precision=HIGHEST) + b_ref[0]


def adaln_modulation(cond, w_mod, b_mod):
    depth, d, n = w_mod.shape
    tn = 512
    return pl.pallas_call(
        _mod_kernel,
        grid=(depth, n // tn),
        in_specs=[pl.BlockSpec((8, d), lambda l, j: (0, 0)),
                  pl.BlockSpec((1, d, tn), lambda l, j: (l, 0, j)),
                  pl.BlockSpec((1, 1, tn), lambda l, j: (l, 0, j))],
        out_specs=pl.BlockSpec((1, 8, tn), lambda l, j: (l, 0, j)),
        out_shape=jax.ShapeDtypeStruct((depth, 8, n), F32),
        compiler_params=_cparams(("parallel", "parallel")),
        name="adaln_modulation",
    )(cond, w_mod, b_mod.reshape(depth, 1, n))


def _modulated_norm(x, nw, sc2, sh2, row0):
    r = x.shape[0]
    y = x * lax.rsqrt(jnp.mean(x * x, axis=-1, keepdims=True) + EPS) * nw
    is_ctx = (row0 + lax.broadcasted_iota(jnp.int32, (r, 1), 0)) < CTX
    sc = jnp.where(is_ctx, sc2[1:2, :], sc2[0:1, :])
    sh = jnp.where(is_ctx, sh2[1:2, :], sh2[0:1, :])
    return y * (1.0 + sc) + sh


IN_TM = 1280
IN_TN = 512
IN_SUB = 256


def _in_proj_kernel(x_ref, nw_ref, sc_ref, sh_ref, wm_ref, wg_ref, qkv_ref, rest_ref, gate_ref, gate_t_ref, a_scr):
    i = pl.program_id(0)
    j = pl.program_id(1)
    n_qkv = QKV_W // IN_TN

    @pl.when(j == 0)
    def _():
        wg_hi, wg_lo = _split2(wg_ref[0])
        for r in range(IN_TM // IN_SUB):
            rows = slice(r * IN_SUB, (r + 1) * IN_SUB)
            h = _modulated_norm(x_ref[rows, :], nw_ref[...], sc_ref[...], sh_ref[...], i * IN_TM + r * IN_SUB)
            a_scr[rows, :] = h.astype(BF16)
            gate = _dot_split(h, wg_hi, wg_lo)
            gate_ref[rows, :] = gate
            gate_t_ref[:, rows] = gate.T

    acc = _dot(a_scr[...], wm_ref[0])

    @pl.when(j < n_qkv)
    def _():
        qkv_ref[...] = acc.astype(BF16)

    @pl.when(j >= n_qkv)
    def _():
        rest_ref[...] = acc


def in_projection(xa, nw, sc2, sh2, w_main, w_gate, layer):
    t_all, d = xa.shape
    n_main = w_main.shape[2]
    n_qkv = QKV_W // IN_TN
    return pl.pallas_call(
        _in_proj_kernel,
        grid=(t_all // IN_TM, n_main // IN_TN),
        in_specs=[pl.BlockSpec((IN_TM, d), lambda i, j: (i, 0)),
                  pl.BlockSpec((1, d), lambda i, j: (0, 0)),
                  pl.BlockSpec((2, d), lambda i, j: (0, 0)),
                  pl.BlockSpec((2, d), lambda i, j: (0, 0)),
                  pl.BlockSpec((1, d, IN_TN), lambda i, j: (layer, 0, j)),
                  pl.BlockSpec((1, d, LANES), lambda i, j: (layer, 0, 0))],
        out_specs=[pl.BlockSpec((IN_TM, IN_TN), lambda i, j: (i, jnp.minimum(j, n_qkv - 1))),
                   pl.BlockSpec((IN_TM, IN_TN), lambda i, j: (i, jnp.maximum(j - n_qkv, 0))),
                   pl.BlockSpec((IN_TM, LANES), lambda i, j: (i, 0)),
                   pl.BlockSpec((LANES, IN_TM), lambda i, j: (0, i))],
        out_shape=[jax.ShapeDtypeStruct((t_all, QKV_W), BF16),
                   jax.ShapeDtypeStruct((t_all, REST_W), F32),
                   jax.ShapeDtypeStruct((t_all, LANES), F32),
                   jax.ShapeDtypeStruct((LANES, t_all), F32)],
        scratch_shapes=[pltpu.VMEM((IN_TM, d), BF16)],
        compiler_params=_cparams(("arbitrary", "arbitrary")),
        name="in_projection",
    )(xa, nw, sc2, sh2, w_main, w_gate)


NA_RB = CHUNK // GRID_W
NA_UR = 12


NA_HB = 2


def _na_kernel(q_ref, k_ref, v_ref, bias_ref, o_ref, *, n_rows):
    rb = pl.program_id(1)
    scale = NA_DH ** -0.5

    @pl.when(rb == 0)
    def _():
        for hh in range(NA_HB):
            hs = slice(hh * NA_DH, (hh + 1) * NA_DH)
            s = _dot_nt(q_ref[:, hs], k_ref[0:CTX, hs]) * scale
            m = jnp.max(s, axis=-1, keepdims=True)
            p = jnp.exp(s - m)
            l = jnp.sum(p, axis=-1, keepdims=True)
            o_ref[:, hs] = (_dot(p.astype(BF16), v_ref[0:CTX, hs]) / l).astype(o_ref.dtype)

    @pl.when(rb > 0)
    def _():
        r0 = (rb - 1) * NA_RB
        u0 = jnp.clip(r0 - NA_KR // 2, 0, n_rows - NA_UR)
        case = jnp.where(r0 == 0, 1, jnp.where(r0 == n_rows - NA_RB, 2, 0))
        start = pl.multiple_of(CTX + u0 * GRID_W, GRID_W)
        for hh in range(NA_HB):
            hs = slice(hh * NA_DH, (hh + 1) * NA_DH)
            q = q_ref[:, hs]
            kc = k_ref[0:CTX, hs]
            vc = v_ref[0:CTX, hs]
            kw = k_ref[pl.ds(start, NA_UR * GRID_W), hs]
            vw = v_ref[pl.ds(start, NA_UR * GRID_W), hs]
            s = _dot_nt(q, kw) * scale + bias_ref[hh, case]
            sc = _dot_nt(q, kc) * scale
            m = jnp.maximum(jnp.max(s, axis=-1, keepdims=True), jnp.max(sc, axis=-1, keepdims=True))
            p = jnp.exp(s - m)
            pc = jnp.exp(sc - m)
            l = jnp.sum(p, axis=-1, keepdims=True) + jnp.sum(pc, axis=-1, keepdims=True)
            o = _dot(p.astype(BF16), vw) + _dot(pc.astype(BF16), vc)
            o_ref[:, hs] = (o / l).astype(o_ref.dtype)


def na_bias_table(rpb):
    h = rpb.shape[0]
    c = np.arange(GRID_W)
    cs = np.clip(c - NA_KC // 2, 0, GRID_W - NA_KC)
    kcol = np.arange(GRID_W)
    inside = (kcol[None, :] >= cs[:, None]) & (kcol[None, :] < cs[:, None] + NA_KC)
    off = kcol[None, :] - c[:, None] + NA_KC - 1
    onehot = (off[None] == np.arange(2 * NA_KC - 1)[:, None, None]) & inside[None]
    band = jnp.einsum('hro,ock->hrck', rpb.astype(F32), jnp.asarray(onehot, F32), precision=HIGHEST)
    band = band + jnp.asarray(np.where(inside, 0.0, NEG), F32)
    neg = jnp.full((h, GRID_W, GRID_W), NEG, F32)
    cases = ([(rr, NA_KR // 2 - 1) for rr in range(NA_RB)],
             [(0, NA_KR - 1 - rr) for rr in range(NA_RB)],
             [(NA_UR - NA_KR, NA_KR // 2 - 1 - rr) for rr in range(NA_RB)])
    tabs = []
    for case in cases:
        rows = []
        for w_off, d0 in case:
            rows.append(jnp.concatenate(
                [band[:, d0 + u - w_off] if 0 <= u - w_off < NA_KR else neg for u in range(NA_UR)], axis=-1))
        tabs.append(jnp.concatenate(rows, axis=1))
    return jnp.stack(tabs, axis=1)


def neighbourhood_attention(qkv, bias_tab):
    t_all = qkv.shape[0]
    n_rows = (t_all - CTX) // GRID_W
    n_rb = t_all // CHUNK
    return pl.pallas_call(
        functools.partial(_na_kernel, n_rows=n_rows),
        grid=(NA_H // NA_HB, n_rb),
        in_specs=[pl.BlockSpec((CHUNK, NA_HB * NA_DH), lambda h, rb: (rb, h)),
                  pl.BlockSpec((t_all, NA_HB * NA_DH), lambda h, rb: (0, NA_H // NA_HB + h)),
                  pl.BlockSpec((t_all, NA_HB * NA_DH), lambda h, rb: (0, 2 * (NA_H // NA_HB) + h)),
                  pl.BlockSpec((NA_HB, 3, CHUNK, NA_UR * GRID_W), lambda h, rb: (h, 0, 0, 0))],
        out_specs=pl.BlockSpec((CHUNK, NA_HB * NA_DH), lambda h, rb: (rb, h)),
        out_shape=jax.ShapeDtypeStruct((t_all, NA_W), BF16),
        compiler_params=_cparams(("parallel", "parallel")),
        name="neighbourhood_attention",
    )(qkv, qkv, qkv, bias_tab)


def _scan_chunk(d, s, n_chunks):
    return jnp.where(d == 0, s, jnp.where(s == 0, 0, n_chunks - s))


def _scan_masks(d):
    row = lax.broadcasted_iota(jnp.int32, (CHUNK, CHUNK), 0)
    col = lax.broadcasted_iota(jnp.int32, (CHUNK, CHUNK), 1)
    mask = jnp.where(d == 0, row - col, col - row) >= 0
    return mask, mask.astype(BF16)


def _split3(x):
    hi = x.astype(BF16)
    r = x - hi.astype(F32)
    mid = r.astype(BF16)
    return hi, mid, (r - mid.astype(F32)).astype(BF16)


def _masked_cumsum(mb, x, xt):
    cum = sum(_dot(mb, p) for p in _split3(x))
    cumt = sum(_dot_nt(p, mb) for p in _split3(xt))
    return cum, cumt


def _direction_gates(d, g_ref, gt_ref):
    g = g_ref[...]
    g = jnp.where(d == 0, g, pltpu.roll(g, LANES - G_DIR, 1))
    gt = jnp.where(d == 0, gt_ref[0:G_DIR, :], gt_ref[G_DIR:2 * G_DIR, :])
    return g, gt


def _rope(x, cos, sin_signed):
    return x * cos + pltpu.roll(x, ML_DH // 2, 1) * sin_signed


def _mlstm_kernel(q_ref, k_ref, v_ref, g_ref, gt_ref, gb_ref, gbt_ref, cos_ref, sin_ref, o_ref,
                  c_scr, n_scr, m_scr):
    d = pl.program_id(0)
    s = pl.program_id(1)

    @pl.when(s == 0)
    def _():
        c_scr[...] = jnp.zeros_like(c_scr)
        n_scr[...] = jnp.zeros_like(n_scr)
        m_scr[...] = jnp.zeros_like(m_scr)

    mask, mf = _scan_masks(d)
    g, gt = _direction_gates(d, g_ref, gt_ref)
    g = g + gb_ref[0]
    gt = gt + gbt_ref[0, 0:G_DIR]
    lf = -_softplus(-g)
    lft = -_softplus(-gt)
    cum, cumt = _masked_cumsum(mf, lf, lft)
    tot = jnp.sum(lf, axis=0, keepdims=True)
    cos = cos_ref[...]
    sin = sin_ref[...]

    for h in range(ML_H):
        hs = slice(h * ML_DH, (h + 1) * ML_DH)
        q = _rope(q_ref[:, hs], cos, sin)
        k = _rope(k_ref[:, hs], cos, sin) * (ML_DH ** -0.5)
        v = v_ref[:, hs]
        qb, kb, vb = q.astype(BF16), k.astype(BF16), v.astype(BF16)
        bt_col = cum[:, G_F + h:G_F + h + 1]
        bt_row = cumt[G_F + h:G_F + h + 1, :]
        ig_col = g[:, G_I + h:G_I + h + 1]
        ig_row = gt[G_I + h:G_I + h + 1, :]
        b_last = tot[:, G_F + h:G_F + h + 1]
        m_prev = m_scr[h]
        cmat = c_scr[h]
        nvec = n_scr[h]

        log_d = jnp.where(mask, bt_col - bt_row + ig_row, -jnp.inf)
        inter = bt_col + m_prev
        m_t = jnp.maximum(jnp.max(log_d, axis=-1, keepdims=True), inter)
        wts = _dot_nt(qb, kb) * jnp.exp(log_d - m_t)
        sc = jnp.exp(inter - m_t)
        num = _dot(wts.astype(BF16), vb) + _dot_nt(qb, cmat.astype(BF16)) * sc
        den = jnp.sum(wts, axis=-1, keepdims=True) + jnp.sum(q * nvec, axis=-1, keepdims=True) * sc
        den = jnp.maximum(jnp.abs(den), jnp.exp(-m_t))
        o_ref[0, :, hs] = num / den

        tail = b_last - bt_col + ig_col
        m_new = jnp.maximum(b_last + m_prev, jnp.max(tail, axis=0, keepdims=True))
        wgt = jnp.exp(tail - m_new)
        decay = jnp.exp(b_last + m_prev - m_new)
        c_scr[h] = decay * cmat + _dot_tn((v * wgt).astype(BF16), kb)
        n_scr[h] = decay * nvec + jnp.sum(wgt * k, axis=0, keepdims=True)
        m_scr[h] = m_new


def mlstm_scan(rest, gates, gates_t, gbias, gbias_t, cos_tab, sin_tab):
    t_all = rest.shape[0]
    n_chunks = t_all // CHUNK
    cm = lambda d, s: _scan_chunk(d, s, n_chunks)
    col = lambda off: off // ML_W
    return pl.pallas_call(
        _mlstm_kernel,
        grid=(2, n_chunks),
        in_specs=[pl.BlockSpec((CHUNK, ML_W), lambda d, s: (cm(d, s), col(R_MQ))),
                  pl.BlockSpec((CHUNK, ML_W), lambda d, s: (cm(d, s), col(R_MK))),
                  pl.BlockSpec((CHUNK, ML_W), lambda d, s: (cm(d, s), col(R_MV))),
                  pl.BlockSpec((CHUNK, LANES), lambda d, s: (cm(d, s), 0)),
                  pl.BlockSpec((LANES, CHUNK), lambda d, s: (0, cm(d, s))),
                  pl.BlockSpec((1, 1, LANES), lambda d, s: (d, 0, 0)),
                  pl.BlockSpec((1, LANES, 1), lambda d, s: (d, 0, 0)),
                  pl.BlockSpec((CHUNK, ML_DH), lambda d, s: (cm(d, s), 0)),
                  pl.BlockSpec((CHUNK, ML_DH), lambda d, s: (cm(d, s), 0))],
        out_specs=pl.BlockSpec((1, CHUNK, ML_W), lambda d, s: (d, cm(d, s), 0)),
        out_shape=jax.ShapeDtypeStruct((2, t_all, ML_W), F32),
        scratch_shapes=[pltpu.VMEM((ML_H, ML_DH, ML_DH), F32),
                        pltpu.VMEM((ML_H, 1, ML_DH), F32),
                        pltpu.VMEM((ML_H, 1, 1), F32)],
        compiler_params=_cparams(("arbitrary", "arbitrary")),
        name="mlstm_scan",
    )(rest, rest, rest, gates, gates_t, gbias, gbias_t, cos_tab, sin_tab)


CONV_HALO = 8


def _conv_kernel(x_ref, p_ref, n_ref, w_ref, b_ref, o_ref, *, n_chunks):
    s = pl.program_id(0)
    x = x_ref[...]
    row = lax.broadcasted_iota(jnp.int32, x.shape, 0)
    prev = jnp.where(s >= 2, p_ref[...], 0.0)
    nxt = jnp.where((s >= 1) & (s <= n_chunks - 2), n_ref[...], 0.0)
    xm1 = jnp.where(row == 0, prev[CONV_HALO - 1:CONV_HALO, :], pltpu.roll(x, 1, 0))
    xm2 = jnp.where(row == 0, prev[CONV_HALO - 2:CONV_HALO - 1, :],
                    jnp.where(row == 1, prev[CONV_HALO - 1:CONV_HALO, :], pltpu.roll(x, 2, 0)))
    xp1 = jnp.where(row == CHUNK - 1, nxt[0:1, :], pltpu.roll(x, CHUNK - 1, 0))
    w = w_ref[0]
    y = w[0:1, :] * xm2 + w[1:2, :] * xm1 + w[2:3, :] * x + w[3:4, :] * xp1 + b_ref[0]
    o_ref[...] = _silu(y)


def ssd_conv(rest, conv_w, conv_b, layer):
    t_all = rest.shape[0]
    n_chunks = t_all // CHUNK
    hb = CHUNK // CONV_HALO
    return pl.pallas_call(
        functools.partial(_conv_kernel, n_chunks=n_chunks),
        grid=(n_chunks,),
        in_specs=[pl.BlockSpec((CHUNK, CONV_CH), lambda s: (s, 0)),
                  pl.BlockSpec((CONV_HALO, CONV_CH), lambda s: (jnp.maximum(s * hb - 1, 0), 0)),
                  pl.BlockSpec((CONV_HALO, CONV_CH), lambda s: (jnp.minimum((s + 1) * hb, n_chunks * hb - 1), 0)),
                  pl.BlockSpec((1, 4, CONV_CH), lambda s: (layer, 0, 0)),
                  pl.BlockSpec((1, 1, CONV_CH), lambda s: (layer, 0, 0))],
        out_specs=pl.BlockSpec((CHUNK, CONV_CH), lambda s: (s, 0)),
        out_shape=jax.ShapeDtypeStruct((t_all, CONV_CH), F32),
        compiler_params=_cparams(("parallel",)),
        name="ssd_conv",
    )(rest, rest, rest, conv_w, conv_b)


def _ssd_kernel(x_ref, g_ref, gt_ref, gb_ref, gbt_ref, al_ref, alt_ref, o_ref, s_scr):
    d = pl.program_id(0)
    s = pl.program_id(1)

    @pl.when(s == 0)
    def _():
        s_scr[...] = jnp.zeros_like(s_scr)

    mask, mf = _scan_masks(d)
    g, gt = _direction_gates(d, g_ref, gt_ref)
    dt = _softplus(g + gb_ref[0])
    dtt = _softplus(gt + gbt_ref[0, 0:G_DIR])
    inc = dt * (-jnp.exp(al_ref[0]))
    inct = dtt * (-jnp.exp(alt_ref[0, 0:G_DIR]))
    cum, cumt = _masked_cumsum(mf, inc, inct)
    tot = jnp.sum(inc, axis=0, keepdims=True)
    e_cum = jnp.exp(cum)
    e_tail = jnp.exp(tot - cum)
    e_tot = jnp.exp(tot)

    first = lax.broadcasted_iota(jnp.int32, (CHUNK, 2 * SSM_P), 1) < SSM_P
    first_row = lax.broadcasted_iota(jnp.int32, (2 * SSM_P, 1), 0) < SSM_P
    col = lambda a, h: a[:, G_DT + h:G_DT + h + 1]
    hpg = SSM_H // SSM_G
    for gi in range(SSM_G):
        bm = x_ref[:, SSM_W + gi * SSM_N:SSM_W + (gi + 1) * SSM_N].astype(BF16)
        cm = x_ref[:, SSM_W + (SSM_G + gi) * SSM_N:SSM_W + (SSM_G + gi + 1) * SSM_N].astype(BF16)
        gmat = _dot_nt(cm, bm)
        for pp in range(hpg // 2):
            h0 = gi * hpg + 2 * pp
            h1 = h0 + 1
            cols = slice(h0 * SSM_P, (h0 + 2) * SSM_P)
            xdt = x_ref[:, cols] * jnp.where(first, col(dt, h0), col(dt, h1))
            xdt_b = xdt.astype(BF16)
            state = s_scr[h0 // 2]
            ys = []
            for h in (h0, h1):
                decay = jnp.exp(jnp.where(mask, col(cum, h) - cumt[G_DT + h:G_DT + h + 1, :], -jnp.inf))
                ys.append(_dot((gmat * decay).astype(BF16), xdt_b))
            carried = _dot_nt(cm, state.astype(BF16)) * jnp.where(first, col(e_cum, h0), col(e_cum, h1))
            o_ref[0, :, cols] = jnp.where(first, ys[0], ys[1]) + carried
            tail = jnp.where(first, col(e_tail, h0), col(e_tail, h1))
            keep = jnp.where(first_row, col(e_tot, h0), col(e_tot, h1))
            s_scr[h0 // 2] = state * keep + _dot_tn((xdt * tail).astype(BF16), bm)


def ssd_scan(xbc_act, gates, gates_t, gbias, gbias_t, alog, alog_t):
    t_all = xbc_act.shape[0]
    n_chunks = t_all // CHUNK
    cm = lambda d, s: _scan_chunk(d, s, n_chunks)
    return pl.pallas_call(
        _ssd_kernel,
        grid=(2, n_chunks),
        in_specs=[pl.BlockSpec((CHUNK, CONV_CH), lambda d, s: (cm(d, s), 0)),
                  pl.BlockSpec((CHUNK, LANES), lambda d, s: (cm(d, s), 0)),
                  pl.BlockSpec((LANES, CHUNK), lambda d, s: (0, cm(d, s))),
                  pl.BlockSpec((1, 1, LANES), lambda d, s: (d, 0, 0)),
                  pl.BlockSpec((1, LANES, 1), lambda d, s: (d, 0, 0)),
                  pl.BlockSpec((1, 1, LANES), lambda d, s: (d, 0, 0)),
                  pl.BlockSpec((1, LANES, 1), lambda d, s: (d, 0, 0))],
        out_specs=pl.BlockSpec((1, CHUNK, SSM_W), lambda d, s: (d, cm(d, s), 0)),
        out_shape=jax.ShapeDtypeStruct((2, t_all, SSM_W), F32),
        scratch_shapes=[pltpu.VMEM((SSM_H // 2, 2 * SSM_P, SSM_N), F32)],
        compiler_params=_cparams(("arbitrary", "arbitrary")),
        name="ssd_scan",
    )(xbc_act, gates, gates_t, gbias, gbias_t, alog, alog_t)


OUT_TM = 640
OUT_TN = 1024
OUT_SUB = 128


def _out_proj_kernel(ona_ref, hs_ref, mo_ref, ys_ref, xs_ref, z_ref, mlw_ref, dsk_ref, ssw_ref,
                     w_ref, x_ref, g_ref, o_ref, a_scr):
    i = pl.program_id(0)
    j = pl.program_id(1)

    @pl.when(j == 0)
    def _():
        def body(r, carry):
            rows = pl.ds(pl.multiple_of(r * OUT_SUB, OUT_SUB), OUT_SUB)
            a_scr[rows, 0:NA_W] = ona_ref[rows, :]
            hsum = hs_ref[0, rows, :] + hs_ref[1, rows, :]
            gate = 1.0 / (1.0 + jnp.exp(-mo_ref[rows, :]))
            for h in range(ML_H):
                cs = slice(h * ML_DH, (h + 1) * ML_DH)
                hh = hsum[:, cs]
                mu = jnp.mean(hh, axis=-1, keepdims=True)
                var = jnp.mean(jnp.square(hh - mu), axis=-1, keepdims=True)
                hn = (hh - mu) * lax.rsqrt(var + EPS) * mlw_ref[0, :, cs]
                a_scr[rows, NA_W + h * ML_DH:NA_W + (h + 1) * ML_DH] = (gate[:, cs] * hn).astype(BF16)
            y = ys_ref[0, rows, :] + ys_ref[1, rows, :] + dsk_ref[0] * xs_ref[rows, :]
            y = y * _silu(z_ref[rows, :])
            gw = SSM_W // SSM_G
            for gi in range(SSM_G):
                cs = slice(gi * gw, (gi + 1) * gw)
                yg = y[:, cs]
                yn = yg * lax.rsqrt(jnp.mean(yg * yg, axis=-1, keepdims=True) + EPS) * ssw_ref[0, :, cs]
                a_scr[rows, NA_W + ML_W + gi * gw:NA_W + ML_W + (gi + 1) * gw] = yn.astype(BF16)
            return carry
        lax.fori_loop(0, OUT_TM // OUT_SUB, body, 0)

    acc = _dot(a_scr[...], w_ref[0])
    is_ctx = (i * OUT_TM + lax.broadcasted_iota(jnp.int32, (OUT_TM, 1), 0)) < CTX
    gate1 = jnp.where(is_ctx, g_ref[1:2, :], g_ref[0:1, :])
    o_ref[...] = x_ref[...] + gate1 * acc


def out_projection(o_na, hs, rest, ys, xbc_act, ml_norm_w, d_skip_vec, ssm_norm_w, w_out, xa, g1, layer):
    t_all, d = xa.shape
    cw = lambda off: off // ML_W
    vec = pl.BlockSpec((1, 1, ML_W), lambda i, j: (layer, 0, 0))
    return pl.pallas_call(
        _out_proj_kernel,
        grid=(t_all // OUT_TM, d // OUT_TN),
        in_specs=[pl.BlockSpec((OUT_TM, NA_W), lambda i, j: (i, 0)),
                  pl.BlockSpec((2, OUT_TM, ML_W), lambda i, j: (0, i, 0)),
                  pl.BlockSpec((OUT_TM, ML_W), lambda i, j: (i, cw(R_MO))),
                  pl.BlockSpec((2, OUT_TM, SSM_W), lambda i, j: (0, i, 0)),
                  pl.BlockSpec((OUT_TM, SSM_W), lambda i, j: (i, 0)),
                  pl.BlockSpec((OUT_TM, SSM_W), lambda i, j: (i, cw(R_Z))),
                  vec, vec, vec,
                  pl.BlockSpec((1, d, OUT_TN), lambda i, j: (layer, 0, j)),
                  pl.BlockSpec((OUT_TM, OUT_TN), lambda i, j: (i, j)),
                  pl.BlockSpec((2, OUT_TN), lambda i, j: (0, j))],
        out_specs=pl.BlockSpec((OUT_TM, OUT_TN), lambda i, j: (i, j)),
        out_shape=jax.ShapeDtypeStruct((t_all, d), F32),
        scratch_shapes=[pltpu.VMEM((OUT_TM, d), BF16)],
        compiler_params=_cparams(("arbitrary", "arbitrary")),
        name="out_projection",
    )(o_na, hs, rest, ys, xbc_act, rest, ml_norm_w, d_skip_vec, ssm_norm_w, w_out, xa, g1)


RT_TM = 640
RT_SUB = 128


def _router_kernel(x_ref, nw_ref, sc_ref, sh_ref, wr_ref, br_ref, h_ref, r_ref, cnt_ref, run_scr):
    i = pl.program_id(0)

    @pl.when(i == 0)
    def _():
        run_scr[...] = jnp.zeros_like(run_scr)

    t_row = lax.broadcasted_iota(jnp.int32, (RT_SUB, RT_SUB), 0)
    t_col = lax.broadcasted_iota(jnp.int32, (RT_SUB, RT_SUB), 1)
    earlier = (t_col < t_row).astype(BF16)
    wr_hi, wr_lo = _split2(wr_ref[0])

    def body(r, run):
        rows = pl.ds(pl.multiple_of(r * RT_SUB, RT_SUB), RT_SUB)
        h = _modulated_norm(x_ref[rows, :], nw_ref[...], sc_ref[...], sh_ref[...], i * RT_TM + r * RT_SUB)
        h_ref[rows, :] = h
        logit = _dot_split(h, wr_hi, wr_lo) + br_ref[0]
        lane = lax.broadcasted_iota(jnp.int32, logit.shape, 1)
        big = jnp.int32(LANES)
        is_g = lane < MOE_GROUPS
        lg = jnp.where(is_g, logit, -jnp.inf)
        gmax = jnp.max(lg, axis=-1, keepdims=True)
        g_sel = jnp.min(jnp.where(is_g & (lg == gmax), lane, big), axis=-1, keepdims=True)
        g_w = 1.0 / jnp.sum(jnp.exp(lg - gmax), axis=-1, keepdims=True)
        lo = MOE_GROUPS + g_sel * MOE_PER_GROUP
        in_g = (lane >= lo) & (lane < lo + MOE_PER_GROUP)
        le = jnp.where(in_g, logit, -jnp.inf)
        v1 = jnp.max(le, axis=-1, keepdims=True)
        i1 = jnp.min(jnp.where(in_g & (le == v1), lane, big), axis=-1, keepdims=True)
        le2 = jnp.where(lane == i1, -jnp.inf, le)
        v2 = jnp.max(le2, axis=-1, keepdims=True)
        i2 = jnp.min(jnp.where(in_g & (lane != i1) & (le2 == v2), lane, big), axis=-1, keepdims=True)
        e2 = jnp.exp(v2 - v1)
        w1 = g_w / (1.0 + e2)
        w2 = g_w * e2 / (1.0 + e2)
        oh1 = (lane == i1 - MOE_GROUPS).astype(F32)
        oh2 = (lane == i2 - MOE_GROUPS).astype(F32)
        oh = oh1 + oh2
        before = _dot(earlier, oh.astype(BF16)) + run
        rank1 = jnp.sum(before * oh1, axis=-1, keepdims=True)
        rank2 = jnp.sum(before * oh2, axis=-1, keepdims=True)
        out = jnp.where(lane == 0, (i1 - MOE_GROUPS).astype(F32),
                        jnp.where(lane == 1, (i2 - MOE_GROUPS).astype(F32),
                                  jnp.where(lane == 2, w1,
                                            jnp.where(lane == 3, w2,
                                                      jnp.where(lane == 4, rank1,
                                                                jnp.where(lane == 5, rank2, 0.0))))))
        r_ref[rows, :] = out
        return run + jnp.sum(oh, axis=0, keepdims=True)
    run = lax.fori_loop(0, RT_TM // RT_SUB, body, run_scr[...])
    run_scr[...] = run
    cnt_ref[...] = jnp.broadcast_to(run, cnt_ref.shape)


def moe_router(xa, nw, sc2, sh2, w_route, b_route, layer):
    t_all, d = xa.shape
    return pl.pallas_call(
        _router_kernel,
        grid=(t_all // RT_TM,),
        in_specs=[pl.BlockSpec((RT_TM, d), lambda i: (i, 0)),
                  pl.BlockSpec((1, d), lambda i: (0, 0)),
                  pl.BlockSpec((2, d), lambda i: (0, 0)),
                  pl.BlockSpec((2, d), lambda i: (0, 0)),
                  pl.BlockSpec((1, d, LANES), lambda i: (layer, 0, 0)),
                  pl.BlockSpec((1, 1, LANES), lambda i: (layer, 0, 0))],
        out_specs=[pl.BlockSpec((RT_TM, d), lambda i: (i, 0)),
                   pl.BlockSpec((RT_TM, LANES), lambda i: (i, 0)),
                   pl.BlockSpec((8, LANES), lambda i: (0, 0))],
        out_shape=[jax.ShapeDtypeStruct((t_all, d), F32),
                   jax.ShapeDtypeStruct((t_all, LANES), F32),
                   jax.ShapeDtypeStruct((8, LANES), F32)],
        scratch_shapes=[pltpu.VMEM((1, LANES), F32)],
        compiler_params=_cparams(("arbitrary",)),
        name="moe_router",
    )(xa, nw, sc2, sh2, w_route, b_route)


EXP_SLOTS = 3


W_CHUNKS = 12
WA_ROWS = D_MODEL // 4
WB_ROWS = MOE_FF // 4
W_AHEAD = 3
SCHED_W = 5


def _expert_kernel(be_ref, nu_ref, st_ref, sc_ref, h_hbm, w1_hbm, w3_hbm, w2_hbm, o_ref,
                   xbuf, wb1, wb3, wb2, stg_a, stg_b, sem, sem_a, sem_b, *, layer):
    b = pl.program_id(0)
    n_used = nu_ref[0]
    cur = sc_ref[SCHED_W * b]
    nxt = sc_ref[SCHED_W * b + 1]
    c0 = sc_ref[SCHED_W * b + 2]
    c1 = sc_ref[SCHED_W * b + 3]

    def chunk_copy(e, k, kind):
        st = k % W_AHEAD
        if kind == 2:
            src = w2_hbm.at[layer, e, pl.ds(pl.multiple_of((k - 8) * WB_ROWS, WB_ROWS), WB_ROWS), :]
            return pltpu.make_async_copy(src, stg_b.at[st], sem_b.at[st])
        w_hbm = w1_hbm if kind == 0 else w3_hbm
        src = w_hbm.at[layer, e, pl.ds(pl.multiple_of((k - 4 * kind) * WA_ROWS, WA_ROWS), WA_ROWS), :]
        return pltpu.make_async_copy(src, stg_a.at[st], sem_a.at[st])

    def by_kind(k, fn):
        if isinstance(k, int):
            fn(k // 4)
            return

        @pl.when(k < 4)
        def _():
            fn(0)

        @pl.when((k >= 4) & (k < 8))
        def _():
            fn(1)

        @pl.when(k >= 8)
        def _():
            fn(2)

    def start_chunk(e, k):
        by_kind(k, lambda kind: chunk_copy(e, k, kind).start())

    def land_chunk(e, k, slot):
        def fn(kind):
            chunk_copy(e, k, kind).wait()
            if kind == 2:
                rows = pl.ds(pl.multiple_of((k - 8) * WB_ROWS, WB_ROWS), WB_ROWS)
                wb2[slot, rows, :] = stg_b[k % W_AHEAD].astype(BF16)
            else:
                rows = pl.ds(pl.multiple_of((k - 4 * kind) * WA_ROWS, WA_ROWS), WA_ROWS)
                dst = wb1 if kind == 0 else wb3
                dst[slot, rows, :] = stg_a[k % W_AHEAD].astype(BF16)
        by_kind(k, fn)

    def stream(e, slot, k0, k1):
        def body(k, carry):
            land_chunk(e, k, slot)

            @pl.when(k + W_AHEAD < W_CHUNKS)
            def _():
                start_chunk(e, k + W_AHEAD)
            return carry
        lax.fori_loop(k0, k1, body, 0)

    @pl.when(b == 0)
    def _():
        e0 = be_ref[0]
        for k in range(W_AHEAD):
            start_chunk(e0, k)
        stream(e0, cur, 0, W_CHUNKS)

    @pl.when(sc_ref[SCHED_W * b + 4] > 0)
    def _():
        for k in range(W_AHEAD):
            start_chunk(nxt, k)

    stream(nxt, 1 - cur, c0, c1)

    def row_copy(blk, slot, t):
        tok = st_ref[blk * MOE_BM + t]
        return pltpu.make_async_copy(h_hbm.at[pl.ds(tok, 1), :], xbuf.at[slot, pl.ds(t, 1), :], sem.at[slot])

    @pl.when(b == 0)
    def _():
        def body(t, carry):
            row_copy(0, 0, t).start()
            row_copy(1, 1, t).start()
            return carry
        lax.fori_loop(0, MOE_BM, body, 0)

    @pl.when(b + 2 < n_used)
    def _():
        for t in range(MOE_BM):
            row_copy(b + 2, (b + 2) % EXP_SLOTS, t).start()

    @pl.when((b == 1) & (n_used == 1))
    def _():
        pltpu.make_async_copy(h_hbm.at[pl.ds(0, MOE_BM), :], xbuf.at[1], sem.at[1]).wait()

    @pl.when(b < n_used)
    def _():
        slot = b % EXP_SLOTS
        pltpu.make_async_copy(h_hbm.at[pl.ds(0, MOE_BM), :], xbuf.at[slot], sem.at[slot]).wait()
        x = xbuf[slot].astype(BF16)
        hid = _silu(_dot(x, wb1[cur])) * _dot(x, wb3[cur])
        o_ref[...] = _dot(hid.astype(BF16), wb2[cur])

    @pl.when(b >= n_used)
    def _():
        o_ref[...] = jnp.zeros_like(o_ref)


def expert_blocks(block_e, n_used, slot_tok, sched, h2, w1, w3, w2, layer):
    d = h2.shape[1]
    ff = w1.shape[-1]
    n_slots = slot_tok.shape[0]
    n_blocks = n_slots // MOE_BM
    any_spec = pl.BlockSpec(memory_space=pl.ANY)
    return pl.pallas_call(
        functools.partial(_expert_kernel, layer=layer),
        grid_spec=pltpu.PrefetchScalarGridSpec(
            num_scalar_prefetch=4,
            grid=(n_blocks,),
            in_specs=[any_spec, any_spec, any_spec, any_spec],
            out_specs=pl.BlockSpec((MOE_BM, d), lambda b, be, nu, st, sc: (b, 0)),
            scratch_shapes=[pltpu.VMEM((EXP_SLOTS, MOE_BM, d), F32),
                            pltpu.VMEM((2, d, ff), BF16),
                            pltpu.VMEM((2, d, ff), BF16),
                            pltpu.VMEM((2, ff, d), BF16),
                            pltpu.VMEM((W_AHEAD, WA_ROWS, ff), F32),
                            pltpu.VMEM((W_AHEAD, WB_ROWS, d), F32),
                            pltpu.SemaphoreType.DMA((EXP_SLOTS,)),
                            pltpu.SemaphoreType.DMA((W_AHEAD,)),
                            pltpu.SemaphoreType.DMA((W_AHEAD,))]),
        out_shape=jax.ShapeDtypeStruct((n_slots, d), F32),
        compiler_params=_cparams(("arbitrary",)),
        name="moe_experts",
    )(block_e, n_used, slot_tok, sched, h2, w1, w3, w2)


def _combine_kernel(dest_ref, x_ref, r_ref, g_ref, fw_ref, y_hbm, o_ref, ybuf, sem, *,
                    row_block0, n_tiles, final_norm):
    i = pl.program_id(0)

    def row_copy(tile, slot, t, k):
        row = dest_ref[(tile + row_block0) * (2 * CHUNK) + 2 * t + k]
        return pltpu.make_async_copy(y_hbm.at[pl.ds(row, 1), :],
                                     ybuf.at[slot, pl.ds(k * CHUNK + t, 1), :], sem.at[slot])

    @pl.when(i == 0)
    def _():
        def body(t, carry):
            row_copy(0, 0, t, 0).start()
            row_copy(0, 0, t, 1).start()
            return carry
        lax.fori_loop(0, CHUNK, body, 0)

    @pl.when(i + 1 < n_tiles)
    def _():
        for t in range(CHUNK):
            row_copy(i + 1, (i + 1) % 2, t, 0).start()
            row_copy(i + 1, (i + 1) % 2, t, 1).start()

    slot = i % 2
    pltpu.make_async_copy(y_hbm.at[pl.ds(0, 2 * CHUNK), :], ybuf.at[slot], sem.at[slot]).wait()

    rt = r_ref[...]
    y = rt[:, 2:3] * ybuf[slot, 0:CHUNK, :] + rt[:, 3:4] * ybuf[slot, CHUNK:2 * CHUNK, :]
    is_ctx = ((i + row_block0) * CHUNK + lax.broadcasted_iota(jnp.int32, (CHUNK, 1), 0)) < CTX
    gate2 = jnp.where(is_ctx, g_ref[1:2, :], g_ref[0:1, :])
    x = x_ref[...] + gate2 * y
    if final_norm:
        x = x * lax.rsqrt(jnp.mean(x * x, axis=-1, keepdims=True) + EPS) * fw_ref[...]
    o_ref[...] = x


def moe_combine(dest, xa, route, g2, final_w, y, final_norm):
    t_all, d = xa.shape
    rb0 = CTX // CHUNK if final_norm else 0
    n_out = t_all // CHUNK - rb0
    return pl.pallas_call(
        functools.partial(_combine_kernel, row_block0=rb0, n_tiles=n_out, final_norm=final_norm),
        grid_spec=pltpu.PrefetchScalarGridSpec(
            num_scalar_prefetch=1,
            grid=(n_out,),
            in_specs=[pl.BlockSpec((CHUNK, d), lambda i, ds: (i + rb0, 0)),
                      pl.BlockSpec((CHUNK, LANES), lambda i, ds: (i + rb0, 0)),
                      pl.BlockSpec((2, d), lambda i, ds: (0, 0)),
                      pl.BlockSpec((1, d), lambda i, ds: (0, 0)),
                      pl.BlockSpec(memory_space=pl.ANY)],
            out_specs=pl.BlockSpec((CHUNK, d), lambda i, ds: (i, 0)),
            scratch_shapes=[pltpu.VMEM((2, 2 * CHUNK, d), F32),
                            pltpu.SemaphoreType.DMA((2,))]),
        out_shape=jax.ShapeDtypeStruct((n_out * CHUNK, d), F32),
        compiler_params=_cparams(("arbitrary",)),
        name="moe_combine",
    )(dest, xa, route, g2, final_w, y)


def moe_dispatch(route, counts):
    n = route.shape[0]
    n_assign = 2 * n
    e_flat = route[:, 0:2].astype(jnp.int32).reshape(-1)
    rank_flat = route[:, 4:6].astype(jnp.int32).reshape(-1)
    padded = (counts + MOE_BM - 1) // MOE_BM * MOE_BM
    pends = jnp.cumsum(padded)
    pstarts = pends - padded
    experts = jnp.arange(N_EXPERTS, dtype=jnp.int32)
    dest = jnp.sum(jnp.where(e_flat[:, None] == experts[None, :], pstarts[None, :], 0), axis=1) + rank_flat
    n_blocks = -(-n_assign // MOE_BM) + N_EXPERTS
    n_slots = n_blocks * MOE_BM
    tok_flat = jnp.arange(n_assign, dtype=jnp.int32) // 2
    slot_tok = jnp.zeros((n_slots,), jnp.int32).at[dest].set(tok_flat)
    block_start = jnp.arange(n_blocks, dtype=jnp.int32) * MOE_BM
    block_e = jnp.minimum(jnp.sum((pends[None, :] <= block_start[:, None]).astype(jnp.int32), axis=1),
                          N_EXPERTS - 1)
    n_used = (pends[-1] // MOE_BM).astype(jnp.int32).reshape(1)

    nb = padded // MOE_BM
    nonempty = nb > 0
    order = jnp.cumsum(nonempty.astype(jnp.int32)) - 1
    later = nonempty[None, :] & (experts[None, :] > experts[:, None])
    nxt_e = jnp.min(jnp.where(later, experts[None, :], N_EXPERTS), axis=1)
    nxt_e = jnp.where(nxt_e == N_EXPERTS, -1, nxt_e)
    first_blk = pstarts // MOE_BM
    i_blk = jnp.arange(n_blocks, dtype=jnp.int32) - first_blk[block_e]
    nb_b = nb[block_e]
    live = (jnp.arange(n_blocks) < n_used[0]) & (nxt_e[block_e] >= 0)
    i_eff = jnp.where(nb_b == 1, i_blk + 1, i_blk)
    steps = jnp.maximum(nb_b - 1, 1)
    c0 = jnp.where(live, jnp.maximum(W_CHUNKS * (i_eff - 1) // steps, 0), 0)
    c1 = jnp.where(live, jnp.maximum(W_CHUNKS * i_eff // steps, 0), 0)
    kick = (live & (i_blk == 0)).astype(jnp.int32)
    sched = jnp.stack([order[block_e] % 2, nxt_e[block_e], c0, c1, kick], axis=1).astype(jnp.int32).reshape(-1)
    return dest, slot_tok, block_e, n_used, sched


def hier_moe(xa, nw, sc2, sh2, g2, w_route, b_route, w1, w3, w2, final_w, layer, final_norm):
    h2, route, cnt = moe_router(xa, nw, sc2, sh2, w_route, b_route, layer)
    dest, slot_tok, block_e, n_used, sched = moe_dispatch(route, cnt[0, :N_EXPERTS].astype(jnp.int32))
    y = expert_blocks(block_e, n_used, slot_tok, sched, h2, w1, w3, w2, layer)
    return moe_combine(dest, xa, route, g2, final_w, y, final_norm)


def rope_tables(t_lat):
    nf = ML_DH // 4
    inv = ROPE_THETA ** (-jnp.arange(nf, dtype=F32) / nf)
    t_idx = jnp.arange(t_lat)
    ang_r = (t_idx // GRID_W).astype(F32)[:, None] * inv
    ang_c = (t_idx % GRID_W).astype(F32)[:, None] * inv
    cos = jnp.concatenate([jnp.cos(ang_r), jnp.cos(ang_c)] * 2, axis=-1)
    sin = jnp.concatenate([-jnp.sin(ang_r), -jnp.sin(ang_c), jnp.sin(ang_r), jnp.sin(ang_c)], axis=-1)
    cos = jnp.concatenate([jnp.ones((CTX, ML_DH), F32), cos], axis=0)
    sin = jnp.concatenate([jnp.zeros((CTX, ML_DH), F32), sin], axis=0)
    return cos, sin


def _pad_lanes(v):
    return jnp.pad(v, [(0, 0)] * (v.ndim - 1) + [(0, LANES - v.shape[-1])])


def kernel(x, c, ctx, c_ctx, w_mod, b_mod, norm1_w, norm2_w, w_in, w_out, na_rpb, ml_gate_b, ml_norm_w, ssm_conv_w, ssm_conv_b, ssm_dt_bias, ssm_a_log, ssm_d, ssm_norm_w, router_g_w, router_g_b, router_e_w, router_e_b, moe_w1, moe_w3, moe_w2, final_norm_w):
    depth = w_mod.shape[0]
    t_lat = x.shape[1]
    d = D_MODEL
    xa = jnp.concatenate([ctx[0], x[0]], axis=0)

    cond = jnp.zeros((8, d), F32).at[0].set(c[0]).at[1].set(c_ctx)
    mod = adaln_modulation(cond, w_mod, b_mod)[:, 0:2, :].reshape(depth, 2, 6, d)
    cos_tab, sin_tab = rope_tables(t_lat)

    o_mq = 3 * NA_W
    o_mg = o_mq + 4 * ML_W
    o_z = o_mg + 4 * ML_H
    o_xbc = o_z + SSM_W
    o_dt = o_xbc + CONV_CH
    w_qk = w_in[:, :, o_mq:o_mq + 2 * ML_W].reshape(depth, d, 2 * ML_H, 2, 2, ML_DH // 4)
    w_qk = jnp.swapaxes(w_qk, 3, 4).reshape(depth, d, 2 * ML_W)
    w_main = jnp.concatenate([w_in[:, :, 0:o_mq], w_in[:, :, o_xbc:o_dt], w_qk, w_in[:, :, o_mq + 2 * ML_W:o_mg],
                              w_in[:, :, o_z:o_xbc]], axis=2).astype(BF16)
    w_mg, w_dt = w_in[:, :, o_mg:o_z], w_in[:, :, o_dt:]
    w_gate = _pad_lanes(jnp.concatenate([w_mg[:, :, 0:8], w_dt[:, :, 0:8], w_mg[:, :, 8:16], w_dt[:, :, 8:16]],
                                        axis=2))
    w_out_b = w_out.astype(BF16)
    w_route = _pad_lanes(jnp.concatenate([router_g_w, router_e_w], axis=2))
    b_route = _pad_lanes(jnp.concatenate([router_g_b, router_e_b], axis=1))[:, None, :]
    gb4 = ml_gate_b.reshape(depth, 2, 2 * ML_H)
    gbias = _pad_lanes(jnp.concatenate([gb4, ssm_dt_bias], axis=2))
    alog = _pad_lanes(jnp.concatenate([jnp.zeros_like(ssm_a_log), ssm_a_log], axis=2))
    d_skip_vec = jnp.repeat(ssm_d, SSM_P, axis=1)[:, None, :]
    conv_b = ssm_conv_b[:, None, :]

    out = None
    for l in range(depth):
        last = l == depth - 1
        sh1, sc1, g1, sh2, sc2, g2 = (mod[l, :, p, :] for p in range(6))

        qkv, rest, gates, gates_t = in_projection(xa, norm1_w[l][None], sc1, sh1, w_main, w_gate, l)
        o_na = neighbourhood_attention(qkv, na_bias_table(na_rpb[l]))
        hs = mlstm_scan(rest, gates, gates_t, gbias[l][:, None, :], gbias[l][:, :, None], cos_tab, sin_tab)
        xbc_act = ssd_conv(rest, ssm_conv_w, conv_b, l)
        ys = ssd_scan(xbc_act, gates, gates_t, gbias[l][:, None, :], gbias[l][:, :, None],
                      alog[l][:, None, :], alog[l][:, :, None])
        xa = out_projection(o_na, hs, rest, ys, xbc_act, ml_norm_w[:, None, :], d_skip_vec,
                            ssm_norm_w[:, None, :], w_out_b, xa, g1, l)
        res = hier_moe(xa, norm2_w[l][None], sc2, sh2, g2, w_route, b_route, moe_w1, moe_w3, moe_w2,
                       final_norm_w[None], l, last)
        if last:
            out = res
        else:
            xa = res
    return out[None]
```
